```python
import math
import jax, jax.numpy as jnp
from jax import lax
import numpy as np

D_MODEL = 1024
BATCH = 4
SEQ = 8192
DEPTH = 4

GRID_W = 64
CTX_LEN = 256
N_MIXERS = 2
S5_GROUP = 16
S5_GROUPS = D_MODEL // S5_GROUP
S5_STATE = 64
POOL_WINDOWS = (2, 4, 8, 16)
POOL_GROUPS = len(POOL_WINDOWS)
POOL_CH = D_MODEL // POOL_GROUPS
D_FF = 4 * D_MODEL
N_S5 = (DEPTH + N_MIXERS - 1) // N_MIXERS
N_POOL = DEPTH // N_MIXERS
DT_MIN = 1e-3
DT_MAX = 1e-1
EPS = 1e-6

kernel_name = "hybrid_s5_pool_prefix_dit"


def _rmsnorm(x, g):
    xf = x.astype(jnp.float32)
    y = xf * lax.rsqrt(jnp.mean(xf * xf, axis=-1, keepdims=True) + EPS)
    return (y * g.astype(jnp.float32)).astype(x.dtype)


def _modulate(h, shift, scale):
    return h * (1 + scale) + shift


def _mlp(h, w1, b1, w2, b2):
    a = jax.nn.relu(h @ w1 + b1)
    return (a * a) @ w2 + b2


def _s5_discretize(a_re, a_im, log_dt, b_re, b_im):
    f = jnp.float32
    a_re, a_im, b_re, b_im = a_re.astype(f), a_im.astype(f), b_re.astype(f), b_im.astype(f)
    dt = jnp.exp(log_dt.astype(f))[:, None]
    da_re, da_im = a_re * dt, a_im * dt
    mag = jnp.exp(da_re)
    lb_re, lb_im = mag * jnp.cos(da_im), mag * jnp.sin(da_im)
    den = a_re * a_re + a_im * a_im
    num_re, num_im = lb_re - 1.0, lb_im
    k_re = (num_re * a_re + num_im * a_im) / den
    k_im = (num_im * a_re - num_re * a_im) / den
    bb_re = k_re[..., None] * b_re - k_im[..., None] * b_im
    bb_im = k_re[..., None] * b_im + k_im[..., None] * b_re
    return lb_re, lb_im, da_re, da_im, bb_re, bb_im


def _apply_b(ug, bb_re, bb_im):
    return (jnp.einsum('blgh,gph->blgp', ug, bb_re),
            jnp.einsum('blgh,gph->blgp', ug, bb_im))


def _diag_scan(bu_re, bu_im, da_re, da_im, reverse):
    def combine(a, b):
        n1, r1, i1 = a
        n2, r2, i2 = b
        mag = jnp.exp(n2 * da_re)
        ang = n2 * da_im
        p_re, p_im = mag * jnp.cos(ang), mag * jnp.sin(ang)
        return (n1 + n2, p_re * r1 - p_im * i1 + r2, p_re * i1 + p_im * r1 + i2)
    n = jnp.ones((1, bu_re.shape[1], 1, 1), jnp.float32)
    _, h_re, h_im = lax.associative_scan(combine, (n, bu_re, bu_im), reverse=reverse, axis=1)
    return h_re, h_im


def _readout(h_re, h_im, c_re, c_im):
    f = jnp.float32
    return (jnp.einsum('blgp,ghp->blgh', h_re, c_re.astype(f))
            - jnp.einsum('blgp,ghp->blgh', h_im, c_im.astype(f)))


def _s5_mixer(u, uc, a_re, a_im, log_dt, b_re, b_im, c_re, c_im, d_skip, glu_w, glu_b, ctx_out):
    f = jnp.float32
    bsz, n_lat, _ = u.shape
    n_ctx = uc.shape[1]
    ug = u.astype(f).reshape(bsz, n_lat, S5_GROUPS, S5_GROUP)
    ucg = uc.astype(f).reshape(bsz, n_ctx, S5_GROUPS, S5_GROUP)
    y = jnp.zeros_like(ug)
    yc = jnp.zeros_like(ucg) if ctx_out else None
    for d in range(2):
        rev = d == 1
        lb_re, lb_im, da_re, da_im, bb_re, bb_im = _s5_discretize(
            a_re[d], a_im[d], log_dt[d], b_re[d], b_im[d])
        cu_re, cu_im = _apply_b(ucg, bb_re, bb_im)
        hc_re, hc_im = _diag_scan(cu_re, cu_im, da_re, da_im, rev)
        end = 0 if rev else -1
        h0_re, h0_im = hc_re[:, end], hc_im[:, end]
        if ctx_out:
            yc = yc + _readout(hc_re, hc_im, c_re[d], c_im[d])
        lu_re, lu_im = _apply_b(ug, bb_re, bb_im)
        start = -1 if rev else 0
        lu_re = lu_re.at[:, start].add(lb_re * h0_re - lb_im * h0_im)
        lu_im = lu_im.at[:, start].add(lb_re * h0_im + lb_im * h0_re)
        h_re, h_im = _diag_scan(lu_re, lu_im, da_re, da_im, rev)
        y = y + _readout(h_re, h_im, c_re[d], c_im[d])

    def post(yy, uu):
        out = yy.reshape(uu.shape) + d_skip.astype(f) * uu.astype(f)
        z = jax.nn.gelu(out).astype(uu.dtype)
        return z * jax.nn.sigmoid(z @ glu_w + glu_b)

    return post(y, u), (post(yc, uc) if ctx_out else None)


def _window_bounds(n, w):
    t = jnp.arange(n)
    return jnp.maximum(t - w // 2, 0), jnp.minimum(t + w - w // 2, n)


def _pool_grid(xg, w, rows):
    b, n, ch = xg.shape
    g = xg.reshape(b, rows, GRID_W, ch)
    s = jnp.cumsum(jnp.cumsum(g, axis=1), axis=2)
    s = jnp.pad(s, ((0, 0), (1, 0), (1, 0), (0, 0)))
    r_lo, r_hi = _window_bounds(rows, w)
    c_lo, c_hi = _window_bounds(GRID_W, w)
    s_hi = jnp.take(s, r_hi, axis=1)
    s_lo = jnp.take(s, r_lo, axis=1)
    tot = (jnp.take(s_hi, c_hi, axis=2) - jnp.take(s_hi, c_lo, axis=2)
           - jnp.take(s_lo, c_hi, axis=2) + jnp.take(s_lo, c_lo, axis=2))
    cnt = ((r_hi - r_lo)[:, None] * (c_hi - c_lo)[None, :]).astype(jnp.float32)
    return (tot / cnt[None, :, :, None] - g).reshape(b, n, ch)


def _pool_seq(xs, w):
    n = xs.shape[1]
    s = jnp.pad(jnp.cumsum(xs, axis=1), ((0, 0), (1, 0), (0, 0)))
    lo, hi = _window_bounds(n, w)
    cnt = (hi - lo).astype(jnp.float32)[None, :, None]
    return (jnp.take(s, hi, axis=1) - jnp.take(s, lo, axis=1)) / cnt - xs


def _pool_mixer(u, pool_w, pool_scale, rows):
    uf = u.astype(jnp.float32)
    parts = []
    for gi, w in enumerate(POOL_WINDOWS):
        xg = uf[..., gi * POOL_CH:(gi + 1) * POOL_CH]
        parts.append(_pool_grid(xg, w, rows) if rows is not None else _pool_seq(xg, w))
    p = jnp.stack(parts, axis=2).astype(u.dtype)
    y = jnp.einsum('blgc,gcd->blgd', p, pool_w).reshape(u.shape)
    return y * pool_scale


def setup_inputs(seed: int = 0) -> dict:
    key = jax.random.key(seed)
    ks = jax.random.split(key, 32)
    f = jnp.float32
    G, P, H, D = S5_GROUPS, S5_STATE, S5_GROUP, D_MODEL

    def nrm(k, shape, s):
        return jax.random.normal(k, shape, f) * s

    n_idx = jnp.arange(P, dtype=f)
    return {
        "x": nrm(ks[0], (BATCH, SEQ, D), 1.0),
        "c": nrm(ks[1], (BATCH, D), 1.0),
        "ctx": nrm(ks[2], (BATCH, CTX_LEN, D), 1.0),
        "c_ctx": nrm(ks[3], (D,), 1.0),
        "ada_w": nrm(ks[4], (DEPTH, D, 6 * D), 0.5 * D ** -0.5),
        "ada_b": nrm(ks[5], (DEPTH, 6 * D), 0.02),
        "norm1_g": 1.0 + nrm(ks[6], (DEPTH, D), 0.05),
        "norm2_g": 1.0 + nrm(ks[7], (DEPTH, D), 0.05),
        "s5_a_re": -0.5 + nrm(ks[8], (N_S5, 2, G, P), 0.02),
        "s5_a_im": math.pi * n_idx + nrm(ks[9], (N_S5, 2, G, P), 0.02),
        "s5_log_dt": jax.random.uniform(ks[10], (N_S5, 2, G), f,
                                         math.log(DT_MIN), math.log(DT_MAX)),
        "s5_b_re": nrm(ks[11], (N_S5, 2, G, P, H), H ** -0.5),
        "s5_b_im": nrm(ks[12], (N_S5, 2, G, P, H), H ** -0.5),
        "s5_c_re": nrm(ks[13], (N_S5, 2, G, H, P), P ** -0.5),
        "s5_c_im": nrm(ks[14], (N_S5, 2, G, H, P), P ** -0.5),
        "s5_d": nrm(ks[15], (N_S5, D), 1.0),
        "s5_glu_w": nrm(ks[16], (N_S5, D, D), D ** -0.5),
        "s5_glu_b": nrm(ks[17], (N_S5, D), 0.02),
        "pool_w": nrm(ks[18], (N_POOL, POOL_GROUPS, POOL_CH, POOL_CH), POOL_CH ** -0.5),
        "pool_scale": 1.0 + nrm(ks[19], (N_POOL, D), 0.05),
        "mlp_w1": nrm(ks[20], (DEPTH, D, D_FF), D ** -0.5),
        "mlp_b1": nrm(ks[21], (DEPTH, D_FF), 0.02),
        "mlp_w2": nrm(ks[22], (DEPTH, D_FF, D), D_FF ** -0.5),
        "mlp_b2": nrm(ks[23], (DEPTH, D), 0.02),
        "final_g": 1.0 + nrm(ks[24], (D,), 0.05),
    }


def reference(x, c, ctx, c_ctx, ada_w, ada_b, norm1_g, norm2_g,
              s5_a_re, s5_a_im, s5_log_dt, s5_b_re, s5_b_im, s5_c_re, s5_c_im,
              s5_d, s5_glu_w, s5_glu_b, pool_w, pool_scale,
              mlp_w1, mlp_b1, mlp_w2, mlp_b2, final_g):
    n_tok = x.shape[1]
    rows = n_tok // GRID_W
    last_ctx_reader = ((DEPTH - 1) // N_MIXERS) * N_MIXERS
    silu_c = jax.nn.silu(c)
    silu_cc = jax.nn.silu(c_ctx)
    h_ctx = ctx
    for i in range(DEPTH):
        ctx_in = i <= last_ctx_reader
        ctx_out = i < last_ctx_reader
        j = i // N_MIXERS
        mod = silu_c @ ada_w[i] + ada_b[i]
        sh1, sc1, g1, sh2, sc2, g2 = jnp.split(mod[:, None, :], 6, axis=-1)
        xn = _modulate(_rmsnorm(x, norm1_g[i]), sh1, sc1)
        if ctx_in:
            mod_c = silu_cc @ ada_w[i] + ada_b[i]
            csh1, csc1, cg1, csh2, csc2, cg2 = jnp.split(mod_c, 6)
            cn = _modulate(_rmsnorm(h_ctx, norm1_g[i]), csh1, csc1)
        if i % N_MIXERS == 0:
            y, yc = _s5_mixer(xn, cn, s5_a_re[j], s5_a_im[j], s5_log_dt[j],
                              s5_b_re[j], s5_b_im[j], s5_c_re[j], s5_c_im[j],
                              s5_d[j], s5_glu_w[j], s5_glu_b[j], ctx_out)
        else:
            y = _pool_mixer(xn, pool_w[j], pool_scale[j], rows)
            yc = _pool_mixer(cn, pool_w[j], pool_scale[j], None) if ctx_out else None
        x = x + g1 * y
        x = x + g2 * _mlp(_modulate(_rmsnorm(x, norm2_g[i]), sh2, sc2),
                          mlp_w1[i], mlp_b1[i], mlp_w2[i], mlp_b2[i])
        if ctx_out:
            h_ctx = h_ctx + cg1 * yc
            h_ctx = h_ctx + cg2 * _mlp(_modulate(_rmsnorm(h_ctx, norm2_g[i]), csh2, csc2),
                                       mlp_w1[i], mlp_b1[i], mlp_w2[i], mlp_b2[i])
    return _rmsnorm(x, final_g)
```

```python
import functools
import math

import numpy as np
import jax
import jax.numpy as jnp
from jax import lax
from jax.experimental import pallas as pl
from jax.experimental.pallas import tpu as pltpu

D_MODEL = 1024
GRID_W = 64
S5_GROUP = 16
S5_GROUPS = D_MODEL // S5_GROUP
S5_STATE = 64
POOL_WINDOWS = (2, 4, 8, 16)
POOL_CH = D_MODEL // len(POOL_WINDOWS)
D_FF = 4 * D_MODEL
N_MIXERS = 2
EPS = 1e-6

S5_CHUNK = 32
S5_ROWS = S5_CHUNK * S5_GROUP
S5_NSTATE = 4 * S5_STATE
CTX_SEG = 16
CTX_OFF = 2
V7X_VMEM_LIMIT_BYTES = 56 * 1024 * 1024

_F32 = jnp.float32
_BF16 = jnp.bfloat16


def _cparams(sem):
    return pltpu.CompilerParams(dimension_semantics=sem, vmem_limit_bytes=V7X_VMEM_LIMIT_BYTES)


def _resident(shape):
    nd = len(shape)
    return pl.BlockSpec(shape, lambda *_: (0,) * nd, pipeline_mode=pl.Buffered(1))


def _mod_kernel(cc_ref, w_ref, b_ref, o_ref):
    c = cc_ref[...]
    a = c * jax.nn.sigmoid(c)
    o_ref[0] = jnp.dot(a, w_ref[0], preferred_element_type=_F32,
                       precision=lax.Precision.HIGHEST) + b_ref[0]


def _modulation(cc, ada_w, ada_b):
    depth, d, n = ada_w.shape
    tn = 1536
    return pl.pallas_call(
        _mod_kernel,
        out_shape=jax.ShapeDtypeStruct((depth, 8, n), _F32),
        grid=(depth, n // tn),
        in_specs=[pl.BlockSpec((8, d), lambda i, j: (0, 0)),
                  pl.BlockSpec((1, d, tn), lambda i, j: (i, 0, j)),
                  pl.BlockSpec((1, 1, tn), lambda i, j: (i, 0, j))],
        out_specs=pl.BlockSpec((1, 8, tn), lambda i, j: (i, 0, j)),
        compiler_params=_cparams(("arbitrary", "arbitrary")),
        name="modulation",
    )(cc, ada_w, ada_b.reshape(depth, 1, n))


def _rms_mod(xf, g, shift, scale):
    ms = jnp.mean(xf * xf, axis=-1, keepdims=True)
    return (xf * lax.rsqrt(ms + EPS) * g) * (1.0 + scale) + shift


def _mod_slices(m):
    d = D_MODEL
    return tuple(m[:, k * d:(k + 1) * d] for k in range(6))


def _gelu_tanh(x):
    return 0.5 * x * (1.0 + jnp.tanh(math.sqrt(2.0 / math.pi) * (x + 0.044715 * (x * x * x))))


def _prenorm_kernel(x_ref, mod_ref, g_ref, o_ref):
    sh1, sc1 = _mod_slices(mod_ref[0])[:2]
    o_ref[0] = _rms_mod(x_ref[0], g_ref[...], sh1, sc1).astype(o_ref.dtype)


def _prenorm(x, mod, mod_row, g, tt):
    b, n, d = x.shape
    row = (lambda bi: bi) if mod_row is None else (lambda bi: mod_row)
    return pl.pallas_call(
        _prenorm_kernel,
        out_shape=jax.ShapeDtypeStruct((b, n, d), _BF16),
        grid=(b, n // tt),
        in_specs=[pl.BlockSpec((1, tt, d), lambda bi, t: (bi, t, 0)),
                  pl.BlockSpec((1, 1, 6 * d), lambda bi, t: (row(bi), 0, 0)),
                  pl.BlockSpec((1, d), lambda bi, t: (0, 0))],
        out_specs=pl.BlockSpec((1, tt, d), lambda bi, t: (bi, t, 0)),
        compiler_params=_cparams(("arbitrary", "arbitrary")),
        name="prenorm",
    )(x, mod, g)


def _seg_scan(s, ar, ai, pos, seg, reverse):
    n = s.shape[1]
    p = S5_STATE
    xr, xi = s[:p], s[p:]

    def shifted(v, dist):
        if reverse:
            r = pltpu.roll(v, n - dist, axis=1)
            return jnp.where(pos < seg - dist, r, 0.0)
        r = pltpu.roll(v, dist, axis=1)
        return jnp.where(pos >= dist, r, 0.0)

    k = 0
    while (1 << k) < seg:
        dist = 1 << k
        a = ar[:, k:k + 1]
        b = ai[:, k:k + 1]
        yr, yi = shifted(xr, dist), shifted(xi, dist)
        xr, xi = xr + (a * yr - b * yi), xi + (a * yi + b * yr)
        k += 1
    return shifted(xr, 1), shifted(xi, 1)


def _s5core_kernel(ut_ref, mt_ref, bst_ref, cst_ref, lp_ref, h0_ref, yt_ref, *h_out, seg, nbatch):
    u = ut_ref[0]
    n = u.shape[1]
    p = S5_STATE
    s = jnp.dot(bst_ref[0], u, preferred_element_type=_F32)
    lane = lax.broadcasted_iota(jnp.int32, (1, n), 1)
    pos = lane & (seg - 1)
    bid = lane >> int(math.log2(seg))
    h0 = h0_ref[0]

    def spread(col0, where_pos):
        def one(rows):
            acc = jnp.zeros((rows.stop - rows.start, n), _F32)
            for b in range(nbatch):
                acc = jnp.where((bid == b) & (pos == where_pos),
                                h0[rows, col0 + b:col0 + b + 1], acc)
            return acc
        return one

    first, last = 0, seg - 1
    sf = s[:2 * p] + spread(8, first)(slice(0, 2 * p))
    sb = s[2 * p:] + spread(8, last)(slice(2 * p, 4 * p))
    lp = lp_ref[0]
    hfr, hfi = _seg_scan(sf, lp[0], lp[1], pos, seg, reverse=False)
    hbr, hbi = _seg_scan(sb, lp[2], lp[3], pos, seg, reverse=True)
    hf = jnp.concatenate([hfr, hfi], axis=0) + spread(0, first)(slice(0, 2 * p))
    hb = jnp.concatenate([hbr, hbi], axis=0) + spread(0, last)(slice(2 * p, 4 * p))
    h = jnp.concatenate([hf, hb], axis=0)
    y = jnp.dot(mt_ref[0], u, preferred_element_type=_F32)
    y = y + jnp.dot(cst_ref[0], h.astype(_BF16), preferred_element_type=_F32)
    yt_ref[0] = y
    if h_out:
        h_out[0][0] = h


def _s5core(ut, mt, bst, cst, lampow, h0, seg, nbatch, want_h):
    g, rows, n = ut.shape
    out_shape = [jax.ShapeDtypeStruct((g, rows, n), _F32)]
    out_specs = [pl.BlockSpec((1, rows, n), lambda i: (i, 0, 0))]
    if want_h:
        out_shape.append(jax.ShapeDtypeStruct((g, S5_NSTATE, n), _F32))
        out_specs.append(pl.BlockSpec((1, S5_NSTATE, n), lambda i: (i, 0, 0)))
    res = pl.pallas_call(
        functools.partial(_s5core_kernel, seg=seg, nbatch=nbatch),
        out_shape=out_shape,
        grid=(g,),
        in_specs=[pl.BlockSpec((1, rows, n), lambda i: (i, 0, 0)),
                  pl.BlockSpec((1, rows, rows), lambda i: (i, 0, 0)),
                  pl.BlockSpec((1, S5_NSTATE, rows), lambda i: (i, 0, 0)),
                  pl.BlockSpec((1, rows, S5_NSTATE), lambda i: (i, 0, 0)),
                  pl.BlockSpec((1, 4, S5_STATE, 128), lambda i: (i, 0, 0, 0)),
                  pl.BlockSpec((1, S5_NSTATE, 128), lambda i: (i, 0, 0))],
        out_specs=out_specs,
        compiler_params=_cparams(("arbitrary",)),
        name="s5core_h" if want_h else "s5core",
    )(ut, mt, bst, cst, lampow, h0)
    return res if want_h else (res[0], None)


def _s5_tables(a_re, a_im, log_dt, b_re, b_im, c_re, c_im, nsteps):
    f = _F32
    t_len = S5_CHUNK
    hp = lax.Precision.HIGHEST
    n = jnp.arange(t_len + 1, dtype=f)
    ks, bss, css, lps = [], [], [], []
    for d in range(2):
        ar, ai = a_re[d].astype(f), a_im[d].astype(f)
        dt = jnp.exp(log_dt[d].astype(f))[:, None]
        da_re, da_im = ar * dt, ai * dt
        mag = jnp.exp(da_re)
        lb_re, lb_im = mag * jnp.cos(da_im), mag * jnp.sin(da_im)
        den = ar * ar + ai * ai
        num_re, num_im = lb_re - 1.0, lb_im
        k_re = (num_re * ar + num_im * ai) / den
        k_im = (num_im * ar - num_re * ai) / den
        bre, bim = b_re[d].astype(f), b_im[d].astype(f)
        bb_re = k_re[..., None] * bre - k_im[..., None] * bim
        bb_im = k_re[..., None] * bim + k_im[..., None] * bre
        pmag = jnp.exp(n[None, None, :] * da_re[..., None])
        pang = n[None, None, :] * da_im[..., None]
        pw_re, pw_im = pmag * jnp.cos(pang), pmag * jnp.sin(pang)
        cre, cim = c_re[d].astype(f), c_im[d].astype(f)
        cl_re = cre[:, None] * pw_re.transpose(0, 2, 1)[:, :t_len, None, :] \
            - cim[:, None] * pw_im.transpose(0, 2, 1)[:, :t_len, None, :]
        cl_im = cre[:, None] * pw_im.transpose(0, 2, 1)[:, :t_len, None, :] \
            + cim[:, None] * pw_re.transpose(0, 2, 1)[:, :t_len, None, :]
        kk = jnp.einsum('gthp,gpk->gthk', cl_re, bb_re, precision=hp) \
            - jnp.einsum('gthp,gpk->gthk', cl_im, bb_im, precision=hp)
        ks.append(kk)
        idx = (t_len - 1 - jnp.arange(t_len)) if d == 0 else jnp.arange(t_len)
        w_re, w_im = pw_re[:, :, idx], pw_im[:, :, idx]
        bs_re = w_re[..., None] * bb_re[:, :, None, :] - w_im[..., None] * bb_im[:, :, None, :]
        bs_im = w_re[..., None] * bb_im[:, :, None, :] + w_im[..., None] * bb_re[:, :, None, :]
        g_ = bs_re.shape[0]
        bss += [bs_re.reshape(g_, S5_STATE, -1), bs_im.reshape(g_, S5_STATE, -1)]
        idx = (jnp.arange(t_len) + 1) if d == 0 else (t_len - jnp.arange(t_len))
        w_re, w_im = pw_re[:, :, idx], pw_im[:, :, idx]
        z_re = cre[:, None] * w_re.transpose(0, 2, 1)[:, :, None, :] \
            - cim[:, None] * w_im.transpose(0, 2, 1)[:, :, None, :]
        z_im = cre[:, None] * w_im.transpose(0, 2, 1)[:, :, None, :] \
            + cim[:, None] * w_re.transpose(0, 2, 1)[:, :, None, :]
        css += [z_re.reshape(g_, -1, S5_STATE), (-z_im).reshape(g_, -1, S5_STATE)]
        kpow = (2.0 ** jnp.arange(nsteps, dtype=f)) * t_len
        lmag = jnp.exp(kpow[None, None, :] * da_re[..., None])
        lang = kpow[None, None, :] * da_im[..., None]
        pad = ((0, 0), (0, 0), (0, 128 - nsteps))
        lps += [jnp.pad(lmag * jnp.cos(lang), pad), jnp.pad(lmag * jnp.sin(lang), pad)]
    kf, kb = ks
    tt = jnp.arange(t_len)
    diff = tt[:, None] - tt[None, :]
    kf_g = kf[:, jnp.clip(diff, 0, t_len - 1)]
    kb_g = kb[:, jnp.clip(-diff, 0, t_len - 1)]
    m5 = jnp.where((diff >= 0)[None, :, :, None, None], kf_g, 0.0) \
        + jnp.where((diff <= 0)[None, :, :, None, None], kb_g, 0.0)
    g_ = m5.shape[0]
    mt = m5.transpose(0, 1, 3, 2, 4).reshape(g_, S5_ROWS, S5_ROWS)
    bst = jnp.concatenate(bss, axis=1)
    cst = jnp.concatenate(css, axis=2)
    lampow = jnp.stack(lps, axis=1)
    return mt.astype(_BF16), bst.astype(_BF16), cst.astype(_BF16), lampow


def _to_ut(xn, seg=None):
    b, n, _ = xn.shape
    c = n // S5_CHUNK
    v = xn.reshape(b, c, S5_CHUNK, S5_GROUPS, S5_GROUP).transpose(3, 2, 4, 0, 1)
    if seg is not None:
        v = jnp.pad(v, ((0, 0),) * 4 + ((CTX_OFF, seg - c - CTX_OFF),))
        c = seg
    return v.reshape(S5_GROUPS, S5_ROWS, b * c)


def _from_yt(yt, b, n, seg=None):
    c = n // S5_CHUNK
    cs = c if seg is None else seg
    v = yt.reshape(S5_GROUPS, S5_CHUNK, S5_GROUP, b, cs)
    if seg is not None:
        v = v[..., CTX_OFF:CTX_OFF + c]
    return v.transpose(3, 4, 1, 0, 2).reshape(b, n, D_MODEL)


def _s5post_kernel(x_ref, y_ref, mod_ref, g_ref, dsk_ref, gw_ref, gb_ref, o_ref):
    sh1, sc1, g1 = _mod_slices(mod_ref[0])[:3]
    xf = x_ref[0]
    xn = _rms_mod(xf, g_ref[...], sh1, sc1)
    z = _gelu_tanh(y_ref[0] + dsk_ref[...] * xn)
    gate = jax.nn.sigmoid(jnp.dot(z.astype(_BF16), gw_ref[...], preferred_element_type=_F32) + gb_ref[...])
    o_ref[0] = xf + g1 * (z * gate)


def _s5post(x, y, mod, mod_row, g, dsk, gw, gb, tt):
    b, n, d = x.shape
    row = (lambda bi: bi) if mod_row is None else (lambda bi: mod_row)
    tok = pl.BlockSpec((1, tt, d), lambda bi, t: (bi, t, 0))
    vec = pl.BlockSpec((1, d), lambda bi, t: (0, 0))
    return pl.pallas_call(
        _s5post_kernel,
        out_shape=jax.ShapeDtypeStruct((b, n, d), _F32),
        grid=(b, n // tt),
        in_specs=[tok, tok, pl.BlockSpec((1, 1, 6 * d), lambda bi, t: (row(bi), 0, 0)),
                  vec, vec, _resident((d, d)), vec],
        out_specs=tok,
        compiler_params=_cparams(("arbitrary", "arbitrary")),
        name="s5post",
    )(x, y, mod, g, dsk, gw, gb)


def _mlp_kernel(x_ref, mod_ref, g_ref, w1_ref, b1_ref, w2_ref, b2_ref, fg_ref, o_ref, *, final):
    sh2, sc2, g2 = _mod_slices(mod_ref[0])[3:]
    xf = x_ref[0]
    xn = _rms_mod(xf, g_ref[...], sh2, sc2).astype(_BF16)
    acc = jnp.zeros(xf.shape, _F32)
    fc = D_MODEL
    for j in range(D_FF // fc):
        a = jnp.dot(xn, w1_ref[:, j * fc:(j + 1) * fc], preferred_element_type=_F32)
        a = jnp.maximum(a + b1_ref[:, j * fc:(j + 1) * fc], 0.0)
        acc = acc + jnp.dot((a * a).astype(_BF16), w2_ref[j * fc:(j + 1) * fc, :],
                            preferred_element_type=_F32)
    out = xf + g2 * (acc + b2_ref[...])
    if final:
        ms = jnp.mean(out * out, axis=-1, keepdims=True)
        out = out * lax.rsqrt(ms + EPS) * fg_ref[...]
    o_ref[0] = out


def _mlp(x, mod, mod_row, g, w1, b1, w2, b2, fg, tt, final):
    b, n, d = x.shape
    row = (lambda bi: bi) if mod_row is None else (lambda bi: mod_row)
    tok = pl.BlockSpec((1, tt, d), lambda bi, t: (bi, t, 0))
    vec = pl.BlockSpec((1, d), lambda bi, t: (0, 0))
    return pl.pallas_call(
        functools.partial(_mlp_kernel, final=final),
        out_shape=jax.ShapeDtypeStruct((b, n, d), _F32),
        grid=(b, n // tt),
        in_specs=[tok, pl.BlockSpec((1, 1, 6 * d), lambda bi, t: (row(bi), 0, 0)), vec,
                  _resident((d, D_FF)), pl.BlockSpec((1, D_FF), lambda bi, t: (0, 0)),
                  _resident((D_FF, d)), vec, vec],
        out_specs=tok,
        compiler_params=_cparams(("arbitrary", "arbitrary")),
        name="mlp_final" if final else "mlp",
    )(x, mod, g, w1, b1, w2, b2, fg)


def _col_window_matrices(tile_rows):
    n = tile_rows * GRID_W
    t = np.arange(n)
    r, c = t // GRID_W, t % GRID_W
    mats = []
    for w in POOL_WINDOWS:
        lo, hi = np.maximum(c - w // 2, 0), np.minimum(c + w - w // 2, GRID_W)
        m = (r[:, None] == r[None, :]) & (c[None, :] >= lo[:, None]) & (c[None, :] < hi[:, None])
        mats.append(m)
    return jnp.asarray(np.stack(mats), dtype=_BF16)


def _seq_window_matrices(n):
    t = np.arange(n)
    mats = []
    for w in POOL_WINDOWS:
        lo, hi = np.maximum(t - w // 2, 0), np.minimum(t + w - w // 2, n)
        mats.append((t[None, :] >= lo[:, None]) & (t[None, :] < hi[:, None]))
    return jnp.asarray(np.stack(mats), dtype=_BF16)


POOL_BLOCK_ROWS = 16
POOL_HALO_ROWS = 8
POOL_TILE_ROWS = 4


def _pool_kernel(xm_ref, xt_ref, xb_ref, mod_ref, g_ref, acol_ref, pw_ref, ps_ref, o_ref, cbuf, *, rows):
    i = pl.program_id(1)
    nblk = pl.num_programs(1)
    sh1, sc1, g1 = _mod_slices(mod_ref[0])[:3]
    gn = g_ref[...]
    halo = POOL_HALO_ROWS * GRID_W
    main = POOL_BLOCK_ROWS * GRID_W
    tile = POOL_TILE_ROWS * GRID_W
    xm = xm_ref[0]
    xn_m = _rms_mod(xm, gn, sh1, sc1)
    xn_t = jnp.where(i > 0, _rms_mod(xt_ref[0], gn, sh1, sc1), 0.0)
    xn_b = jnp.where(i < nblk - 1, _rms_mod(xb_ref[0], gn, sh1, sc1), 0.0)
    xn_ext = jnp.concatenate([xn_t, xn_m, xn_b], axis=0).astype(_BF16)
    ntile = (main + 2 * halo) // tile
    tok = lax.broadcasted_iota(jnp.int32, (main, POOL_CH), 0)
    r = i * POOL_BLOCK_ROWS + (tok >> int(math.log2(GRID_W)))
    c = tok & (GRID_W - 1)
    for gi, w in enumerate(POOL_WINDOWS):
        ch = slice(gi * POOL_CH, (gi + 1) * POOL_CH)
        a = acol_ref[gi]
        for k in range(ntile):
            cbuf[k * tile:(k + 1) * tile, :] = jnp.dot(
                a, xn_ext[k * tile:(k + 1) * tile, ch], preferred_element_type=_F32)
        tot = jnp.zeros((main, POOL_CH), _F32)
        for k in range(-(w // 2), w - w // 2):
            off = halo + k * GRID_W
            tot = tot + cbuf[off:off + main, :]
        rcnt = jnp.minimum(r + (w - w // 2), rows) - jnp.maximum(r - w // 2, 0)
        ccnt = jnp.minimum(c + (w - w // 2), GRID_W) - jnp.maximum(c - w // 2, 0)
        p = tot / (rcnt * ccnt).astype(_F32) - xn_m[:, ch]
        y = jnp.dot(p.astype(_BF16), pw_ref[gi], preferred_element_type=_F32) * ps_ref[:, ch]
        o_ref[0, :, ch] = xm[:, ch] + g1[:, ch] * y


def _pool_grid(x, mod, g, pw, ps):
    b, n, d = x.shape
    rows = n // GRID_W
    main = POOL_BLOCK_ROWS * GRID_W
    halo = POOL_HALO_ROWS * GRID_W
    nblk = n // main
    ratio = main // halo
    nh = n // halo
    acol = _col_window_matrices(POOL_TILE_ROWS)
    tile = POOL_TILE_ROWS * GRID_W
    vec = pl.BlockSpec((1, d), lambda bi, i: (0, 0))
    return pl.pallas_call(
        functools.partial(_pool_kernel, rows=rows),
        out_shape=jax.ShapeDtypeStruct((b, n, d), _F32),
        grid=(b, nblk),
        in_specs=[pl.BlockSpec((1, main, d), lambda bi, i: (bi, i, 0)),
                  pl.BlockSpec((1, halo, d), lambda bi, i: (bi, jnp.maximum(i * ratio - 1, 0), 0)),
                  pl.BlockSpec((1, halo, d), lambda bi, i: (bi, jnp.minimum((i + 1) * ratio, nh - 1), 0)),
                  pl.BlockSpec((1, 1, 6 * d), lambda bi, i: (bi, 0, 0)),
                  vec,
                  _resident((len(POOL_WINDOWS), tile, tile)),
                  _resident((len(POOL_WINDOWS), POOL_CH, POOL_CH)),
                  vec],
        out_specs=pl.BlockSpec((1, main, d), lambda bi, i: (bi, i, 0)),
        scratch_shapes=[pltpu.VMEM((main + 2 * halo, POOL_CH), _F32)],
        compiler_params=_cparams(("arbitrary", "arbitrary")),
        name="pool_grid",
    )(x, x, x, mod, g, acol, pw, ps)


def _poolseq_kernel(x_ref, mod_ref, g_ref, aseq_ref, pw_ref, ps_ref, o_ref):
    sh1, sc1, g1 = _mod_slices(mod_ref[0])[:3]
    xf = x_ref[0]
    n = xf.shape[0]
    xn = _rms_mod(xf, g_ref[...], sh1, sc1)
    xb = xn.astype(_BF16)
    t = lax.broadcasted_iota(jnp.int32, (n, POOL_CH), 0)
    for gi, w in enumerate(POOL_WINDOWS):
        ch = slice(gi * POOL_CH, (gi + 1) * POOL_CH)
        tot = jnp.dot(aseq_ref[gi], xb[:, ch], preferred_element_type=_F32)
        cnt = jnp.minimum(t + (w - w // 2), n) - jnp.maximum(t - w // 2, 0)
        p = tot / cnt.astype(_F32) - xn[:, ch]
        y = jnp.dot(p.astype(_BF16), pw_ref[gi], preferred_element_type=_F32) * ps_ref[:, ch]
        o_ref[0, :, ch] = xf[:, ch] + g1[:, ch] * y


def _pool_seq(x, mod, mod_row, g, pw, ps):
    b, n, d = x.shape
    aseq = _seq_window_matrices(n)
    vec = pl.BlockSpec((1, d), lambda bi: (0, 0))
    return pl.pallas_call(
        _poolseq_kernel,
        out_shape=jax.ShapeDtypeStruct((b, n, d), _F32),
        grid=(b,),
        in_specs=[pl.BlockSpec((1, n, d), lambda bi: (bi, 0, 0)),
                  pl.BlockSpec((1, 1, 6 * d), lambda bi: (mod_row, 0, 0)),
                  vec,
                  _resident((len(POOL_WINDOWS), n, n)),
                  _resident((len(POOL_WINDOWS), POOL_CH, POOL_CH)),
                  vec],
        out_specs=pl.BlockSpec((1, n, d), lambda bi: (bi, 0, 0)),
        compiler_params=_cparams(("arbitrary",)),
        name="pool_seq",
    )(x, mod, g, aseq, pw, ps)


def kernel(x, c, ctx, c_ctx, ada_w, ada_b, norm1_g, norm2_g, s5_a_re, s5_a_im, s5_log_dt, s5_b_re, s5_b_im, s5_c_re, s5_c_im, s5_d, s5_glu_w, s5_glu_b, pool_w, pool_scale, mlp_w1, mlp_b1, mlp_w2, mlp_b2, final_g):
    bsz, n_tok, d = x.shape
    n_ctx = ctx.shape[1]
    depth = ada_w.shape[0]
    assert d == D_MODEL and bsz <= 4 and n_tok % (POOL_BLOCK_ROWS * GRID_W) == 0
    lat_chunks = n_tok // S5_CHUNK
    ctx_chunks = n_ctx // S5_CHUNK
    assert lat_chunks & (lat_chunks - 1) == 0 and n_ctx % S5_CHUNK == 0
    assert ctx_chunks + 2 * CTX_OFF <= CTX_SEG
    last_ctx_reader = ((depth - 1) // N_MIXERS) * N_MIXERS
    ctx_row = bsz
    tt = min(512, n_tok)
    tt_ctx = min(512, n_ctx)

    cc = jnp.zeros((8, d), _F32).at[:bsz].set(c).at[ctx_row].set(c_ctx)
    mods = _modulation(cc, ada_w, ada_b)
    w1 = mlp_w1.astype(_BF16)
    w2 = mlp_w2.astype(_BF16)
    glu_w = s5_glu_w.astype(_BF16)
    pw = pool_w.astype(_BF16)
    zeros_h0 = jnp.zeros((S5_GROUPS, S5_NSTATE, 128), _F32)
    lat_steps = int(math.log2(lat_chunks))
    ctx_steps = int(math.log2(CTX_SEG))

    h_ctx = ctx
    for i in range(depth):
        ctx_in = i <= last_ctx_reader
        ctx_out = i < last_ctx_reader
        j = i // N_MIXERS
        mod = mods[i].reshape(8, 1, 6 * d)
        g1n = norm1_g[i].reshape(1, d)
        g2n = norm2_g[i].reshape(1, d)
        if i % N_MIXERS == 0:
            mt, bst, cst, lampow = _s5_tables(s5_a_re[j], s5_a_im[j], s5_log_dt[j], s5_b_re[j], s5_b_im[j],
                                              s5_c_re[j], s5_c_im[j], max(lat_steps, ctx_steps))
            dsk = s5_d[j].reshape(1, d)
            gb = s5_glu_b[j].reshape(1, d)
            cn = _prenorm(h_ctx, mod, ctx_row, g1n, tt_ctx)
            yct, hc = _s5core(_to_ut(cn, CTX_SEG), mt, bst, cst, lampow, zeros_h0, CTX_SEG, bsz, True)
            hc = hc.reshape(S5_GROUPS, S5_NSTATE, bsz, CTX_SEG)
            p2 = 2 * S5_STATE
            e = CTX_OFF + ctx_chunks
            h0 = jnp.concatenate([hc[:, :p2, :, e], hc[:, p2:, :, CTX_OFF - 1]], axis=1)
            lh0 = jnp.concatenate([hc[:, :p2, :, e + 1], hc[:, p2:, :, CTX_OFF - 2]], axis=1)
            h0arr = jnp.zeros((S5_GROUPS, S5_NSTATE, 128), _F32)
            h0arr = h0arr.at[:, :, :bsz].set(h0).at[:, :, 8:8 + bsz].set(lh0)
            xn = _prenorm(x, mod, None, g1n, tt)
            yt, _ = _s5core(_to_ut(xn), mt, bst, cst, lampow, h0arr, lat_chunks, bsz, False)
            x = _s5post(x, _from_yt(yt, bsz, n_tok), mod, None, g1n, dsk, glu_w[j], gb, tt)
            if ctx_out:
                h_ctx = _s5post(h_ctx, _from_yt(yct, bsz, n_ctx, CTX_SEG), mod, ctx_row, g1n, dsk,
                                glu_w[j], gb, tt_ctx)
        else:
            ps = pool_scale[j].reshape(1, d)
            x = _pool_grid(x, mod, g1n, pw[j], ps)
            if ctx_out:
                h_ctx = _pool_seq(h_ctx, mod, ctx_row, g1n, pw[j], ps)
        final = i == depth - 1
        x = _mlp(x, mod, None, g2n, w1[i], mlp_b1[i].reshape(1, D_FF), w2[i], mlp_b2[i].reshape(1, d),
                 final_g.reshape(1, d), tt, final)
        if ctx_out:
            h_ctx = _mlp(h_ctx, mod, ctx_row, g2n, w1[i], mlp_b1[i].reshape(1, D_FF), w2[i],
                         mlp_b2[i].reshape(1, d), final_g.reshape(1, d), tt_ctx, False)
    return x
```

```python
import functools
import math

import numpy as np
import jax
import jax.numpy as jnp
from jax import lax
from jax.experimental import pallas as pl
from jax.experimental.pallas import tpu as pltpu

D_MODEL = 1024
GRID_W = 64
S5_GROUP = 16
S5_GROUPS = D_MODEL // S5_GROUP
S5_STATE = 64
POOL_WINDOWS = (2, 4, 8, 16)
POOL_CH = D_MODEL // len(POOL_WINDOWS)
D_FF = 4 * D_MODEL
N_MIXERS = 2
EPS = 1e-6
LANES = 128

S5_CHUNK = 32
S5_ROWS = S5_CHUNK * S5_GROUP
S5_NSTATE = 4 * S5_STATE
CTX_SEG = 16
CTX_OFF = 2
S5_PRE_TOK = 8
S5_POST_TOK = 4
V7X_VMEM_LIMIT_BYTES = 56 * 1024 * 1024

_F32 = jnp.float32
_BF16 = jnp.bfloat16


def _cparams(sem):
    return pltpu.CompilerParams(dimension_semantics=sem, vmem_limit_bytes=V7X_VMEM_LIMIT_BYTES)


def _resident(shape):
    nd = len(shape)
    return pl.BlockSpec(shape, lambda *_: (0,) * nd, pipeline_mode=pl.Buffered(1))


def _mod_kernel(cc_ref, w_ref, b_ref, o_ref):
    c = cc_ref[...]
    a = c * jax.nn.sigmoid(c)
    o_ref[0] = jnp.dot(a, w_ref[0], preferred_element_type=_F32,
                       precision=lax.Precision.HIGHEST) + b_ref[0]


def _modulation(cc, ada_w, ada_b):
    depth, d, n = ada_w.shape
    tn = 1536
    return pl.pallas_call(
        _mod_kernel,
        out_shape=jax.ShapeDtypeStruct((depth, 8, n), _F32),
        grid=(depth, n // tn),
        in_specs=[pl.BlockSpec((8, d), lambda i, j: (0, 0)),
                  pl.BlockSpec((1, d, tn), lambda i, j: (i, 0, j)),
                  pl.BlockSpec((1, 1, tn), lambda i, j: (i, 0, j))],
        out_specs=pl.BlockSpec((1, 8, tn), lambda i, j: (i, 0, j)),
        compiler_params=_cparams(("arbitrary", "arbitrary")),
        name="modulation",
    )(cc, ada_w, ada_b.reshape(depth, 1, n))


def _rms_mod(xf, g, shift, scale):
    ms = jnp.mean(xf * xf, axis=-1, keepdims=True)
    return (xf * lax.rsqrt(ms + EPS) * g) * (1.0 + scale) + shift


def _mod_slices(m):
    d = D_MODEL
    return tuple(m[:, k * d:(k + 1) * d] for k in range(6))


def _gelu_tanh(x):
    return 0.5 * x * (1.0 + jnp.tanh(math.sqrt(2.0 / math.pi) * (x + 0.044715 * (x * x * x))))


def _cmul(a, yr, yi):
    ar, ai = a
    return ar * yr - ai * yi, ar * yi + ai * yr


def _prenorm_kernel(x_ref, mod_ref, g_ref, o_ref):
    sh1, sc1 = _mod_slices(mod_ref[0])[:2]
    o_ref[0] = _rms_mod(x_ref[0], g_ref[...], sh1, sc1).astype(o_ref.dtype)


def _prenorm(x, mod, mod_row, g, tt):
    b, n, d = x.shape
    return pl.pallas_call(
        _prenorm_kernel,
        out_shape=jax.ShapeDtypeStruct((b, n, d), _BF16),
        grid=(b, n // tt),
        in_specs=[pl.BlockSpec((1, tt, d), lambda bi, t: (bi, t, 0)),
                  pl.BlockSpec((1, 1, 6 * d), lambda bi, t: (mod_row, 0, 0)),
                  pl.BlockSpec((1, d), lambda bi, t: (0, 0))],
        out_specs=pl.BlockSpec((1, tt, d), lambda bi, t: (bi, t, 0)),
        compiler_params=_cparams(("arbitrary", "arbitrary")),
        name="prenorm",
    )(x, mod, g)


def _s5pre_kernel(x_ref, mod_ref, g_ref, ut_ref, *, nb, k):
    d = D_MODEL
    gn = g_ref[...]
    mods = [_mod_slices(mod_ref[b])[:2] for b in range(nb)]
    for j in range(k):
        xn = jnp.concatenate(
            [_rms_mod(x_ref[b, :, j * d:(j + 1) * d], gn, *mods[b]) for b in range(nb)], axis=0)
        ut_ref[:, j * S5_GROUP:(j + 1) * S5_GROUP, :] = (
            xn.T.astype(_BF16).reshape(S5_GROUPS, S5_GROUP, LANES))


def _s5pre(x, mod, g):
    b, n, d = x.shape
    lg = LANES // b
    per = n // lg
    k = S5_PRE_TOK
    blocks_per_v = S5_CHUNK // k
    ncol = b * (n // S5_CHUNK)
    return pl.pallas_call(
        functools.partial(_s5pre_kernel, nb=b, k=k),
        out_shape=jax.ShapeDtypeStruct((S5_GROUPS, S5_ROWS, ncol), _BF16),
        grid=(per // k,),
        in_specs=[pl.BlockSpec((b, lg, k * d), lambda r: (0, 0, r)),
                  pl.BlockSpec((8, 1, 6 * d), lambda r: (0, 0, 0)),
                  pl.BlockSpec((1, d), lambda r: (0, 0))],
        out_specs=pl.BlockSpec((S5_GROUPS, k * S5_GROUP, LANES),
                               lambda r: (0, r % blocks_per_v, r // blocks_per_v)),
        compiler_params=_cparams(("arbitrary",)),
        name="s5pre",
    )(x.reshape(b, lg, per * d), mod, g)


def _lane_shift(v, dist, pos, seg, reverse):
    n = v.shape[1]
    if reverse:
        return jnp.where(pos < seg - dist, pltpu.roll(v, n - dist, axis=1), 0.0)
    return jnp.where(pos >= dist, pltpu.roll(v, dist, axis=1), 0.0)


def _lane_scan(xr, xi, lam, unit, pos, seg, reverse):
    k = 0
    while (1 << k) < seg:
        dist = 1 << k
        mr, mi = _cmul(lam(unit * dist), _lane_shift(xr, dist, pos, seg, reverse),
                       _lane_shift(xi, dist, pos, seg, reverse))
        xr, xi = xr + mr, xi + mi
        k += 1
    return _lane_shift(xr, 1, pos, seg, reverse), _lane_shift(xi, 1, pos, seg, reverse)


def _s5core_kernel(utl_ref, utc_ref, mt_ref, bst_ref, cst_ref, lp_ref, ytl_ref, *ytc_out,
                   nb, ctx_chunks, pows):
    p = S5_STATE
    ul = utl_ref[0]
    uc = utc_ref[0]
    ncol = ul.shape[1]
    nv = ncol // LANES
    lg = LANES // nb
    bst, cst, mt = bst_ref[0], cst_ref[0], mt_ref[0]
    lp = lp_ref[0]

    def lam_of(d):
        def lam(n):
            k = pows.index(n)
            return lp[2 * d][:, k:k + 1], lp[2 * d + 1][:, k:k + 1]
        return lam

    lane = lax.broadcasted_iota(jnp.int32, (1, LANES), 1)
    sc = jnp.dot(bst, uc, preferred_element_type=_F32)
    posc = lane & (CTX_SEG - 1)
    hc = []
    for d in range(2):
        r0 = 2 * p * d
        hr, hi = _lane_scan(sc[r0:r0 + p], sc[r0 + p:r0 + 2 * p], lam_of(d), 1, posc, CTX_SEG, d == 1)
        hc += [hr, hi]
    sl = jnp.dot(bst, ul, preferred_element_type=_F32)
    posl = lane & (lg - 1)
    bidl = lane >> int(math.log2(lg))
    hl = []
    for d in range(2):
        lam = lam_of(d)
        r0 = 2 * p * d
        sr = [sl[r0:r0 + p, v * LANES:(v + 1) * LANES] for v in range(nv)]
        si = [sl[r0 + p:r0 + 2 * p, v * LANES:(v + 1) * LANES] for v in range(nv)]
        order = list(range(nv)) if d == 0 else list(range(nv - 1, -1, -1))
        ir, ii = {}, {}
        prev = None
        for v in order:
            if prev is None:
                ir[v], ii[v] = sr[v], si[v]
            else:
                mr, mi = _cmul(lam(1), ir[prev], ii[prev])
                ir[v], ii[v] = mr + sr[v], mi + si[v]
            prev = v
        entry = 0 if d == 0 else lg - 1
        src = (CTX_OFF + ctx_chunks) if d == 0 else (CTX_OFF - 1)
        h0r = jnp.zeros((p, LANES), _F32)
        h0i = jnp.zeros((p, LANES), _F32)
        for b in range(nb):
            m = (bidl == b) & (posl == entry)
            col = b * CTX_SEG + src
            h0r = jnp.where(m, hc[2 * d][:, col:col + 1], h0r)
            h0i = jnp.where(m, hc[2 * d + 1][:, col:col + 1], h0i)
        jr, ji = _cmul(lam(nv), h0r, h0i)
        er, ei = _lane_scan(ir[prev] + jr, ii[prev] + ji, lam, nv, posl, lg, d == 1)
        er, ei = er + h0r, ei + h0i
        hr, hi = {order[0]: er}, {order[0]: ei}
        for n, v in enumerate(order[1:], start=1):
            mr, mi = _cmul(lam(n), er, ei)
            hr[v], hi[v] = ir[order[n - 1]] + mr, ii[order[n - 1]] + mi
        hl += [jnp.concatenate([hr[v] for v in range(nv)], axis=1),
               jnp.concatenate([hi[v] for v in range(nv)], axis=1)]
    h = jnp.concatenate(hl, axis=0).astype(_BF16)
    y = jnp.dot(mt, ul, preferred_element_type=_F32)
    ytl_ref[0] = y + jnp.dot(cst, h, preferred_element_type=_F32)
    if ytc_out:
        hcb = jnp.concatenate(hc, axis=0).astype(_BF16)
        yc = jnp.dot(mt, uc, preferred_element_type=_F32)
        ytc_out[0][0] = yc + jnp.dot(cst, hcb, preferred_element_type=_F32)


def _s5_pows(nv, lg):
    pows = set(range(1, nv + 1))
    pows |= {nv << k for k in range(int(math.log2(lg)))}
    pows |= {1 << k for k in range(int(math.log2(CTX_SEG)))}
    return tuple(sorted(pows))


def _s5core(utl, utc, mt, bst, cst, lampow, nb, ctx_chunks, pows, want_yc):
    g, rows, n = utl.shape
    out_shape = [jax.ShapeDtypeStruct((g, rows, n), _F32)]
    out_specs = [pl.BlockSpec((1, rows, n), lambda i: (i, 0, 0))]
    if want_yc:
        out_shape.append(jax.ShapeDtypeStruct((g, rows, LANES), _F32))
        out_specs.append(pl.BlockSpec((1, rows, LANES), lambda i: (i, 0, 0)))
    res = pl.pallas_call(
        functools.partial(_s5core_kernel, nb=nb, ctx_chunks=ctx_chunks, pows=pows),
        out_shape=out_shape,
        grid=(g,),
        in_specs=[pl.BlockSpec((1, rows, n), lambda i: (i, 0, 0)),
                  pl.BlockSpec((1, rows, LANES), lambda i: (i, 0, 0)),
                  pl.BlockSpec((1, rows, rows), lambda i: (i, 0, 0)),
                  pl.BlockSpec((1, S5_NSTATE, rows), lambda i: (i, 0, 0)),
                  pl.BlockSpec((1, rows, S5_NSTATE), lambda i: (i, 0, 0)),
                  pl.BlockSpec((1, 4, S5_STATE, LANES), lambda i: (i, 0, 0, 0))],
        out_specs=out_specs,
        compiler_params=_cparams(("arbitrary",)),
        name="s5core_yc" if want_yc else "s5core",
    )(utl, utc, mt, bst, cst, lampow)
    return (res[0], res[1]) if want_yc else (res[0], None)


def _s5_tables(a_re, a_im, log_dt, b_re, b_im, c_re, c_im, pows):
    f = _F32
    t_len = S5_CHUNK
    hp = lax.Precision.HIGHEST
    n = jnp.arange(t_len + 1, dtype=f)
    ks, bss, css, lps = [], [], [], []
    for d in range(2):
        ar, ai = a_re[d].astype(f), a_im[d].astype(f)
        dt = jnp.exp(log_dt[d].astype(f))[:, None]
        da_re, da_im = ar * dt, ai * dt
        mag = jnp.exp(da_re)
        lb_re, lb_im = mag * jnp.cos(da_im), mag * jnp.sin(da_im)
        den = ar * ar + ai * ai
        num_re, num_im = lb_re - 1.0, lb_im
        k_re = (num_re * ar + num_im * ai) / den
        k_im = (num_im * ar - num_re * ai) / den
        bre, bim = b_re[d].astype(f), b_im[d].astype(f)
        bb_re = k_re[..., None] * bre - k_im[..., None] * bim
        bb_im = k_re[..., None] * bim + k_im[..., None] * bre
        pmag = jnp.exp(n[None, None, :] * da_re[..., None])
        pang = n[None, None, :] * da_im[..., None]
        pw_re, pw_im = pmag * jnp.cos(pang), pmag * jnp.sin(pang)
        cre, cim = c_re[d].astype(f), c_im[d].astype(f)
        cl_re = cre[:, None] * pw_re.transpose(0, 2, 1)[:, :t_len, None, :] \
            - cim[:, None] * pw_im.transpose(0, 2, 1)[:, :t_len, None, :]
        cl_im = cre[:, None] * pw_im.transpose(0, 2, 1)[:, :t_len, None, :] \
            + cim[:, None] * pw_re.transpose(0, 2, 1)[:, :t_len, None, :]
        kk = jnp.einsum('gthp,gpk->gthk', cl_re, bb_re, precision=hp) \
            - jnp.einsum('gthp,gpk->gthk', cl_im, bb_im, precision=hp)
        ks.append(kk)
        idx = (t_len - 1 - jnp.arange(t_len)) if d == 0 else jnp.arange(t_len)
        w_re, w_im = pw_re[:, :, idx], pw_im[:, :, idx]
        bs_re = w_re[..., None] * bb_re[:, :, None, :] - w_im[..., None] * bb_im[:, :, None, :]
        bs_im = w_re[..., None] * bb_im[:, :, None, :] + w_im[..., None] * bb_re[:, :, None, :]
        g_ = bs_re.shape[0]
        bss += [bs_re.reshape(g_, S5_STATE, -1), bs_im.reshape(g_, S5_STATE, -1)]
        idx = (jnp.arange(t_len) + 1) if d == 0 else (t_len - jnp.arange(t_len))
        w_re, w_im = pw_re[:, :, idx], pw_im[:, :, idx]
        z_re = cre[:, None] * w_re.transpose(0, 2, 1)[:, :, None, :] \
            - cim[:, None] * w_im.transpose(0, 2, 1)[:, :, None, :]
        z_im = cre[:, None] * w_im.transpose(0, 2, 1)[:, :, None, :] \
            + cim[:, None] * w_re.transpose(0, 2, 1)[:, :, None, :]
        css += [z_re.reshape(g_, -1, S5_STATE), (-z_im).reshape(g_, -1, S5_STATE)]
        kpow = jnp.asarray(pows, dtype=f) * t_len
        lmag = jnp.exp(kpow[None, None, :] * da_re[..., None])
        lang = kpow[None, None, :] * da_im[..., None]
        pad = ((0, 0), (0, 0), (0, LANES - len(pows)))
        lps += [jnp.pad(lmag * jnp.cos(lang), pad), jnp.pad(lmag * jnp.sin(lang), pad)]
    kf, kb = ks
    tt = jnp.arange(t_len)
    diff = tt[:, None] - tt[None, :]
    kf_g = kf[:, jnp.clip(diff, 0, t_len - 1)]
    kb_g = kb[:, jnp.clip(-diff, 0, t_len - 1)]
    m5 = jnp.where((diff >= 0)[None, :, :, None, None], kf_g, 0.0) \
        + jnp.where((diff <= 0)[None, :, :, None, None], kb_g, 0.0)
    g_ = m5.shape[0]
    mt = m5.transpose(0, 1, 3, 2, 4).reshape(g_, S5_ROWS, S5_ROWS)
    bst = jnp.concatenate(bss, axis=1)
    cst = jnp.concatenate(css, axis=2)
    lampow = jnp.stack(lps, axis=1)
    return mt.astype(_BF16), bst.astype(_BF16), cst.astype(_BF16), lampow


def _ctx_to_ut(xn):
    b, n, _ = xn.shape
    c = n // S5_CHUNK
    v = xn.reshape(b, c, S5_CHUNK, S5_GROUPS, S5_GROUP).transpose(3, 2, 4, 0, 1)
    v = jnp.pad(v, ((0, 0),) * 4 + ((CTX_OFF, CTX_SEG - c - CTX_OFF),))
    v = v.reshape(S5_GROUPS, S5_ROWS, b * CTX_SEG)
    return jnp.pad(v, ((0, 0), (0, 0), (0, LANES - b * CTX_SEG)))


def _ctx_from_yt(yt, b, n):
    c = n // S5_CHUNK
    v = yt[:, :, :b * CTX_SEG].reshape(S5_GROUPS, S5_CHUNK, S5_GROUP, b, CTX_SEG)
    v = v[..., CTX_OFF:CTX_OFF + c]
    return v.transpose(3, 4, 1, 0, 2).reshape(b, n, D_MODEL)


def _s5_glu(xf, xn, y, dsk, gw, gb, g1):
    z = _gelu_tanh(y + dsk * xn)
    gate = jax.nn.sigmoid(jnp.dot(z.astype(_BF16), gw, preferred_element_type=_F32) + gb)
    return xf + g1 * (z * gate)


def _s5post_ctx_kernel(x_ref, y_ref, mod_ref, g_ref, dsk_ref, gw_ref, gb_ref, o_ref):
    sh1, sc1, g1 = _mod_slices(mod_ref[0])[:3]
    xf = x_ref[0]
    xn = _rms_mod(xf, g_ref[...], sh1, sc1)
    o_ref[0] = _s5_glu(xf, xn, y_ref[0], dsk_ref[...], gw_ref[...], gb_ref[...], g1)


def _s5post_ctx(x, y, mod, mod_row, g, dsk, gw, gb):
    b, n, d = x.shape
    tok = pl.BlockSpec((1, n, d), lambda bi: (bi, 0, 0))
    vec = pl.BlockSpec((1, d), lambda bi: (0, 0))
    return pl.pallas_call(
        _s5post_ctx_kernel,
        out_shape=jax.ShapeDtypeStruct((b, n, d), _F32),
        grid=(b,),
        in_specs=[tok, tok, pl.BlockSpec((1, 1, 6 * d), lambda bi: (mod_row, 0, 0)),
                  vec, vec, _resident((d, d)), vec],
        out_specs=tok,
        compiler_params=_cparams(("arbitrary",)),
        name="s5post_ctx",
    )(x, y, mod, g, dsk, gw, gb)


def _s5post_kernel(x_ref, yt_ref, mod_ref, g_ref, dsk_ref, gw_ref, gb_ref, o_ref, *, nb, k):
    d = D_MODEL
    lg = LANES // nb
    gn = g_ref[...]
    dsk = dsk_ref[...]
    mods = [_mod_slices(mod_ref[b])[:3] for b in range(nb)]
    xs, zs = [], []
    for j in range(k):
        y = yt_ref[:, j * S5_GROUP:(j + 1) * S5_GROUP, :].reshape(d, LANES).T
        for b in range(nb):
            xf = x_ref[b, :, j * d:(j + 1) * d]
            xn = _rms_mod(xf, gn, mods[b][0], mods[b][1])
            xs.append(xf)
            zs.append(_gelu_tanh(y[b * lg:(b + 1) * lg] + dsk * xn))
    z = jnp.concatenate(zs, axis=0)
    gate = jax.nn.sigmoid(jnp.dot(z.astype(_BF16), gw_ref[...], preferred_element_type=_F32) + gb_ref[...])
    o = z * gate
    for j in range(k):
        for b in range(nb):
            q = j * nb + b
            o_ref[b, :, j * d:(j + 1) * d] = xs[q] + mods[b][2] * o[q * lg:(q + 1) * lg]


def _s5post(x, yt, mod, g, dsk, gw, gb):
    b, n, d = x.shape
    lg = LANES // b
    per = n // lg
    k = S5_POST_TOK
    blocks_per_v = S5_CHUNK // k
    tok = pl.BlockSpec((b, lg, k * d), lambda r: (0, 0, r))
    vec = pl.BlockSpec((1, d), lambda r: (0, 0))
    out = pl.pallas_call(
        functools.partial(_s5post_kernel, nb=b, k=k),
        out_shape=jax.ShapeDtypeStruct((b, lg, per * d), _F32),
        grid=(per // k,),
        in_specs=[tok,
                  pl.BlockSpec((S5_GROUPS, k * S5_GROUP, LANES),
                               lambda r: (0, r % blocks_per_v, r // blocks_per_v)),
                  pl.BlockSpec((8, 1, 6 * d), lambda r: (0, 0, 0)),
                  vec, vec, _resident((d, d)), vec],
        out_specs=tok,
        compiler_params=_cparams(("arbitrary",)),
        name="s5post",
    )(x.reshape(b, lg, per * d), yt, mod, g, dsk, gw, gb)
    return out.reshape(b, n, d)


def _mlp_kernel(x_ref, mod_ref, g_ref, w1_ref, b1_ref, w2_ref, b2_ref, fg_ref, o_ref, *, final):
    sh2, sc2, g2 = _mod_slices(mod_ref[0])[3:]
    xf = x_ref[0]
    xn = _rms_mod(xf, g_ref[...], sh2, sc2).astype(_BF16)
    acc = jnp.zeros(xf.shape, _F32)
    fc = D_MODEL
    for j in range(D_FF // fc):
        a = jnp.dot(xn, w1_ref[:, j * fc:(j + 1) * fc], preferred_element_type=_F32)
        a = jnp.maximum(a + b1_ref[:, j * fc:(j + 1) * fc], 0.0)
        acc = acc + jnp.dot((a * a).astype(_BF16), w2_ref[j * fc:(j + 1) * fc, :],
                            preferred_element_type=_F32)
    out = xf + g2 * (acc + b2_ref[...])
    if final:
        ms = jnp.mean(out * out, axis=-1, keepdims=True)
        out = out * lax.rsqrt(ms + EPS) * fg_ref[...]
    o_ref[0] = out


def _mlp(x, mod, mod_row, g, w1, b1, w2, b2, fg, tt, final):
    b, n, d = x.shape
    row = (lambda bi: bi) if mod_row is None else (lambda bi: mod_row)
    tok = pl.BlockSpec((1, tt, d), lambda bi, t: (bi, t, 0))
    vec = pl.BlockSpec((1, d), lambda bi, t: (0, 0))
    return pl.pallas_call(
        functools.partial(_mlp_kernel, final=final),
        out_shape=jax.ShapeDtypeStruct((b, n, d), _F32),
        grid=(b, n // tt),
        in_specs=[tok, pl.BlockSpec((1, 1, 6 * d), lambda bi, t: (row(bi), 0, 0)), vec,
                  _resident((d, D_FF)), pl.BlockSpec((1, D_FF), lambda bi, t: (0, 0)),
                  _resident((D_FF, d)), vec, vec],
        out_specs=tok,
        compiler_params=_cparams(("arbitrary", "arbitrary")),
        name="mlp_final" if final else "mlp",
    )(x, mod, g, w1, b1, w2, b2, fg)


def _col_window_matrices(tile_rows):
    n = tile_rows * GRID_W
    t = np.arange(n)
    r, c = t // GRID_W, t % GRID_W
    mats = []
    for w in POOL_WINDOWS:
        lo, hi = np.maximum(c - w // 2, 0), np.minimum(c + w - w // 2, GRID_W)
        m = (r[:, None] == r[None, :]) & (c[None, :] >= lo[:, None]) & (c[None, :] < hi[:, None])
        mats.append(m)
    return jnp.asarray(np.stack(mats), dtype=_BF16)


def _seq_window_matrices(n):
    t = np.arange(n)
    mats = []
    for w in POOL_WINDOWS:
        lo, hi = np.maximum(t - w // 2, 0), np.minimum(t + w - w // 2, n)
        mats.append((t[None, :] >= lo[:, None]) & (t[None, :] < hi[:, None]))
    return jnp.asarray(np.stack(mats), dtype=_BF16)


POOL_BLOCK_ROWS = 16
POOL_HALO_ROWS = 8
POOL_TILE_ROWS = 4


def _pool_kernel(xm_ref, xt_ref, xb_ref, mod_ref, g_ref, acol_ref, pw_ref, ps_ref, o_ref, cbuf, *, rows):
    i = pl.program_id(1)
    nblk = pl.num_programs(1)
    sh1, sc1, g1 = _mod_slices(mod_ref[0])[:3]
    gn = g_ref[...]
    halo = POOL_HALO_ROWS * GRID_W
    main = POOL_BLOCK_ROWS * GRID_W
    tile = POOL_TILE_ROWS * GRID_W
    xm = xm_ref[0]
    xn_m = _rms_mod(xm, gn, sh1, sc1)
    xn_t = jnp.where(i > 0, _rms_mod(xt_ref[0], gn, sh1, sc1), 0.0)
    xn_b = jnp.where(i < nblk - 1, _rms_mod(xb_ref[0], gn, sh1, sc1), 0.0)
    xn_ext = jnp.concatenate([xn_t, xn_m, xn_b], axis=0).astype(_BF16)
    ntile = (main + 2 * halo) // tile
    tok = lax.broadcasted_iota(jnp.int32, (main, POOL_CH), 0)
    r = i * POOL_BLOCK_ROWS + (tok >> int(math.log2(GRID_W)))
    c = tok & (GRID_W - 1)
    for gi, w in enumerate(POOL_WINDOWS):
        ch = slice(gi * POOL_CH, (gi + 1) * POOL_CH)
        a = acol_ref[gi]
        for k in range(ntile):
            cbuf[k * tile:(k + 1) * tile, :] = jnp.dot(
                a, xn_ext[k * tile:(k + 1) * tile, ch], preferred_element_type=_F32)
        tot = jnp.zeros((main, POOL_CH), _F32)
        for k in range(-(w // 2), w - w // 2):
            off = halo + k * GRID_W
            tot = tot + cbuf[off:off + main, :]
        rcnt = jnp.minimum(r + (w - w // 2), rows) - jnp.maximum(r - w // 2, 0)
        ccnt = jnp.minimum(c + (w - w // 2), GRID_W) - jnp.maximum(c - w // 2, 0)
        p = tot / (rcnt * ccnt).astype(_F32) - xn_m[:, ch]
        y = jnp.dot(p.astype(_BF16), pw_ref[gi], preferred_element_type=_F32) * ps_ref[:, ch]
        o_ref[0, :, ch] = xm[:, ch] + g1[:, ch] * y


def _pool_grid(x, mod, g, pw, ps):
    b, n, d = x.shape
    rows = n // GRID_W
    main = POOL_BLOCK_ROWS * GRID_W
    halo = POOL_HALO_ROWS * GRID_W
    nblk = n // main
    ratio = main // halo
    nh = n // halo
    acol = _col_window_matrices(POOL_TILE_ROWS)
    tile = POOL_TILE_ROWS * GRID_W
    vec = pl.BlockSpec((1, d), lambda bi, i: (0, 0))
    return pl.pallas_call(
        functools.partial(_pool_kernel, rows=rows),
        out_shape=jax.ShapeDtypeStruct((b, n, d), _F32),
        grid=(b, nblk),
        in_specs=[pl.BlockSpec((1, main, d), lambda bi, i: (bi, i, 0)),
                  pl.BlockSpec((1, halo, d), lambda bi, i: (bi, jnp.maximum(i * ratio - 1, 0), 0)),
                  pl.BlockSpec((1, halo, d), lambda bi, i: (bi, jnp.minimum((i + 1) * ratio, nh - 1), 0)),
                  pl.BlockSpec((1, 1, 6 * d), lambda bi, i: (bi, 0, 0)),
                  vec,
                  _resident((len(POOL_WINDOWS), tile, tile)),
                  _resident((len(POOL_WINDOWS), POOL_CH, POOL_CH)),
                  vec],
        out_specs=pl.BlockSpec((1, main, d), lambda bi, i: (bi, i, 0)),
        scratch_shapes=[pltpu.VMEM((main + 2 * halo, POOL_CH), _F32)],
        compiler_params=_cparams(("arbitrary", "arbitrary")),
        name="pool_grid",
    )(x, x, x, mod, g, acol, pw, ps)


def _poolseq_kernel(x_ref, mod_ref, g_ref, aseq_ref, pw_ref, ps_ref, o_ref):
    sh1, sc1, g1 = _mod_slices(mod_ref[0])[:3]
    xf = x_ref[0]
    n = xf.shape[0]
    xn = _rms_mod(xf, g_ref[...], sh1, sc1)
    xb = xn.astype(_BF16)
    t = lax.broadcasted_iota(jnp.int32, (n, POOL_CH), 0)
    for gi, w in enumerate(POOL_WINDOWS):
        ch = slice(gi * POOL_CH, (gi + 1) * POOL_CH)
        tot = jnp.dot(aseq_ref[gi], xb[:, ch], preferred_element_type=_F32)
        cnt = jnp.minimum(t + (w - w // 2), n) - jnp.maximum(t - w // 2, 0)
        p = tot / cnt.astype(_F32) - xn[:, ch]
        y = jnp.dot(p.astype(_BF16), pw_ref[gi], preferred_element_type=_F32) * ps_ref[:, ch]
        o_ref[0, :, ch] = xf[:, ch] + g1[:, ch] * y


def _pool_seq(x, mod, mod_row, g, pw, ps):
    b, n, d = x.shape
    aseq = _seq_window_matrices(n)
    vec = pl.BlockSpec((1, d), lambda bi: (0, 0))
    return pl.pallas_call(
        _poolseq_kernel,
        out_shape=jax.ShapeDtypeStruct((b, n, d), _F32),
        grid=(b,),
        in_specs=[pl.BlockSpec((1, n, d), lambda bi: (bi, 0, 0)),
                  pl.BlockSpec((1, 1, 6 * d), lambda bi: (mod_row, 0, 0)),
                  vec,
                  _resident((len(POOL_WINDOWS), n, n)),
                  _resident((len(POOL_WINDOWS), POOL_CH, POOL_CH)),
                  vec],
        out_specs=pl.BlockSpec((1, n, d), lambda bi: (bi, 0, 0)),
        compiler_params=_cparams(("arbitrary",)),
        name="pool_seq",
    )(x, mod, g, aseq, pw, ps)


def kernel(x, c, ctx, c_ctx, ada_w, ada_b, norm1_g, norm2_g, s5_a_re, s5_a_im, s5_log_dt, s5_b_re, s5_b_im, s5_c_re, s5_c_im, s5_d, s5_glu_w, s5_glu_b, pool_w, pool_scale, mlp_w1, mlp_b1, mlp_w2, mlp_b2, final_g):
    bsz, n_tok, d = x.shape
    n_ctx = ctx.shape[1]
    depth = ada_w.shape[0]
    assert d == D_MODEL and n_tok % (POOL_BLOCK_ROWS * GRID_W) == 0
    assert LANES % bsz == 0 and bsz * CTX_SEG <= LANES and bsz < 8
    lat_chunks = n_tok // S5_CHUNK
    ctx_chunks = n_ctx // S5_CHUNK
    assert n_tok % S5_CHUNK == 0 and n_ctx % S5_CHUNK == 0 and (bsz * lat_chunks) % LANES == 0
    assert ctx_chunks + 2 * CTX_OFF <= CTX_SEG
    nv = bsz * lat_chunks // LANES
    pows = _s5_pows(nv, LANES // bsz)
    last_ctx_reader = ((depth - 1) // N_MIXERS) * N_MIXERS
    ctx_row = bsz
    tt = min(512, n_tok)
    tt_ctx = min(512, n_ctx)

    cc = jnp.zeros((8, d), _F32).at[:bsz].set(c).at[ctx_row].set(c_ctx)
    mods = _modulation(cc, ada_w, ada_b)
    w1 = mlp_w1.astype(_BF16)
    w2 = mlp_w2.astype(_BF16)
    glu_w = s5_glu_w.astype(_BF16)
    pw = pool_w.astype(_BF16)

    h_ctx = ctx
    for i in range(depth):
        ctx_in = i <= last_ctx_reader
        ctx_out = i < last_ctx_reader
        j = i // N_MIXERS
        mod = mods[i].reshape(8, 1, 6 * d)
        g1n = norm1_g[i].reshape(1, d)
        g2n = norm2_g[i].reshape(1, d)
        if i % N_MIXERS == 0:
            mt, bst, cst, lampow = _s5_tables(s5_a_re[j], s5_a_im[j], s5_log_dt[j], s5_b_re[j], s5_b_im[j],
                                              s5_c_re[j], s5_c_im[j], pows)
            dsk = s5_d[j].reshape(1, d)
            gb = s5_glu_b[j].reshape(1, d)
            if ctx_in:
                utc = _ctx_to_ut(_prenorm(h_ctx, mod, ctx_row, g1n, tt_ctx))
            else:
                utc = jnp.zeros((S5_GROUPS, S5_ROWS, LANES), _BF16)
            utl = _s5pre(x, mod, g1n)
            ytl, ytc = _s5core(utl, utc, mt, bst, cst, lampow, bsz, ctx_chunks, pows, ctx_out)
            x = _s5post(x, ytl, mod, g1n, dsk, glu_w[j], gb)
            if ctx_out:
                h_ctx = _s5post_ctx(h_ctx, _ctx_from_yt(ytc, bsz, n_ctx), mod, ctx_row, g1n, dsk,
                                    glu_w[j], gb)
        else:
            ps = pool_scale[j].reshape(1, d)
            x = _pool_grid(x, mod, g1n, pw[j], ps)
            if ctx_out:
                h_ctx = _pool_seq(h_ctx, mod, ctx_row, g1n, pw[j], ps)
        final = i == depth - 1
        x = _mlp(x, mod, None, g2n, w1[i], mlp_b1[i].reshape(1, D_FF), w2[i], mlp_b2[i].reshape(1, d),
                 final_g.reshape(1, d), tt, final)
        if ctx_out:
            h_ctx = _mlp(h_ctx, mod, ctx_row, g2n, w1[i], mlp_b1[i].reshape(1, D_FF), w2[i],
                         mlp_b2[i].reshape(1, d), final_g.reshape(1, d), tt_ctx, False)
    return x
```

```python
import functools
import math

import numpy as np
import jax
import jax.numpy as jnp
from jax import lax
from jax.experimental import pallas as pl
from jax.experimental.pallas import tpu as pltpu

D_MODEL = 1024
GRID_W = 64
S5_GROUP = 16
S5_GROUPS = D_MODEL // S5_GROUP
S5_STATE = 64
POOL_WINDOWS = (2, 4, 8, 16)
POOL_CH = D_MODEL // len(POOL_WINDOWS)
D_FF = 4 * D_MODEL
N_MIXERS = 2
EPS = 1e-6
LANES = 128

S5_CHUNK = 32
S5_ROWS = S5_CHUNK * S5_GROUP
S5_NSTATE = 4 * S5_STATE
CTX_SEG = 16
CTX_OFF = 2
S5_PRE_TOK = 8
S5_POST_TOK = 8
V7X_VMEM_LIMIT_BYTES = 56 * 1024 * 1024

_F32 = jnp.float32
_BF16 = jnp.bfloat16


def _cparams(sem):
    return pltpu.CompilerParams(dimension_semantics=sem, vmem_limit_bytes=V7X_VMEM_LIMIT_BYTES)


def _resident(shape):
    nd = len(shape)
    return pl.BlockSpec(shape, lambda *_: (0,) * nd, pipeline_mode=pl.Buffered(1))


def _mod_kernel(cc_ref, w_ref, b_ref, o_ref):
    c = cc_ref[...]
    a = c * jax.nn.sigmoid(c)
    o_ref[0] = jnp.dot(a, w_ref[0], preferred_element_type=_F32,
                       precision=lax.Precision.HIGHEST) + b_ref[0]


def _modulation(cc, ada_w, ada_b):
    depth, d, n = ada_w.shape
    tn = 1536
    return pl.pallas_call(
        _mod_kernel,
        out_shape=jax.ShapeDtypeStruct((depth, 8, n), _F32),
        grid=(depth, n // tn),
        in_specs=[pl.BlockSpec((8, d), lambda i, j: (0, 0)),
                  pl.BlockSpec((1, d, tn), lambda i, j: (i, 0, j)),
                  pl.BlockSpec((1, 1, tn), lambda i, j: (i, 0, j))],
        out_specs=pl.BlockSpec((1, 8, tn), lambda i, j: (i, 0, j)),
        compiler_params=_cparams(("arbitrary", "arbitrary")),
        name="modulation",
    )(cc, ada_w, ada_b.reshape(depth, 1, n))


def _rms_mod(xf, g, shift, scale):
    ms = jnp.mean(xf * xf, axis=-1, keepdims=True)
    return (xf * lax.rsqrt(ms + EPS) * g) * (1.0 + scale) + shift


def _mod_slices(m):
    d = D_MODEL
    return tuple(m[:, k * d:(k + 1) * d] for k in range(6))


def _gelu_tanh(x):
    return 0.5 * x * (1.0 + jnp.tanh(math.sqrt(2.0 / math.pi) * (x + 0.044715 * (x * x * x))))


def _cmul(a, yr, yi):
    ar, ai = a
    return ar * yr - ai * yi, ar * yi + ai * yr


def _bf16_parts(x):
    hi = x.astype(_BF16)
    r1 = x - hi.astype(_F32)
    mid = r1.astype(_BF16)
    lo = (r1 - mid.astype(_F32)).astype(_BF16)
    return hi, mid, lo


def _select_cols(x, sel):
    return sum(jnp.dot(part, sel, preferred_element_type=_F32) for part in _bf16_parts(x))


def _select_rows(sel, x):
    return sum(jnp.dot(sel, part, preferred_element_type=_F32) for part in _bf16_parts(x))


def _prenorm_kernel(x_ref, mod_ref, g_ref, o_ref):
    sh1, sc1 = _mod_slices(mod_ref[0])[:2]
    o_ref[0] = _rms_mod(x_ref[0], g_ref[...], sh1, sc1).astype(o_ref.dtype)


def _prenorm(x, mod, mod_row, g, tt):
    b, n, d = x.shape
    return pl.pallas_call(
        _prenorm_kernel,
        out_shape=jax.ShapeDtypeStruct((b, n, d), _BF16),
        grid=(b, n // tt),
        in_specs=[pl.BlockSpec((1, tt, d), lambda bi, t: (bi, t, 0)),
                  pl.BlockSpec((1, 1, 6 * d), lambda bi, t: (mod_row, 0, 0)),
                  pl.BlockSpec((1, d), lambda bi, t: (0, 0))],
        out_specs=pl.BlockSpec((1, tt, d), lambda bi, t: (bi, t, 0)),
        compiler_params=_cparams(("arbitrary", "arbitrary")),
        name="prenorm",
    )(x, mod, g)


def _s5pre_kernel(x_ref, mod_ref, g_ref, ut_ref, *, nb, k):
    gn = g_ref[...]
    mods = [_mod_slices(mod_ref[b])[:2] for b in range(nb)]
    xs = [jnp.swapaxes(x_ref[b], 0, 1) for b in range(nb)]
    for j in range(k):
        xn = jnp.concatenate([_rms_mod(xs[b][j], gn, *mods[b]) for b in range(nb)], axis=0)
        ut_ref[:, j * S5_GROUP:(j + 1) * S5_GROUP, :] = (
            xn.T.astype(_BF16).reshape(S5_GROUPS, S5_GROUP, LANES))


def _s5pre(x, mod, g):
    b, n, d = x.shape
    lg = LANES // b
    per = n // lg
    k = S5_PRE_TOK
    blocks_per_v = S5_CHUNK // k
    ncol = b * (n // S5_CHUNK)
    return pl.pallas_call(
        functools.partial(_s5pre_kernel, nb=b, k=k),
        out_shape=jax.ShapeDtypeStruct((S5_GROUPS, S5_ROWS, ncol), _BF16),
        grid=(per // k,),
        in_specs=[pl.BlockSpec((b, lg, k, d), lambda r: (0, 0, r, 0)),
                  pl.BlockSpec((8, 1, 6 * d), lambda r: (0, 0, 0)),
                  pl.BlockSpec((1, d), lambda r: (0, 0))],
        out_specs=pl.BlockSpec((S5_GROUPS, k * S5_GROUP, LANES),
                               lambda r: (0, r % blocks_per_v, r // blocks_per_v)),
        compiler_params=_cparams(("arbitrary",)),
        name="s5pre",
    )(x.reshape(b, lg, per, d), mod, g)


def _lane_shift(v, dist, pos, seg, reverse):
    n = v.shape[1]
    if reverse:
        return jnp.where(pos < seg - dist, pltpu.roll(v, n - dist, axis=1), 0.0)
    return jnp.where(pos >= dist, pltpu.roll(v, dist, axis=1), 0.0)


def _lane_scan(xr, xi, lam, unit, pos, seg, reverse):
    k = 0
    while (1 << k) < seg:
        dist = 1 << k
        mr, mi = _cmul(lam(unit * dist), _lane_shift(xr, dist, pos, seg, reverse),
                       _lane_shift(xi, dist, pos, seg, reverse))
        xr, xi = xr + mr, xi + mi
        k += 1
    return _lane_shift(xr, 1, pos, seg, reverse), _lane_shift(xi, 1, pos, seg, reverse)


def _s5core_kernel(utl_ref, utc_ref, mt_ref, bst_ref, cst_ref, lp_ref, ytl_ref, *ytc_out,
                   nb, ctx_chunks, pows):
    p = S5_STATE
    ul = utl_ref[0]
    uc = utc_ref[0]
    ncol = ul.shape[1]
    nv = ncol // LANES
    lg = LANES // nb
    bst, cst, mt = bst_ref[0], cst_ref[0], mt_ref[0]
    lp = lp_ref[0]

    def lam_of(d):
        def lam(n):
            k = pows.index(n)
            return lp[2 * d][:, k:k + 1], lp[2 * d + 1][:, k:k + 1]
        return lam

    lane = lax.broadcasted_iota(jnp.int32, (1, LANES), 1)
    sc = jnp.dot(bst, uc, preferred_element_type=_F32)
    posc = lane & (CTX_SEG - 1)
    hc = []
    for d in range(2):
        r0 = 2 * p * d
        hr, hi = _lane_scan(sc[r0:r0 + p], sc[r0 + p:r0 + 2 * p], lam_of(d), 1, posc, CTX_SEG, d == 1)
        hc += [hr, hi]
    sl = jnp.dot(bst, ul, preferred_element_type=_F32)
    posl = lane & (lg - 1)
    bidl = lane >> int(math.log2(lg))
    hl = []
    for d in range(2):
        lam = lam_of(d)
        r0 = 2 * p * d
        sr = [sl[r0:r0 + p, v * LANES:(v + 1) * LANES] for v in range(nv)]
        si = [sl[r0 + p:r0 + 2 * p, v * LANES:(v + 1) * LANES] for v in range(nv)]
        order = list(range(nv)) if d == 0 else list(range(nv - 1, -1, -1))
        ir, ii = {}, {}
        prev = None
        for v in order:
            if prev is None:
                ir[v], ii[v] = sr[v], si[v]
            else:
                mr, mi = _cmul(lam(1), ir[prev], ii[prev])
                ir[v], ii[v] = mr + sr[v], mi + si[v]
            prev = v
        entry = 0 if d == 0 else lg - 1
        src = (CTX_OFF + ctx_chunks) if d == 0 else (CTX_OFF - 1)
        h0r = jnp.zeros((p, LANES), _F32)
        h0i = jnp.zeros((p, LANES), _F32)
        for b in range(nb):
            m = (bidl == b) & (posl == entry)
            col = b * CTX_SEG + src
            h0r = jnp.where(m, hc[2 * d][:, col:col + 1], h0r)
            h0i = jnp.where(m, hc[2 * d + 1][:, col:col + 1], h0i)
        jr, ji = _cmul(lam(nv), h0r, h0i)
        er, ei = _lane_scan(ir[prev] + jr, ii[prev] + ji, lam, nv, posl, lg, d == 1)
        er, ei = er + h0r, ei + h0i
        hr, hi = {order[0]: er}, {order[0]: ei}
        for n, v in enumerate(order[1:], start=1):
            mr, mi = _cmul(lam(n), er, ei)
            hr[v], hi[v] = ir[order[n - 1]] + mr, ii[order[n - 1]] + mi
        hl += [jnp.concatenate([hr[v] for v in range(nv)], axis=1),
               jnp.concatenate([hi[v] for v in range(nv)], axis=1)]
    h = jnp.concatenate(hl, axis=0).astype(_BF16)
    y = jnp.dot(mt, ul, preferred_element_type=_F32)
    ytl_ref[0] = y + jnp.dot(cst, h, preferred_element_type=_F32)
    if ytc_out:
        hcb = jnp.concatenate(hc, axis=0).astype(_BF16)
        yc = jnp.dot(mt, uc, preferred_element_type=_F32)
        ytc_out[0][0] = yc + jnp.dot(cst, hcb, preferred_element_type=_F32)


def _s5_pows(nv, lg):
    pows = set(range(1, nv + 1))
    pows |= {nv << k for k in range(int(math.log2(lg)))}
    pows |= {1 << k for k in range(int(math.log2(CTX_SEG)))}
    return tuple(sorted(pows))


def _s5core(utl, utc, mt, bst, cst, lampow, nb, ctx_chunks, pows, want_yc):
    g, rows, n = utl.shape
    out_shape = [jax.ShapeDtypeStruct((g, rows, n), _F32)]
    out_specs = [pl.BlockSpec((1, rows, n), lambda i: (i, 0, 0))]
    if want_yc:
        out_shape.append(jax.ShapeDtypeStruct((g, rows, LANES), _F32))
        out_specs.append(pl.BlockSpec((1, rows, LANES), lambda i: (i, 0, 0)))
    res = pl.pallas_call(
        functools.partial(_s5core_kernel, nb=nb, ctx_chunks=ctx_chunks, pows=pows),
        out_shape=out_shape,
        grid=(g,),
        in_specs=[pl.BlockSpec((1, rows, n), lambda i: (i, 0, 0)),
                  pl.BlockSpec((1, rows, LANES), lambda i: (i, 0, 0)),
                  pl.BlockSpec((1, rows, rows), lambda i: (i, 0, 0)),
                  pl.BlockSpec((1, S5_NSTATE, rows), lambda i: (i, 0, 0)),
                  pl.BlockSpec((1, rows, S5_NSTATE), lambda i: (i, 0, 0)),
                  pl.BlockSpec((1, 4, S5_STATE, LANES), lambda i: (i, 0, 0, 0))],
        out_specs=out_specs,
        compiler_params=_cparams(("arbitrary",)),
        name="s5core_yc" if want_yc else "s5core",
    )(utl, utc, mt, bst, cst, lampow)
    return (res[0], res[1]) if want_yc else (res[0], None)


def _s5prep_kernel(acol_ref, arow_ref, ldt_ref, bre_ref, bim_ref, cre_ref, cim_ref, c1_ref, c2_ref,
                   esel_ref, etile_ref, e2_ref, e3_ref, pows_ref, mt_ref, bst_ref, cst_ref, lp_ref):
    hp = lax.Precision.HIGHEST
    dot = functools.partial(jnp.dot, precision=hp, preferred_element_type=_F32)
    t_len, p = S5_CHUNK, S5_STATE
    lane = lax.broadcasted_iota(jnp.int32, (1, LANES), 1)
    n_lane = lane.astype(_F32)
    n_sub = lax.broadcasted_iota(jnp.int32, (p, 1), 0).astype(_F32)
    strips = []
    for d in range(2):
        dt = jnp.exp(ldt_ref[d, 0])
        a = acol_ref[d, 0]
        ar, ai = a[:, 0:1], a[:, 1:2]
        dar, dai = ar * dt, ai * dt
        mag = jnp.exp(dar)
        lbr, lbi = mag * jnp.cos(dai), mag * jnp.sin(dai)
        den = ar * ar + ai * ai
        nr, ni = lbr - 1.0, lbi
        kr = (nr * ar + ni * ai) / den
        ki = (ni * ar - nr * ai) / den
        bbr = kr * bre_ref[d, 0] - ki * bim_ref[d, 0]
        bbi = kr * bim_ref[d, 0] + ki * bre_ref[d, 0]
        pm, pa = jnp.exp(n_lane * dar), n_lane * dai
        wr = _select_cols(pm * jnp.cos(pa), esel_ref[d])
        wi = _select_cols(pm * jnp.sin(pa), esel_ref[d])
        btr = _select_cols(bbr, etile_ref[...])
        bti = _select_cols(bbi, etile_ref[...])
        bsr, bsi = wr * btr - wi * bti, wr * bti + wi * btr
        bst_ref[0, (2 * d) * p:(2 * d + 1) * p, :] = bsr.astype(_BF16)
        bst_ref[0, (2 * d + 1) * p:(2 * d + 2) * p, :] = bsi.astype(_BF16)
        strips.append(dot(cre_ref[d, 0], bsr) - dot(cim_ref[d, 0], bsi))
        arow = arow_ref[d, 0]
        tm, ta = jnp.exp(n_sub * (arow[0:1] * dt)), n_sub * (arow[1:2] * dt)
        cs, sn = jnp.cos(ta), jnp.sin(ta)
        w1 = _select_rows(e2_ref[d], tm * jnp.where(lane < p, cs, sn))
        w2 = _select_rows(e2_ref[d], tm * jnp.where(lane < p, sn, cs))
        c1 = _select_rows(e3_ref[...], c1_ref[d, 0])
        c2 = _select_rows(e3_ref[...], c2_ref[d, 0])
        cst_ref[0, :, d * LANES:(d + 1) * LANES] = (c1 * w1 + c2 * w2).astype(_BF16)
        m = pows_ref[...] * float(t_len)
        qm, qa = jnp.exp(m * dar), m * dai
        lp_ref[0, 2 * d] = qm * jnp.cos(qa)
        lp_ref[0, 2 * d + 1] = qm * jnp.sin(qa)
    rf, rb = strips
    lane_w = lax.broadcasted_iota(jnp.int32, (1, S5_ROWS), 1)
    for t in range(t_len):
        sf = (t_len - 1 - t) * S5_GROUP
        f = rf if sf == 0 else pltpu.roll(rf, S5_ROWS - sf, axis=1)
        f = jnp.where(lane_w < (t + 1) * S5_GROUP, f, 0.0)
        sb = t * S5_GROUP
        bk = rb if sb == 0 else pltpu.roll(rb, sb, axis=1)
        bk = jnp.where(lane_w >= sb, bk, 0.0)
        mt_ref[0, t * S5_GROUP:(t + 1) * S5_GROUP, :] = (f + bk).astype(_BF16)


def _s5prep_constants(pows):
    t_len, h = S5_CHUNK, S5_GROUP
    s_of = np.arange(S5_ROWS) // h
    h_of = np.arange(S5_ROWS) % h
    n128 = np.arange(LANES)
    esel = np.stack([n128[:, None] == (t_len - 1 - s_of)[None, :], n128[:, None] == s_of[None, :]])
    etile = np.arange(h)[:, None] == h_of[None, :]
    n64 = np.arange(S5_STATE)
    e2 = np.stack([(s_of + 1)[:, None] == n64[None, :], (t_len - s_of)[:, None] == n64[None, :]])
    e3 = h_of[:, None] == np.arange(h)[None, :]
    prow = np.zeros((1, LANES), np.float32)
    prow[0, :len(pows)] = pows
    f = lambda m: jnp.asarray(m, dtype=_BF16)
    return f(esel), f(etile), f(e2), f(e3), jnp.asarray(prow)


def _s5prep(a_re, a_im, log_dt, b_re, b_im, c_re, c_im, pows):
    g, p, h = S5_GROUPS, S5_STATE, S5_GROUP
    acol = jnp.stack([a_re, a_im], axis=-1)
    arow = jnp.stack([jnp.concatenate([a_re, a_re], -1),
                      jnp.concatenate([a_im, a_im], -1)], axis=2)
    ldt = log_dt.reshape(2, g, 1, 1)
    c1 = jnp.concatenate([c_re, -c_re], -1)
    c2 = jnp.concatenate([-c_im, -c_im], -1)
    esel, etile, e2, e3, prow = _s5prep_constants(pows)

    def per_g(*tail):
        return pl.BlockSpec((2, 1) + tail, lambda i: (0, i) + (0,) * len(tail))

    return pl.pallas_call(
        _s5prep_kernel,
        out_shape=[jax.ShapeDtypeStruct((g, S5_ROWS, S5_ROWS), _BF16),
                   jax.ShapeDtypeStruct((g, S5_NSTATE, S5_ROWS), _BF16),
                   jax.ShapeDtypeStruct((g, S5_ROWS, S5_NSTATE), _BF16),
                   jax.ShapeDtypeStruct((g, 4, p, LANES), _F32)],
        grid=(g,),
        in_specs=[per_g(p, 2), per_g(2, LANES), per_g(1, 1), per_g(p, h), per_g(p, h),
                  per_g(h, p), per_g(h, p), per_g(h, LANES), per_g(h, LANES),
                  _resident(esel.shape), _resident(etile.shape), _resident(e2.shape), _resident(e3.shape),
                  _resident(prow.shape)],
        out_specs=[pl.BlockSpec((1, S5_ROWS, S5_ROWS), lambda i: (i, 0, 0)),
                   pl.BlockSpec((1, S5_NSTATE, S5_ROWS), lambda i: (i, 0, 0)),
                   pl.BlockSpec((1, S5_ROWS, S5_NSTATE), lambda i: (i, 0, 0)),
                   pl.BlockSpec((1, 4, p, LANES), lambda i: (i, 0, 0, 0))],
        compiler_params=_cparams(("arbitrary",)),
        name="s5prep",
    )(acol, arow, ldt, b_re, b_im, c_re, c_im, c1, c2, esel, etile, e2, e3, prow)


def _ctx_to_ut(xn):
    b, n, _ = xn.shape
    c = n // S5_CHUNK
    v = xn.reshape(b, c, S5_CHUNK, S5_GROUPS, S5_GROUP).transpose(3, 2, 4, 0, 1)
    v = jnp.pad(v, ((0, 0),) * 4 + ((CTX_OFF, CTX_SEG - c - CTX_OFF),))
    v = v.reshape(S5_GROUPS, S5_ROWS, b * CTX_SEG)
    return jnp.pad(v, ((0, 0), (0, 0), (0, LANES - b * CTX_SEG)))


def _ctx_from_yt(yt, b, n):
    c = n // S5_CHUNK
    v = yt[:, :, :b * CTX_SEG].reshape(S5_GROUPS, S5_CHUNK, S5_GROUP, b, CTX_SEG)
    v = v[..., CTX_OFF:CTX_OFF + c]
    return v.transpose(3, 4, 1, 0, 2).reshape(b, n, D_MODEL)


def _s5_glu(xf, xn, y, dsk, gw, gb, g1):
    z = _gelu_tanh(y + dsk * xn)
    gate = jax.nn.sigmoid(jnp.dot(z.astype(_BF16), gw, preferred_element_type=_F32) + gb)
    return xf + g1 * (z * gate)


def _s5post_ctx_kernel(x_ref, y_ref, mod_ref, g_ref, dsk_ref, gw_ref, gb_ref, o_ref):
    sh1, sc1, g1 = _mod_slices(mod_ref[0])[:3]
    xf = x_ref[0]
    xn = _rms_mod(xf, g_ref[...], sh1, sc1)
    o_ref[0] = _s5_glu(xf, xn, y_ref[0], dsk_ref[...], gw_ref[...], gb_ref[...], g1)


def _s5post_ctx(x, y, mod, mod_row, g, dsk, gw, gb):
    b, n, d = x.shape
    tok = pl.BlockSpec((1, n, d), lambda bi: (bi, 0, 0))
    vec = pl.BlockSpec((1, d), lambda bi: (0, 0))
    return pl.pallas_call(
        _s5post_ctx_kernel,
        out_shape=jax.ShapeDtypeStruct((b, n, d), _F32),
        grid=(b,),
        in_specs=[tok, tok, pl.BlockSpec((1, 1, 6 * d), lambda bi: (mod_row, 0, 0)),
                  vec, vec, _resident((d, d)), vec],
        out_specs=tok,
        compiler_params=_cparams(("arbitrary",)),
        name="s5post_ctx",
    )(x, y, mod, g, dsk, gw, gb)


def _s5post_kernel(x_ref, yt_ref, mod_ref, g_ref, dsk_ref, gw_ref, gb_ref, o_ref, *, nb, k):
    d = D_MODEL
    lg = LANES // nb
    gn = g_ref[...]
    dsk = dsk_ref[...]
    mods = [_mod_slices(mod_ref[b])[:3] for b in range(nb)]
    xb = [jnp.swapaxes(x_ref[b], 0, 1) for b in range(nb)]
    zs = []
    for j in range(k):
        y = yt_ref[:, j * S5_GROUP:(j + 1) * S5_GROUP, :].reshape(d, LANES).T
        for b in range(nb):
            xn = _rms_mod(xb[b][j], gn, mods[b][0], mods[b][1])
            zs.append(_gelu_tanh(y[b * lg:(b + 1) * lg] + dsk * xn))
    z = jnp.concatenate(zs, axis=0)
    gate = jax.nn.sigmoid(jnp.dot(z.astype(_BF16), gw_ref[...], preferred_element_type=_F32) + gb_ref[...])
    o = z * gate
    for b in range(nb):
        res = jnp.stack([xb[b][j] + mods[b][2] * o[(j * nb + b) * lg:(j * nb + b + 1) * lg]
                         for j in range(k)], axis=0)
        o_ref[b] = jnp.swapaxes(res, 0, 1)


def _s5post(x, yt, mod, g, dsk, gw, gb):
    b, n, d = x.shape
    lg = LANES // b
    per = n // lg
    k = S5_POST_TOK
    blocks_per_v = S5_CHUNK // k
    tok = pl.BlockSpec((b, lg, k, d), lambda r: (0, 0, r, 0))
    vec = pl.BlockSpec((1, d), lambda r: (0, 0))
    out = pl.pallas_call(
        functools.partial(_s5post_kernel, nb=b, k=k),
        out_shape=jax.ShapeDtypeStruct((b, lg, per, d), _F32),
        grid=(per // k,),
        in_specs=[tok,
                  pl.BlockSpec((S5_GROUPS, k * S5_GROUP, LANES),
                               lambda r: (0, r % blocks_per_v, r // blocks_per_v)),
                  pl.BlockSpec((8, 1, 6 * d), lambda r: (0, 0, 0)),
                  vec, vec, _resident((d, d)), vec],
        out_specs=tok,
        compiler_params=_cparams(("arbitrary",)),
        name="s5post",
    )(x.reshape(b, lg, per, d), yt, mod, g, dsk, gw, gb)
    return out.reshape(b, n, d)


def _mlp_kernel(x_ref, mod_ref, g_ref, w1_ref, b1_ref, w2_ref, b2_ref, fg_ref, o_ref, *, final):
    sh2, sc2, g2 = _mod_slices(mod_ref[0])[3:]
    xf = x_ref[0]
    xn = _rms_mod(xf, g_ref[...], sh2, sc2).astype(_BF16)
    acc = jnp.zeros(xf.shape, _F32)
    fc = D_MODEL
    for j in range(D_FF // fc):
        a = jnp.dot(xn, w1_ref[:, j * fc:(j + 1) * fc], preferred_element_type=_F32)
        a = jnp.maximum(a + b1_ref[:, j * fc:(j + 1) * fc], 0.0)
        acc = acc + jnp.dot((a * a).astype(_BF16), w2_ref[j * fc:(j + 1) * fc, :],
                            preferred_element_type=_F32)
    out = xf + g2 * (acc + b2_ref[...])
    if final:
        ms = jnp.mean(out * out, axis=-1, keepdims=True)
        out = out * lax.rsqrt(ms + EPS) * fg_ref[...]
    o_ref[0] = out


def _mlp(x, mod, mod_row, g, w1, b1, w2, b2, fg, tt, final):
    b, n, d = x.shape
    row = (lambda bi: bi) if mod_row is None else (lambda bi: mod_row)
    tok = pl.BlockSpec((1, tt, d), lambda bi, t: (bi, t, 0))
    vec = pl.BlockSpec((1, d), lambda bi, t: (0, 0))
    return pl.pallas_call(
        functools.partial(_mlp_kernel, final=final),
        out_shape=jax.ShapeDtypeStruct((b, n, d), _F32),
        grid=(b, n // tt),
        in_specs=[tok, pl.BlockSpec((1, 1, 6 * d), lambda bi, t: (row(bi), 0, 0)), vec,
                  _resident((d, D_FF)), pl.BlockSpec((1, D_FF), lambda bi, t: (0, 0)),
                  _resident((D_FF, d)), vec, vec],
        out_specs=tok,
        compiler_params=_cparams(("arbitrary", "arbitrary")),
        name="mlp_final" if final else "mlp",
    )(x, mod, g, w1, b1, w2, b2, fg)


def _col_window_matrices(tile_rows):
    n = tile_rows * GRID_W
    t = np.arange(n)
    r, c = t // GRID_W, t % GRID_W
    mats = []
    for w in POOL_WINDOWS:
        lo, hi = np.maximum(c - w // 2, 0), np.minimum(c + w - w // 2, GRID_W)
        m = (r[:, None] == r[None, :]) & (c[None, :] >= lo[:, None]) & (c[None, :] < hi[:, None])
        mats.append(m)
    return jnp.asarray(np.stack(mats), dtype=_BF16)


def _seq_window_matrices(n):
    t = np.arange(n)
    mats = []
    for w in POOL_WINDOWS:
        lo, hi = np.maximum(t - w // 2, 0), np.minimum(t + w - w // 2, n)
        mats.append((t[None, :] >= lo[:, None]) & (t[None, :] < hi[:, None]))
    return jnp.asarray(np.stack(mats), dtype=_BF16)


POOL_BLOCK_ROWS = 16
POOL_HALO_ROWS = 8
POOL_TILE_ROWS = 4


def _pool_kernel(xm_ref, xt_ref, xb_ref, mod_ref, g_ref, acol_ref, pw_ref, ps_ref, o_ref, cbuf, *, rows):
    i = pl.program_id(1)
    nblk = pl.num_programs(1)
    sh1, sc1, g1 = _mod_slices(mod_ref[0])[:3]
    gn = g_ref[...]
    halo = POOL_HALO_ROWS * GRID_W
    main = POOL_BLOCK_ROWS * GRID_W
    tile = POOL_TILE_ROWS * GRID_W
    xm = xm_ref[0]
    xn_m = _rms_mod(xm, gn, sh1, sc1)
    xn_t = jnp.where(i > 0, _rms_mod(xt_ref[0], gn, sh1, sc1), 0.0)
    xn_b = jnp.where(i < nblk - 1, _rms_mod(xb_ref[0], gn, sh1, sc1), 0.0)
    xn_ext = jnp.concatenate([xn_t, xn_m, xn_b], axis=0).astype(_BF16)
    ntile = (main + 2 * halo) // tile
    tok = lax.broadcasted_iota(jnp.int32, (main, POOL_CH), 0)
    r = i * POOL_BLOCK_ROWS + (tok >> int(math.log2(GRID_W)))
    c = tok & (GRID_W - 1)
    for gi, w in enumerate(POOL_WINDOWS):
        ch = slice(gi * POOL_CH, (gi + 1) * POOL_CH)
        a = acol_ref[gi]
        for k in range(ntile):
            cbuf[k * tile:(k + 1) * tile, :] = jnp.dot(
                a, xn_ext[k * tile:(k + 1) * tile, ch], preferred_element_type=_F32)
        tot = jnp.zeros((main, POOL_CH), _F32)
        for k in range(-(w // 2), w - w // 2):
            off = halo + k * GRID_W
            tot = tot + cbuf[off:off + main, :]
        rcnt = jnp.minimum(r + (w - w // 2), rows) - jnp.maximum(r - w // 2, 0)
        ccnt = jnp.minimum(c + (w - w // 2), GRID_W) - jnp.maximum(c - w // 2, 0)
        p = tot / (rcnt * ccnt).astype(_F32) - xn_m[:, ch]
        y = jnp.dot(p.astype(_BF16), pw_ref[gi], preferred_element_type=_F32) * ps_ref[:, ch]
        o_ref[0, :, ch] = xm[:, ch] + g1[:, ch] * y


def _pool_grid(x, mod, g, pw, ps):
    b, n, d = x.shape
    rows = n // GRID_W
    main = POOL_BLOCK_ROWS * GRID_W
    halo = POOL_HALO_ROWS * GRID_W
    nblk = n // main
    ratio = main // halo
    nh = n // halo
    acol = _col_window_matrices(POOL_TILE_ROWS)
    tile = POOL_TILE_ROWS * GRID_W
    vec = pl.BlockSpec((1, d), lambda bi, i: (0, 0))
    return pl.pallas_call(
        functools.partial(_pool_kernel, rows=rows),
        out_shape=jax.ShapeDtypeStruct((b, n, d), _F32),
        grid=(b, nblk),
        in_specs=[pl.BlockSpec((1, main, d), lambda bi, i: (bi, i, 0)),
                  pl.BlockSpec((1, halo, d), lambda bi, i: (bi, jnp.maximum(i * ratio - 1, 0), 0)),
                  pl.BlockSpec((1, halo, d), lambda bi, i: (bi, jnp.minimum((i + 1) * ratio, nh - 1), 0)),
                  pl.BlockSpec((1, 1, 6 * d), lambda bi, i: (bi, 0, 0)),
                  vec,
                  _resident((len(POOL_WINDOWS), tile, tile)),
                  _resident((len(POOL_WINDOWS), POOL_CH, POOL_CH)),
                  vec],
        out_specs=pl.BlockSpec((1, main, d), lambda bi, i: (bi, i, 0)),
        scratch_shapes=[pltpu.VMEM((main + 2 * halo, POOL_CH), _F32)],
        compiler_params=_cparams(("arbitrary", "arbitrary")),
        name="pool_grid",
    )(x, x, x, mod, g, acol, pw, ps)


def _poolseq_kernel(x_ref, mod_ref, g_ref, aseq_ref, pw_ref, ps_ref, o_ref):
    sh1, sc1, g1 = _mod_slices(mod_ref[0])[:3]
    xf = x_ref[0]
    n = xf.shape[0]
    xn = _rms_mod(xf, g_ref[...], sh1, sc1)
    xb = xn.astype(_BF16)
    t = lax.broadcasted_iota(jnp.int32, (n, POOL_CH), 0)
    for gi, w in enumerate(POOL_WINDOWS):
        ch = slice(gi * POOL_CH, (gi + 1) * POOL_CH)
        tot = jnp.dot(aseq_ref[gi], xb[:, ch], preferred_element_type=_F32)
        cnt = jnp.minimum(t + (w - w // 2), n) - jnp.maximum(t - w // 2, 0)
        p = tot / cnt.astype(_F32) - xn[:, ch]
        y = jnp.dot(p.astype(_BF16), pw_ref[gi], preferred_element_type=_F32) * ps_ref[:, ch]
        o_ref[0, :, ch] = xf[:, ch] + g1[:, ch] * y


def _pool_seq(x, mod, mod_row, g, pw, ps):
    b, n, d = x.shape
    aseq = _seq_window_matrices(n)
    vec = pl.BlockSpec((1, d), lambda bi: (0, 0))
    return pl.pallas_call(
        _poolseq_kernel,
        out_shape=jax.ShapeDtypeStruct((b, n, d), _F32),
        grid=(b,),
        in_specs=[pl.BlockSpec((1, n, d), lambda bi: (bi, 0, 0)),
                  pl.BlockSpec((1, 1, 6 * d), lambda bi: (mod_row, 0, 0)),
                  vec,
                  _resident((len(POOL_WINDOWS), n, n)),
                  _resident((len(POOL_WINDOWS), POOL_CH, POOL_CH)),
                  vec],
        out_specs=pl.BlockSpec((1, n, d), lambda bi: (bi, 0, 0)),
        compiler_params=_cparams(("arbitrary",)),
        name="pool_seq",
    )(x, mod, g, aseq, pw, ps)


def kernel(x, c, ctx, c_ctx, ada_w, ada_b, norm1_g, norm2_g, s5_a_re, s5_a_im, s5_log_dt, s5_b_re, s5_b_im, s5_c_re, s5_c_im, s5_d, s5_glu_w, s5_glu_b, pool_w, pool_scale, mlp_w1, mlp_b1, mlp_w2, mlp_b2, final_g):
    bsz, n_tok, d = x.shape
    n_ctx = ctx.shape[1]
    depth = ada_w.shape[0]
    assert d == D_MODEL and n_tok % (POOL_BLOCK_ROWS * GRID_W) == 0
    assert LANES % bsz == 0 and bsz * CTX_SEG <= LANES and bsz < 8
    lat_chunks = n_tok // S5_CHUNK
    ctx_chunks = n_ctx // S5_CHUNK
    assert n_tok % S5_CHUNK == 0 and n_ctx % S5_CHUNK == 0 and (bsz * lat_chunks) % LANES == 0
    assert ctx_chunks + 2 * CTX_OFF <= CTX_SEG
    nv = bsz * lat_chunks // LANES
    pows = _s5_pows(nv, LANES // bsz)
    last_ctx_reader = ((depth - 1) // N_MIXERS) * N_MIXERS
    ctx_row = bsz
    tt = min(512, n_tok)
    tt_ctx = min(512, n_ctx)

    cc = jnp.zeros((8, d), _F32).at[:bsz].set(c).at[ctx_row].set(c_ctx)
    mods = _modulation(cc, ada_w, ada_b)
    w1 = mlp_w1.astype(_BF16)
    w2 = mlp_w2.astype(_BF16)
    glu_w = s5_glu_w.astype(_BF16)
    pw = pool_w.astype(_BF16)

    h_ctx = ctx
    for i in range(depth):
        ctx_in = i <= last_ctx_reader
        ctx_out = i < last_ctx_reader
        j = i // N_MIXERS
        mod = mods[i].reshape(8, 1, 6 * d)
        g1n = norm1_g[i].reshape(1, d)
        g2n = norm2_g[i].reshape(1, d)
        if i % N_MIXERS == 0:
            mt, bst, cst, lampow = _s5prep(s5_a_re[j], s5_a_im[j], s5_log_dt[j], s5_b_re[j], s5_b_im[j],
                                           s5_c_re[j], s5_c_im[j], pows)
            dsk = s5_d[j].reshape(1, d)
            gb = s5_glu_b[j].reshape(1, d)
            if ctx_in:
                utc = _ctx_to_ut(_prenorm(h_ctx, mod, ctx_row, g1n, tt_ctx))
            else:
                utc = jnp.zeros((S5_GROUPS, S5_ROWS, LANES), _BF16)
            utl = _s5pre(x, mod, g1n)
            ytl, ytc = _s5core(utl, utc, mt, bst, cst, lampow, bsz, ctx_chunks, pows, ctx_out)
            x = _s5post(x, ytl, mod, g1n, dsk, glu_w[j], gb)
            if ctx_out:
                h_ctx = _s5post_ctx(h_ctx, _ctx_from_yt(ytc, bsz, n_ctx), mod, ctx_row, g1n, dsk,
                                    glu_w[j], gb)
        else:
            ps = pool_scale[j].reshape(1, d)
            x = _pool_grid(x, mod, g1n, pw[j], ps)
            if ctx_out:
                h_ctx = _pool_seq(h_ctx, mod, ctx_row, g1n, pw[j], ps)
        final = i == depth - 1
        x = _mlp(x, mod, None, g2n, w1[i], mlp_b1[i].reshape(1, D_FF), w2[i], mlp_b2[i].reshape(1, d),
                 final_g.reshape(1, d), tt, final)
        if ctx_out:
            h_ctx = _mlp(h_ctx, mod, ctx_row, g2n, w1[i], mlp_b1[i].reshape(1, D_FF), w2[i],
                         mlp_b2[i].reshape(1, d), final_g.reshape(1, d), tt_ctx, False)
    return x
```

```python
import functools
import math

import numpy as np
import jax
import jax.numpy as jnp
from jax import lax
from jax.experimental import pallas as pl
from jax.experimental.pallas import tpu as pltpu

D_MODEL = 1024
GRID_W = 64
S5_GROUP = 16
S5_GROUPS = D_MODEL // S5_GROUP
S5_STATE = 64
POOL_WINDOWS = (2, 4, 8, 16)
POOL_CH = D_MODEL // len(POOL_WINDOWS)
D_FF = 4 * D_MODEL
N_MIXERS = 2
EPS = 1e-6
LANES = 128

S5_CHUNK = 32
S5_ROWS = S5_CHUNK * S5_GROUP
S5_NSTATE = 4 * S5_STATE
CTX_SEG = 16
CTX_OFF = 2
S5_PRE_TOK = 8
S5_POST_TOK = 8
S5_CORE_GROUPS = 2
S5_PREP_GROUPS = 4
V7X_VMEM_LIMIT_BYTES = 56 * 1024 * 1024

_F32 = jnp.float32
_BF16 = jnp.bfloat16


def _cparams(sem):
    return pltpu.CompilerParams(dimension_semantics=sem, vmem_limit_bytes=V7X_VMEM_LIMIT_BYTES)


def _resident(shape):
    nd = len(shape)
    return pl.BlockSpec(shape, lambda *_: (0,) * nd, pipeline_mode=pl.Buffered(1))


def _resident_layer(shape, layer):
    nd = len(shape)
    return pl.BlockSpec((1,) + tuple(shape), lambda *_: (layer,) + (0,) * nd, pipeline_mode=pl.Buffered(1))


def _mod_kernel(cc_ref, w_ref, b_ref, o_ref):
    c = cc_ref[...]
    a = c * jax.nn.sigmoid(c)
    o_ref[0] = jnp.dot(a, w_ref[0], preferred_element_type=_F32,
                       precision=lax.Precision.HIGHEST) + b_ref[0]


def _modulation(cc, ada_w, ada_b):
    depth, d, n = ada_w.shape
    tn = 1536
    return pl.pallas_call(
        _mod_kernel,
        out_shape=jax.ShapeDtypeStruct((depth, 8, n), _F32),
        grid=(depth, n // tn),
        in_specs=[pl.BlockSpec((8, d), lambda i, j: (0, 0)),
                  pl.BlockSpec((1, d, tn), lambda i, j: (i, 0, j)),
                  pl.BlockSpec((1, 1, tn), lambda i, j: (i, 0, j))],
        out_specs=pl.BlockSpec((1, 8, tn), lambda i, j: (i, 0, j)),
        compiler_params=_cparams(("arbitrary", "arbitrary")),
        name="modulation",
    )(cc, ada_w, ada_b.reshape(depth, 1, n))


def _rms_mod(xf, g, shift, scale):
    ms = jnp.mean(xf * xf, axis=-1, keepdims=True)
    return (xf * lax.rsqrt(ms + EPS)) * (g * (1.0 + scale)) + shift


def _mod_slices(m):
    d = D_MODEL
    return tuple(m[:, k * d:(k + 1) * d] for k in range(6))


def _gelu_tanh(x):
    return 0.5 * x * (1.0 + jnp.tanh(math.sqrt(2.0 / math.pi) * (x + 0.044715 * (x * x * x))))


def _cmul(a, yr, yi):
    ar, ai = a
    return ar * yr - ai * yi, ar * yi + ai * yr


def _bf16_parts(x):
    hi = x.astype(_BF16)
    r1 = x - hi.astype(_F32)
    mid = r1.astype(_BF16)
    lo = (r1 - mid.astype(_F32)).astype(_BF16)
    return hi, mid, lo


def _select_cols(x, sel):
    return sum(jnp.dot(part, sel, preferred_element_type=_F32) for part in _bf16_parts(x))


def _select_rows(sel, x):
    return sum(jnp.dot(sel, part, preferred_element_type=_F32) for part in _bf16_parts(x))


def _prenorm_kernel(x_ref, mod_ref, g_ref, o_ref):
    sh1, sc1 = _mod_slices(mod_ref[0])[:2]
    o_ref[0] = _rms_mod(x_ref[0], g_ref[...], sh1, sc1).astype(o_ref.dtype)


def _prenorm(x, mod, mod_row, g, tt):
    b, n, d = x.shape
    return pl.pallas_call(
        _prenorm_kernel,
        out_shape=jax.ShapeDtypeStruct((b, n, d), _BF16),
        grid=(b, n // tt),
        in_specs=[pl.BlockSpec((1, tt, d), lambda bi, t: (bi, t, 0)),
                  pl.BlockSpec((1, 1, 6 * d), lambda bi, t: (mod_row, 0, 0)),
                  pl.BlockSpec((1, d), lambda bi, t: (0, 0))],
        out_specs=pl.BlockSpec((1, tt, d), lambda bi, t: (bi, t, 0)),
        compiler_params=_cparams(("arbitrary", "arbitrary")),
        name="prenorm",
    )(x, mod, g)


def _s5pre_kernel(x_ref, mod_ref, g_ref, ut_ref, *, nb, k):
    gn = g_ref[...]
    mods = [_mod_slices(mod_ref[b])[:2] for b in range(nb)]
    xs = [jnp.swapaxes(x_ref[b], 0, 1) for b in range(nb)]
    for j in range(k):
        xn = jnp.concatenate([_rms_mod(xs[b][j], gn, *mods[b]) for b in range(nb)], axis=0)
        ut_ref[:, 0, j * S5_GROUP:(j + 1) * S5_GROUP, :] = (
            xn.T.astype(_BF16).reshape(S5_GROUPS, S5_GROUP, LANES))


def _s5pre(x, mod, g):
    b, n, d = x.shape
    lg = LANES // b
    per = n // lg
    k = S5_PRE_TOK
    blocks_per_v = S5_CHUNK // k
    nv = b * (n // S5_CHUNK) // LANES
    return pl.pallas_call(
        functools.partial(_s5pre_kernel, nb=b, k=k),
        out_shape=jax.ShapeDtypeStruct((S5_GROUPS, nv, S5_ROWS, LANES), _BF16),
        grid=(per // k,),
        in_specs=[pl.BlockSpec((b, lg, k, d), lambda r: (0, 0, r, 0)),
                  pl.BlockSpec((8, 1, 6 * d), lambda r: (0, 0, 0)),
                  pl.BlockSpec((1, d), lambda r: (0, 0))],
        out_specs=pl.BlockSpec((S5_GROUPS, 1, k * S5_GROUP, LANES),
                               lambda r: (0, r // blocks_per_v, r % blocks_per_v, 0)),
        compiler_params=_cparams(("arbitrary",)),
        name="s5pre",
    )(x.reshape(b, lg, per, d), mod, g)


def _lane_shift(v, dist, pos, seg, reverse):
    n = v.shape[1]
    if reverse:
        return jnp.where(pos < seg - dist, pltpu.roll(v, n - dist, axis=1), 0.0)
    return jnp.where(pos >= dist, pltpu.roll(v, dist, axis=1), 0.0)


def _lane_scan(xr, xi, lam, unit, pos, seg, reverse):
    k = 0
    while (1 << k) < seg:
        dist = 1 << k
        mr, mi = _cmul(lam(unit * dist), _lane_shift(xr, dist, pos, seg, reverse),
                       _lane_shift(xi, dist, pos, seg, reverse))
        xr, xi = xr + mr, xi + mi
        k += 1
    return _lane_shift(xr, 1, pos, seg, reverse), _lane_shift(xi, 1, pos, seg, reverse)


def _s5core_kernel(utl_ref, utc_ref, mt_ref, bst_ref, cst_ref, lp_ref, ytl_ref, *ytc_out,
                   nb, ctx_chunks, pows):
    for q in range(utl_ref.shape[0]):
        _s5core_group(q, utl_ref, utc_ref, mt_ref, bst_ref, cst_ref, lp_ref, ytl_ref,
                      ytc_out[0] if ytc_out else None, nb, ctx_chunks, pows)


def _s5core_group(q, utl_ref, utc_ref, mt_ref, bst_ref, cst_ref, lp_ref, ytl_ref, ytc_ref,
                  nb, ctx_chunks, pows):
    p = S5_STATE
    nv = utl_ref.shape[1]
    ul = jnp.concatenate([utl_ref[q, v] for v in range(nv)], axis=1)
    uc = utc_ref[q]
    lg = LANES // nb
    bst, cst, mt = bst_ref[q], cst_ref[q], mt_ref[q]
    lp = lp_ref[q]

    def lam_of(d):
        def lam(n):
            k = pows.index(n)
            return lp[2 * d][:, k:k + 1], lp[2 * d + 1][:, k:k + 1]
        return lam

    lane = lax.broadcasted_iota(jnp.int32, (1, LANES), 1)
    sc = jnp.dot(bst, uc, preferred_element_type=_F32)
    posc = lane & (CTX_SEG - 1)
    hc = []
    for d in range(2):
        r0 = 2 * p * d
        hr, hi = _lane_scan(sc[r0:r0 + p], sc[r0 + p:r0 + 2 * p], lam_of(d), 1, posc, CTX_SEG, d == 1)
        hc += [hr, hi]
    sl = jnp.dot(bst, ul, preferred_element_type=_F32)
    y0 = jnp.dot(mt, ul, preferred_element_type=_F32)
    for v in range(nv):
        ytl_ref[q, v] = y0[:, v * LANES:(v + 1) * LANES]
    posl = lane & (lg - 1)
    bidl = lane >> int(math.log2(lg))
    hl = []
    for d in range(2):
        lam = lam_of(d)
        r0 = 2 * p * d
        sr = [sl[r0:r0 + p, v * LANES:(v + 1) * LANES] for v in range(nv)]
        si = [sl[r0 + p:r0 + 2 * p, v * LANES:(v + 1) * LANES] for v in range(nv)]
        order = list(range(nv)) if d == 0 else list(range(nv - 1, -1, -1))
        ir, ii = {}, {}
        prev = None
        for v in order:
            if prev is None:
                ir[v], ii[v] = sr[v], si[v]
            else:
                mr, mi = _cmul(lam(1), ir[prev], ii[prev])
                ir[v], ii[v] = mr + sr[v], mi + si[v]
            prev = v
        entry = 0 if d == 0 else lg - 1
        src = (CTX_OFF + ctx_chunks) if d == 0 else (CTX_OFF - 1)
        h0r = jnp.zeros((p, LANES), _F32)
        h0i = jnp.zeros((p, LANES), _F32)
        for b in range(nb):
            m = (bidl == b) & (posl == entry)
            col = b * CTX_SEG + src
            h0r = jnp.where(m, hc[2 * d][:, col:col + 1], h0r)
            h0i = jnp.where(m, hc[2 * d + 1][:, col:col + 1], h0i)
        jr, ji = _cmul(lam(nv), h0r, h0i)
        er, ei = _lane_scan(ir[prev] + jr, ii[prev] + ji, lam, nv, posl, lg, d == 1)
        er, ei = er + h0r, ei + h0i
        hr, hi = {order[0]: er}, {order[0]: ei}
        for n, v in enumerate(order[1:], start=1):
            mr, mi = _cmul(lam(n), er, ei)
            hr[v], hi[v] = ir[order[n - 1]] + mr, ii[order[n - 1]] + mi
        hl += [jnp.concatenate([hr[v] for v in range(nv)], axis=1),
               jnp.concatenate([hi[v] for v in range(nv)], axis=1)]
    h = jnp.concatenate(hl, axis=0).astype(_BF16)
    y1 = jnp.dot(cst, h, preferred_element_type=_F32)
    for v in range(nv):
        ytl_ref[q, v] += y1[:, v * LANES:(v + 1) * LANES]
    if ytc_ref is not None:
        hcb = jnp.concatenate(hc, axis=0).astype(_BF16)
        yc = jnp.dot(mt, uc, preferred_element_type=_F32)
        ytc_ref[q] = yc + jnp.dot(cst, hcb, preferred_element_type=_F32)


def _s5_pows(nv, lg):
    pows = set(range(1, nv + 1))
    pows |= {nv << k for k in range(int(math.log2(lg)))}
    pows |= {1 << k for k in range(int(math.log2(CTX_SEG)))}
    return tuple(sorted(pows))


def _s5core(utl, utc, mt, bst, cst, lampow, nb, ctx_chunks, pows, want_yc):
    g, nv, rows, _ = utl.shape
    gq = S5_CORE_GROUPS
    out_shape = [jax.ShapeDtypeStruct((g, nv, rows, LANES), _F32)]
    out_specs = [pl.BlockSpec((gq, nv, rows, LANES), lambda i: (i, 0, 0, 0))]
    if want_yc:
        out_shape.append(jax.ShapeDtypeStruct((g, rows, LANES), _F32))
        out_specs.append(pl.BlockSpec((gq, rows, LANES), lambda i: (i, 0, 0)))
    res = pl.pallas_call(
        functools.partial(_s5core_kernel, nb=nb, ctx_chunks=ctx_chunks, pows=pows),
        out_shape=out_shape,
        grid=(g // gq,),
        in_specs=[pl.BlockSpec((gq, nv, rows, LANES), lambda i: (i, 0, 0, 0)),
                  pl.BlockSpec((gq, rows, LANES), lambda i: (i, 0, 0)),
                  pl.BlockSpec((gq, rows, rows), lambda i: (i, 0, 0)),
                  pl.BlockSpec((gq, S5_NSTATE, rows), lambda i: (i, 0, 0)),
                  pl.BlockSpec((gq, rows, S5_NSTATE), lambda i: (i, 0, 0)),
                  pl.BlockSpec((gq, 4, S5_STATE, LANES), lambda i: (i, 0, 0, 0))],
        out_specs=out_specs,
        compiler_params=_cparams(("arbitrary",)),
        name="s5core_yc" if want_yc else "s5core",
    )(utl, utc, mt, bst, cst, lampow)
    return (res[0], res[1]) if want_yc else (res[0], None)


def _s5prep_kernel(*refs):
    for q in range(refs[0].shape[1]):
        _s5prep_group(q, *refs)


def _s5prep_group(q, acol_ref, arow_ref, ldt_ref, bre_ref, bim_ref, cre_ref, cim_ref, c1_ref, c2_ref,
                  esel_ref, etile_ref, e2_ref, e3_ref, pcol_ref, mt_ref, bst_ref, cst_ref, lp_ref):
    hp = lax.Precision.HIGHEST
    dot = functools.partial(jnp.dot, precision=hp, preferred_element_type=_F32)
    t_len, p = S5_CHUNK, S5_STATE
    lane = lax.broadcasted_iota(jnp.int32, (1, LANES), 1)
    n_lane = lane.astype(_F32)
    npow = pcol_ref.shape[0]
    strips = []
    for d in range(2):
        dt = jnp.exp(ldt_ref[d, q])
        a = acol_ref[d, q]
        ar, ai = a[:, 0:1], a[:, 1:2]
        dar, dai = ar * dt, ai * dt
        pm, pa = jnp.exp(n_lane * dar), n_lane * dai
        lr, li = pm * jnp.cos(pa), pm * jnp.sin(pa)
        lbr, lbi = lr[:, 1:2], li[:, 1:2]
        den = ar * ar + ai * ai
        nr, ni = lbr - 1.0, lbi
        kr = (nr * ar + ni * ai) / den
        ki = (ni * ar - nr * ai) / den
        bbr = kr * bre_ref[d, q] - ki * bim_ref[d, q]
        bbi = kr * bim_ref[d, q] + ki * bre_ref[d, q]
        wr = _select_cols(lr, esel_ref[d])
        wi = _select_cols(li, esel_ref[d])
        btr = _select_cols(bbr, etile_ref[...])
        bti = _select_cols(bbi, etile_ref[...])
        bsr, bsi = wr * btr - wi * bti, wr * bti + wi * btr
        bst_ref[q, (2 * d) * p:(2 * d + 1) * p, :] = bsr.astype(_BF16)
        bst_ref[q, (2 * d + 1) * p:(2 * d + 2) * p, :] = bsi.astype(_BF16)
        strips.append(dot(cre_ref[d, q], bsr) - dot(cim_ref[d, q], bsi))
        w1 = _select_rows(e2_ref[d], jnp.concatenate([lr, li], axis=0).T)
        w2 = _select_rows(e2_ref[d], jnp.concatenate([li, lr], axis=0).T)
        c1 = _select_rows(e3_ref[...], c1_ref[d, q])
        c2 = _select_rows(e3_ref[...], c2_ref[d, q])
        cst_ref[q, :, d * LANES:(d + 1) * LANES] = (c1 * w1 + c2 * w2).astype(_BF16)
        arow = arow_ref[d, q]
        m = pcol_ref[...] * float(t_len)
        qm, qa = jnp.exp(m * (arow[0:1] * dt)), m * (arow[1:2] * dt)
        packed = qm * jnp.where(lane < p, jnp.cos(qa), jnp.sin(qa))
        lpt = jnp.concatenate([packed, jnp.zeros((LANES - npow, LANES), _F32)], axis=0).T
        lp_ref[q, 2 * d] = lpt[:p]
        lp_ref[q, 2 * d + 1] = lpt[p:]
    rf, rb = strips
    lane_w = lax.broadcasted_iota(jnp.int32, (1, S5_ROWS), 1)
    for t in range(t_len):
        sf = (t_len - 1 - t) * S5_GROUP
        f = rf if sf == 0 else pltpu.roll(rf, S5_ROWS - sf, axis=1)
        f = jnp.where(lane_w < (t + 1) * S5_GROUP, f, 0.0)
        sb = t * S5_GROUP
        bk = rb if sb == 0 else pltpu.roll(rb, sb, axis=1)
        bk = jnp.where(lane_w >= sb, bk, 0.0)
        mt_ref[q, t * S5_GROUP:(t + 1) * S5_GROUP, :] = (f + bk).astype(_BF16)


def _s5prep_constants(pows):
    t_len, h = S5_CHUNK, S5_GROUP
    s_of = np.arange(S5_ROWS) // h
    h_of = np.arange(S5_ROWS) % h
    n128 = np.arange(LANES)
    esel = np.stack([n128[:, None] == (t_len - 1 - s_of)[None, :], n128[:, None] == s_of[None, :]])
    etile = np.arange(h)[:, None] == h_of[None, :]
    e2 = np.stack([(s_of + 1)[:, None] == n128[None, :], (t_len - s_of)[:, None] == n128[None, :]])
    e3 = h_of[:, None] == np.arange(h)[None, :]
    pcol = np.zeros((16, 1), np.float32)
    pcol[:len(pows), 0] = pows
    f = lambda m: jnp.asarray(m, dtype=_BF16)
    return f(esel), f(etile), f(e2), f(e3), jnp.asarray(pcol)


def _s5prep(a_re, a_im, log_dt, b_re, b_im, c_re, c_im, pows):
    g, p, h = S5_GROUPS, S5_STATE, S5_GROUP
    acol = jnp.stack([a_re, a_im], axis=-1)
    arow = jnp.stack([jnp.concatenate([a_re, a_re], -1),
                      jnp.concatenate([a_im, a_im], -1)], axis=2)
    ldt = log_dt.reshape(2, g, 1, 1)
    c1 = jnp.concatenate([c_re, -c_re], -1)
    c2 = jnp.concatenate([-c_im, -c_im], -1)
    esel, etile, e2, e3, pcol = _s5prep_constants(pows)

    gq = S5_PREP_GROUPS

    def per_g(*tail):
        return pl.BlockSpec((2, gq) + tail, lambda i: (0, i) + (0,) * len(tail))

    return pl.pallas_call(
        _s5prep_kernel,
        out_shape=[jax.ShapeDtypeStruct((g, S5_ROWS, S5_ROWS), _BF16),
                   jax.ShapeDtypeStruct((g, S5_NSTATE, S5_ROWS), _BF16),
                   jax.ShapeDtypeStruct((g, S5_ROWS, S5_NSTATE), _BF16),
                   jax.ShapeDtypeStruct((g, 4, p, LANES), _F32)],
        grid=(g // gq,),
        in_specs=[per_g(p, 2), per_g(2, LANES), per_g(1, 1), per_g(p, h), per_g(p, h),
                  per_g(h, p), per_g(h, p), per_g(h, LANES), per_g(h, LANES),
                  _resident(esel.shape), _resident(etile.shape), _resident(e2.shape), _resident(e3.shape),
                  _resident(pcol.shape)],
        out_specs=[pl.BlockSpec((gq, S5_ROWS, S5_ROWS), lambda i: (i, 0, 0)),
                   pl.BlockSpec((gq, S5_NSTATE, S5_ROWS), lambda i: (i, 0, 0)),
                   pl.BlockSpec((gq, S5_ROWS, S5_NSTATE), lambda i: (i, 0, 0)),
                   pl.BlockSpec((gq, 4, p, LANES), lambda i: (i, 0, 0, 0))],
        compiler_params=_cparams(("arbitrary",)),
        name="s5prep",
    )(acol, arow, ldt, b_re, b_im, c_re, c_im, c1, c2, esel, etile, e2, e3, pcol)


def _ctx_to_ut(xn):
    b, n, _ = xn.shape
    c = n // S5_CHUNK
    v = xn.reshape(b, c, S5_CHUNK, S5_GROUPS, S5_GROUP).transpose(3, 2, 4, 0, 1)
    v = jnp.pad(v, ((0, 0),) * 4 + ((CTX_OFF, CTX_SEG - c - CTX_OFF),))
    v = v.reshape(S5_GROUPS, S5_ROWS, b * CTX_SEG)
    return jnp.pad(v, ((0, 0), (0, 0), (0, LANES - b * CTX_SEG)))


def _ctx_from_yt(yt, b, n):
    c = n // S5_CHUNK
    v = yt[:, :, :b * CTX_SEG].reshape(S5_GROUPS, S5_CHUNK, S5_GROUP, b, CTX_SEG)
    v = v[..., CTX_OFF:CTX_OFF + c]
    return v.transpose(3, 4, 1, 0, 2).reshape(b, n, D_MODEL)


def _s5_glu(xf, xn, y, dsk, gw, gb, g1):
    z = _gelu_tanh(y + dsk * xn)
    gate = jax.nn.sigmoid(jnp.dot(z.astype(_BF16), gw, preferred_element_type=_F32) + gb)
    return xf + g1 * (z * gate)


def _s5post_ctx_kernel(x_ref, y_ref, mod_ref, g_ref, dsk_ref, gw_ref, gb_ref, o_ref):
    sh1, sc1, g1 = _mod_slices(mod_ref[0])[:3]
    xf = x_ref[0]
    xn = _rms_mod(xf, g_ref[...], sh1, sc1)
    o_ref[0] = _s5_glu(xf, xn, y_ref[0], dsk_ref[...], gw_ref[0], gb_ref[...], g1)


def _s5post_ctx(x, y, mod, mod_row, g, dsk, gw, layer, gb):
    b, n, d = x.shape
    tok = pl.BlockSpec((1, n, d), lambda bi: (bi, 0, 0))
    vec = pl.BlockSpec((1, d), lambda bi: (0, 0))
    return pl.pallas_call(
        _s5post_ctx_kernel,
        out_shape=jax.ShapeDtypeStruct((b, n, d), _F32),
        grid=(b,),
        in_specs=[tok, tok, pl.BlockSpec((1, 1, 6 * d), lambda bi: (mod_row, 0, 0)),
                  vec, vec, _resident_layer((d, d), layer), vec],
        out_specs=tok,
        compiler_params=_cparams(("arbitrary",)),
        name="s5post_ctx",
    )(x, y, mod, g, dsk, gw, gb)


def _s5post_kernel(x_ref, yt_ref, mod_ref, g_ref, dsk_ref, gw_ref, gb_ref, o_ref, *, nb, k):
    d = D_MODEL
    lg = LANES // nb
    gn = g_ref[...]
    dsk = dsk_ref[...]
    mods = [_mod_slices(mod_ref[b])[:3] for b in range(nb)]
    ys = [yt_ref[:, 0, j * S5_GROUP:(j + 1) * S5_GROUP, :].reshape(d, LANES).T for j in range(k)]
    rows = lg * k
    xs, zs = [], []
    for b in range(nb):
        yb = jnp.stack([ys[j][b * lg:(b + 1) * lg] for j in range(k)], axis=0)
        yb = jnp.swapaxes(yb, 0, 1).reshape(rows, d)
        xf = x_ref[b].reshape(rows, d)
        xn = _rms_mod(xf, gn, mods[b][0], mods[b][1])
        xs.append(xf)
        zs.append(_gelu_tanh(yb + dsk * xn))
    z = jnp.concatenate(zs, axis=0)
    gate = jax.nn.sigmoid(jnp.dot(z.astype(_BF16), gw_ref[0], preferred_element_type=_F32) + gb_ref[...])
    o = z * gate
    for b in range(nb):
        o_ref[b] = (xs[b] + mods[b][2] * o[b * rows:(b + 1) * rows]).reshape(lg, k, d)


def _s5post(x, yt, mod, g, dsk, gw, layer, gb):
    b, n, d = x.shape
    lg = LANES // b
    per = n // lg
    k = S5_POST_TOK
    blocks_per_v = S5_CHUNK // k
    tok = pl.BlockSpec((b, lg, k, d), lambda r: (0, 0, r, 0))
    vec = pl.BlockSpec((1, d), lambda r: (0, 0))
    out = pl.pallas_call(
        functools.partial(_s5post_kernel, nb=b, k=k),
        out_shape=jax.ShapeDtypeStruct((b, lg, per, d), _F32),
        grid=(per // k,),
        in_specs=[tok,
                  pl.BlockSpec((S5_GROUPS, 1, k * S5_GROUP, LANES),
                               lambda r: (0, r // blocks_per_v, r % blocks_per_v, 0)),
                  pl.BlockSpec((8, 1, 6 * d), lambda r: (0, 0, 0)),
                  vec, vec, _resident_layer((d, d), layer), vec],
        out_specs=tok,
        compiler_params=_cparams(("arbitrary",)),
        name="s5post",
    )(x.reshape(b, lg, per, d), yt, mod, g, dsk, gw, gb)
    return out.reshape(b, n, d)


def _mlp_kernel(x_ref, mod_ref, g_ref, w1_ref, b1_ref, w2_ref, b2_ref, fg_ref, o_ref, *, final):
    sh2, sc2, g2 = _mod_slices(mod_ref[0])[3:]
    xf = x_ref[0]
    xn = _rms_mod(xf, g_ref[...], sh2, sc2).astype(_BF16)
    acc = jnp.zeros(xf.shape, _F32)
    fc = D_MODEL
    for j in range(D_FF // fc):
        a = jnp.dot(xn, w1_ref[0, :, j * fc:(j + 1) * fc], preferred_element_type=_F32)
        a = jnp.maximum(a + b1_ref[:, j * fc:(j + 1) * fc], 0.0)
        acc = acc + jnp.dot((a * a).astype(_BF16), w2_ref[0, j * fc:(j + 1) * fc, :],
                            preferred_element_type=_F32)
    out = xf + g2 * (acc + b2_ref[...])
    if final:
        ms = jnp.mean(out * out, axis=-1, keepdims=True)
        out = out * lax.rsqrt(ms + EPS) * fg_ref[...]
    o_ref[0] = out


def _mlp(x, mod, mod_row, g, w1, b1, w2, b2, layer, fg, tt, final):
    b, n, d = x.shape
    row = (lambda bi: bi) if mod_row is None else (lambda bi: mod_row)
    tok = pl.BlockSpec((1, tt, d), lambda bi, t: (bi, t, 0))
    vec = pl.BlockSpec((1, d), lambda bi, t: (0, 0))
    return pl.pallas_call(
        functools.partial(_mlp_kernel, final=final),
        out_shape=jax.ShapeDtypeStruct((b, n, d), _F32),
        grid=(b, n // tt),
        in_specs=[tok, pl.BlockSpec((1, 1, 6 * d), lambda bi, t: (row(bi), 0, 0)), vec,
                  _resident_layer((d, D_FF), layer), pl.BlockSpec((1, D_FF), lambda bi, t: (0, 0)),
                  _resident_layer((D_FF, d), layer), vec, vec],
        out_specs=tok,
        compiler_params=_cparams(("arbitrary", "arbitrary")),
        name="mlp_final" if final else "mlp",
    )(x, mod, g, w1, b1, w2, b2, fg)


def _col_window_matrices(tile_rows):
    n = tile_rows * GRID_W
    t = np.arange(n)
    r, c = t // GRID_W, t % GRID_W
    mats = []
    for w in POOL_WINDOWS:
        lo, hi = np.maximum(c - w // 2, 0), np.minimum(c + w - w // 2, GRID_W)
        m = (r[:, None] == r[None, :]) & (c[None, :] >= lo[:, None]) & (c[None, :] < hi[:, None])
        mats.append(m)
    return jnp.asarray(np.stack(mats), dtype=_BF16)


def _grid_inverse_counts(rows):
    t = np.arange(rows * GRID_W)
    r, c = t // GRID_W, t % GRID_W
    tabs = []
    for w in POOL_WINDOWS:
        rc = np.minimum(r + w - w // 2, rows) - np.maximum(r - w // 2, 0)
        cc = np.minimum(c + w - w // 2, GRID_W) - np.maximum(c - w // 2, 0)
        tabs.append(1.0 / (rc * cc))
    return jnp.asarray(np.stack(tabs), dtype=_F32)


def _seq_window_matrices(n):
    t = np.arange(n)
    mats = []
    for w in POOL_WINDOWS:
        lo, hi = np.maximum(t - w // 2, 0), np.minimum(t + w - w // 2, n)
        mats.append((t[None, :] >= lo[:, None]) & (t[None, :] < hi[:, None]))
    return jnp.asarray(np.stack(mats), dtype=_BF16)


POOL_BLOCK_ROWS = 16
POOL_HALO_ROWS = 8
POOL_TILE_ROWS = 4


def _pool_kernel(xm_ref, xt_ref, xb_ref, mod_ref, g_ref, acol_ref, pw_ref, ps_ref, inv_ref, o_ref, cbuf):
    i = pl.program_id(1)
    nblk = pl.num_programs(1)
    sh1, sc1, g1 = _mod_slices(mod_ref[0])[:3]
    gn = g_ref[...]
    halo = POOL_HALO_ROWS * GRID_W
    main = POOL_BLOCK_ROWS * GRID_W
    tile = POOL_TILE_ROWS * GRID_W
    xm = xm_ref[0]
    xn_m = _rms_mod(xm, gn, sh1, sc1)
    xn_t = jnp.where(i > 0, _rms_mod(xt_ref[0], gn, sh1, sc1), 0.0)
    xn_b = jnp.where(i < nblk - 1, _rms_mod(xb_ref[0], gn, sh1, sc1), 0.0)
    xn_ext = jnp.concatenate([xn_t, xn_m, xn_b], axis=0).astype(_BF16)
    ntile = (main + 2 * halo) // tile
    for gi, w in enumerate(POOL_WINDOWS):
        ch = slice(gi * POOL_CH, (gi + 1) * POOL_CH)
        a = acol_ref[gi]
        for k in range(ntile):
            cbuf[k * tile:(k + 1) * tile, :] = jnp.dot(
                a, xn_ext[k * tile:(k + 1) * tile, ch], preferred_element_type=_F32)
        tot = jnp.zeros((main, POOL_CH), _F32)
        for k in range(-(w // 2), w - w // 2):
            off = halo + k * GRID_W
            tot = tot + cbuf[off:off + main, :]
        inv = inv_ref[gi]
        p = tot * jnp.concatenate([inv] * (POOL_CH // LANES), axis=1) - xn_m[:, ch]
        y = jnp.dot(p.astype(_BF16), pw_ref[0, gi], preferred_element_type=_F32) * ps_ref[:, ch]
        o_ref[0, :, ch] = xm[:, ch] + g1[:, ch] * y


def _pool_grid(x, mod, g, pw, layer, ps):
    b, n, d = x.shape
    rows = n // GRID_W
    main = POOL_BLOCK_ROWS * GRID_W
    halo = POOL_HALO_ROWS * GRID_W
    nblk = n // main
    ratio = main // halo
    nh = n // halo
    acol = _col_window_matrices(POOL_TILE_ROWS)
    tile = POOL_TILE_ROWS * GRID_W
    inv = jnp.broadcast_to(_grid_inverse_counts(rows)[:, :, None], (len(POOL_WINDOWS), n, LANES))
    vec = pl.BlockSpec((1, d), lambda bi, i: (0, 0))
    return pl.pallas_call(
        _pool_kernel,
        out_shape=jax.ShapeDtypeStruct((b, n, d), _F32),
        grid=(b, nblk),
        in_specs=[pl.BlockSpec((1, main, d), lambda bi, i: (bi, i, 0)),
                  pl.BlockSpec((1, halo, d), lambda bi, i: (bi, jnp.maximum(i * ratio - 1, 0), 0)),
                  pl.BlockSpec((1, halo, d), lambda bi, i: (bi, jnp.minimum((i + 1) * ratio, nh - 1), 0)),
                  pl.BlockSpec((1, 1, 6 * d), lambda bi, i: (bi, 0, 0)),
                  vec,
                  _resident((len(POOL_WINDOWS), tile, tile)),
                  _resident_layer((len(POOL_WINDOWS), POOL_CH, POOL_CH), layer),
                  vec,
                  pl.BlockSpec((len(POOL_WINDOWS), main, LANES), lambda bi, i: (0, i, 0))],
        out_specs=pl.BlockSpec((1, main, d), lambda bi, i: (bi, i, 0)),
        scratch_shapes=[pltpu.VMEM((main + 2 * halo, POOL_CH), _F32)],
        compiler_params=_cparams(("arbitrary", "arbitrary")),
        name="pool_grid",
    )(x, x, x, mod, g, acol, pw, ps, inv)


def _poolseq_kernel(x_ref, mod_ref, g_ref, aseq_ref, pw_ref, ps_ref, o_ref):
    sh1, sc1, g1 = _mod_slices(mod_ref[0])[:3]
    xf = x_ref[0]
    n = xf.shape[0]
    xn = _rms_mod(xf, g_ref[...], sh1, sc1)
    xb = xn.astype(_BF16)
    t = lax.broadcasted_iota(jnp.int32, (n, POOL_CH), 0)
    for gi, w in enumerate(POOL_WINDOWS):
        ch = slice(gi * POOL_CH, (gi + 1) * POOL_CH)
        tot = jnp.dot(aseq_ref[gi], xb[:, ch], preferred_element_type=_F32)
        cnt = jnp.minimum(t + (w - w // 2), n) - jnp.maximum(t - w // 2, 0)
        p = tot / cnt.astype(_F32) - xn[:, ch]
        y = jnp.dot(p.astype(_BF16), pw_ref[0, gi], preferred_element_type=_F32) * ps_ref[:, ch]
        o_ref[0, :, ch] = xf[:, ch] + g1[:, ch] * y


def _pool_seq(x, mod, mod_row, g, pw, layer, ps):
    b, n, d = x.shape
    aseq = _seq_window_matrices(n)
    vec = pl.BlockSpec((1, d), lambda bi: (0, 0))
    return pl.pallas_call(
        _poolseq_kernel,
        out_shape=jax.ShapeDtypeStruct((b, n, d), _F32),
        grid=(b,),
        in_specs=[pl.BlockSpec((1, n, d), lambda bi: (bi, 0, 0)),
                  pl.BlockSpec((1, 1, 6 * d), lambda bi: (mod_row, 0, 0)),
                  vec,
                  _resident((len(POOL_WINDOWS), n, n)),
                  _resident_layer((len(POOL_WINDOWS), POOL_CH, POOL_CH), layer),
                  vec],
        out_specs=pl.BlockSpec((1, n, d), lambda bi: (bi, 0, 0)),
        compiler_params=_cparams(("arbitrary",)),
        name="pool_seq",
    )(x, mod, g, aseq, pw, ps)


def kernel(x, c, ctx, c_ctx, ada_w, ada_b, norm1_g, norm2_g, s5_a_re, s5_a_im, s5_log_dt, s5_b_re, s5_b_im, s5_c_re, s5_c_im, s5_d, s5_glu_w, s5_glu_b, pool_w, pool_scale, mlp_w1, mlp_b1, mlp_w2, mlp_b2, final_g):
    bsz, n_tok, d = x.shape
    n_ctx = ctx.shape[1]
    depth = ada_w.shape[0]
    assert d == D_MODEL and n_tok % (POOL_BLOCK_ROWS * GRID_W) == 0
    assert LANES % bsz == 0 and bsz * CTX_SEG <= LANES and bsz < 8
    lat_chunks = n_tok // S5_CHUNK
    ctx_chunks = n_ctx // S5_CHUNK
    assert n_tok % S5_CHUNK == 0 and n_ctx % S5_CHUNK == 0 and (bsz * lat_chunks) % LANES == 0
    assert ctx_chunks + 2 * CTX_OFF <= CTX_SEG
    nv = bsz * lat_chunks // LANES
    pows = _s5_pows(nv, LANES // bsz)
    last_ctx_reader = ((depth - 1) // N_MIXERS) * N_MIXERS
    ctx_row = bsz
    tt = min(512, n_tok)
    tt_ctx = min(512, n_ctx)

    cc = jnp.zeros((8, d), _F32).at[:bsz].set(c).at[ctx_row].set(c_ctx)
    mods = _modulation(cc, ada_w, ada_b)
    w1 = mlp_w1.astype(_BF16)
    w2 = mlp_w2.astype(_BF16)
    glu_w = s5_glu_w.astype(_BF16)
    pw = pool_w.astype(_BF16)

    h_ctx = ctx
    for i in range(depth):
        ctx_in = i <= last_ctx_reader
        ctx_out = i < last_ctx_reader
        j = i // N_MIXERS
        mod = mods[i].reshape(8, 1, 6 * d)
        g1n = norm1_g[i].reshape(1, d)
        g2n = norm2_g[i].reshape(1, d)
        if i % N_MIXERS == 0:
            mt, bst, cst, lampow = _s5prep(s5_a_re[j], s5_a_im[j], s5_log_dt[j], s5_b_re[j], s5_b_im[j],
                                           s5_c_re[j], s5_c_im[j], pows)
            dsk = s5_d[j].reshape(1, d)
            gb = s5_glu_b[j].reshape(1, d)
            if ctx_in:
                utc = _ctx_to_ut(_prenorm(h_ctx, mod, ctx_row, g1n, tt_ctx))
            else:
                utc = jnp.zeros((S5_GROUPS, S5_ROWS, LANES), _BF16)
            utl = _s5pre(x, mod, g1n)
            ytl, ytc = _s5core(utl, utc, mt, bst, cst, lampow, bsz, ctx_chunks, pows, ctx_out)
            x = _s5post(x, ytl, mod, g1n, dsk, glu_w, j, gb)
            if ctx_out:
                h_ctx = _s5post_ctx(h_ctx, _ctx_from_yt(ytc, bsz, n_ctx), mod, ctx_row, g1n, dsk,
                                    glu_w, j, gb)
        else:
            ps = pool_scale[j].reshape(1, d)
            x = _pool_grid(x, mod, g1n, pw, j, ps)
            if ctx_out:
                h_ctx = _pool_seq(h_ctx, mod, ctx_row, g1n, pw, j, ps)
        final = i == depth - 1
        x = _mlp(x, mod, None, g2n, w1, mlp_b1[i].reshape(1, D_FF), w2, mlp_b2[i].reshape(1, d), i,
                 final_g.reshape(1, d), tt, final)
        if ctx_out:
            h_ctx = _mlp(h_ctx, mod, ctx_row, g2n, w1, mlp_b1[i].reshape(1, D_FF), w2,
                         mlp_b2[i].reshape(1, d), i, final_g.reshape(1, d), tt_ctx, False)
    return x
```

```python
import functools
import math

import numpy as np
import jax
import jax.numpy as jnp
from jax import lax
from jax.experimental import pallas as pl
from jax.experimental.pallas import tpu as pltpu

D_MODEL = 1024
GRID_W = 64
S5_GROUP = 16
S5_GROUPS = D_MODEL // S5_GROUP
S5_STATE = 64
POOL_WINDOWS = (2, 4, 8, 16)
POOL_CH = D_MODEL // len(POOL_WINDOWS)
D_FF = 4 * D_MODEL
N_MIXERS = 2
EPS = 1e-6
LANES = 128

S5_CHUNK = 32
S5_ROWS = S5_CHUNK * S5_GROUP
S5_NSTATE = 4 * S5_STATE
CTX_SEG = 16
CTX_OFF = 2
S5_PRE_TOK = 8
S5_POST_TOK = 8
S5_CORE_GROUPS = 2
S5_PREP_GROUPS = 4
V7X_VMEM_LIMIT_BYTES = 56 * 1024 * 1024

_F32 = jnp.float32
_BF16 = jnp.bfloat16


def _cparams(sem):
    return pltpu.CompilerParams(dimension_semantics=sem, vmem_limit_bytes=V7X_VMEM_LIMIT_BYTES)


def _resident(shape):
    nd = len(shape)
    return pl.BlockSpec(shape, lambda *_: (0,) * nd, pipeline_mode=pl.Buffered(1))


def _resident_layer(shape, layer):
    nd = len(shape)
    return pl.BlockSpec((1,) + tuple(shape), lambda *_: (layer,) + (0,) * nd, pipeline_mode=pl.Buffered(1))


def _mod_kernel(cc_ref, w_ref, b_ref, o_ref):
    c = cc_ref[...]
    a = c * jax.nn.sigmoid(c)
    o_ref[0] = jnp.dot(a, w_ref[0], preferred_element_type=_F32,
                       precision=lax.Precision.HIGHEST) + b_ref[0]


def _modulation(cc, ada_w, ada_b):
    depth, d, n = ada_w.shape
    tn = 1536
    return pl.pallas_call(
        _mod_kernel,
        out_shape=jax.ShapeDtypeStruct((depth, 8, n), _F32),
        grid=(depth, n // tn),
        in_specs=[pl.BlockSpec((8, d), lambda i, j: (0, 0)),
                  pl.BlockSpec((1, d, tn), lambda i, j: (i, 0, j)),
                  pl.BlockSpec((1, 1, tn), lambda i, j: (i, 0, j))],
        out_specs=pl.BlockSpec((1, 8, tn), lambda i, j: (i, 0, j)),
        compiler_params=_cparams(("arbitrary", "arbitrary")),
        name="modulation",
    )(cc, ada_w, ada_b.reshape(depth, 1, n))


def _rms_mod(xf, g, shift, scale):
    ms = jnp.mean(xf * xf, axis=-1, keepdims=True)
    return (xf * lax.rsqrt(ms + EPS)) * (g * (1.0 + scale)) + shift


def _mod_slices(m):
    d = D_MODEL
    return tuple(m[:, k * d:(k + 1) * d] for k in range(6))


def _gelu_tanh(x):
    c = math.sqrt(2.0 / math.pi)
    hx = 0.5 * x
    return hx + hx * jnp.tanh(x * (c + (c * 0.044715) * (x * x)))


def _sigmoid(x):
    return 0.5 * jnp.tanh(0.5 * x) + 0.5


def _cmul(a, yr, yi):
    ar, ai = a
    return ar * yr - ai * yi, ar * yi + ai * yr


def _bf16_parts(x):
    hi = x.astype(_BF16)
    r1 = x - hi.astype(_F32)
    mid = r1.astype(_BF16)
    lo = (r1 - mid.astype(_F32)).astype(_BF16)
    return hi, mid, lo


def _select_cols(x, sel):
    return sum(jnp.dot(part, sel, preferred_element_type=_F32) for part in _bf16_parts(x))


def _select_rows(sel, x):
    return sum(jnp.dot(sel, part, preferred_element_type=_F32) for part in _bf16_parts(x))


def _prenorm_kernel(x_ref, mod_ref, g_ref, o_ref):
    sh1, sc1 = _mod_slices(mod_ref[0])[:2]
    o_ref[0] = _rms_mod(x_ref[0], g_ref[...], sh1, sc1).astype(o_ref.dtype)


def _prenorm(x, mod, mod_row, g, tt):
    b, n, d = x.shape
    return pl.pallas_call(
        _prenorm_kernel,
        out_shape=jax.ShapeDtypeStruct((b, n, d), _BF16),
        grid=(b, n // tt),
        in_specs=[pl.BlockSpec((1, tt, d), lambda bi, t: (bi, t, 0)),
                  pl.BlockSpec((1, 1, 6 * d), lambda bi, t: (mod_row, 0, 0)),
                  pl.BlockSpec((1, d), lambda bi, t: (0, 0))],
        out_specs=pl.BlockSpec((1, tt, d), lambda bi, t: (bi, t, 0)),
        compiler_params=_cparams(("arbitrary", "arbitrary")),
        name="prenorm",
    )(x, mod, g)


def _s5pre_kernel(x_ref, mod_ref, g_ref, ut_ref, *, nb, k):
    gn = g_ref[...]
    mods = [_mod_slices(mod_ref[b])[:2] for b in range(nb)]
    xs = [jnp.swapaxes(x_ref[b], 0, 1) for b in range(nb)]
    for j in range(k):
        xn = jnp.concatenate([_rms_mod(xs[b][j], gn, *mods[b]) for b in range(nb)], axis=0)
        ut_ref[:, 0, j * S5_GROUP:(j + 1) * S5_GROUP, :] = (
            xn.T.astype(_BF16).reshape(S5_GROUPS, S5_GROUP, LANES))


def _s5pre(x, mod, g):
    b, n, d = x.shape
    lg = LANES // b
    per = n // lg
    k = S5_PRE_TOK
    blocks_per_v = S5_CHUNK // k
    nv = b * (n // S5_CHUNK) // LANES
    return pl.pallas_call(
        functools.partial(_s5pre_kernel, nb=b, k=k),
        out_shape=jax.ShapeDtypeStruct((S5_GROUPS, nv, S5_ROWS, LANES), _BF16),
        grid=(per // k,),
        in_specs=[pl.BlockSpec((b, lg, k, d), lambda r: (0, 0, r, 0)),
                  pl.BlockSpec((8, 1, 6 * d), lambda r: (0, 0, 0)),
                  pl.BlockSpec((1, d), lambda r: (0, 0))],
        out_specs=pl.BlockSpec((S5_GROUPS, 1, k * S5_GROUP, LANES),
                               lambda r: (0, r // blocks_per_v, r % blocks_per_v, 0)),
        compiler_params=_cparams(("arbitrary",)),
        name="s5pre",
    )(x.reshape(b, lg, per, d), mod, g)


def _lane_shift(v, dist, pos, seg, reverse):
    n = v.shape[1]
    if reverse:
        return jnp.where(pos < seg - dist, pltpu.roll(v, n - dist, axis=1), 0.0)
    return jnp.where(pos >= dist, pltpu.roll(v, dist, axis=1), 0.0)


def _lane_scan(xr, xi, lam, unit, pos, seg, reverse):
    k = 0
    while (1 << k) < seg:
        dist = 1 << k
        mr, mi = _cmul(lam(unit * dist), _lane_shift(xr, dist, pos, seg, reverse),
                       _lane_shift(xi, dist, pos, seg, reverse))
        xr, xi = xr + mr, xi + mi
        k += 1
    return _lane_shift(xr, 1, pos, seg, reverse), _lane_shift(xi, 1, pos, seg, reverse)


def _s5core_kernel(utl_ref, utc_ref, mt_ref, bst_ref, cst_ref, lp_ref, ytl_ref, *ytc_out,
                   nb, ctx_chunks, pows):
    for q in range(utl_ref.shape[0]):
        _s5core_group(q, utl_ref, utc_ref, mt_ref, bst_ref, cst_ref, lp_ref, ytl_ref,
                      ytc_out[0] if ytc_out else None, nb, ctx_chunks, pows)


def _s5core_group(q, utl_ref, utc_ref, mt_ref, bst_ref, cst_ref, lp_ref, ytl_ref, ytc_ref,
                  nb, ctx_chunks, pows):
    p = S5_STATE
    nv = utl_ref.shape[1]
    ul = jnp.concatenate([utl_ref[q, v] for v in range(nv)], axis=1)
    uc = utc_ref[q]
    lg = LANES // nb
    bst, cst, mt = bst_ref[q], cst_ref[q], mt_ref[q]
    lp = lp_ref[q]

    def lam_of(d):
        def lam(n):
            k = pows.index(n)
            return lp[2 * d][:, k:k + 1], lp[2 * d + 1][:, k:k + 1]
        return lam

    lane = lax.broadcasted_iota(jnp.int32, (1, LANES), 1)
    sc = jnp.dot(bst, uc, preferred_element_type=_F32)
    posc = lane & (CTX_SEG - 1)
    hc = []
    for d in range(2):
        r0 = 2 * p * d
        hr, hi = _lane_scan(sc[r0:r0 + p], sc[r0 + p:r0 + 2 * p], lam_of(d), 1, posc, CTX_SEG, d == 1)
        hc += [hr, hi]
    sl = jnp.dot(bst, ul, preferred_element_type=_F32)
    y0 = jnp.dot(mt, ul, preferred_element_type=_F32)
    for v in range(nv):
        ytl_ref[q, v] = y0[:, v * LANES:(v + 1) * LANES]
    posl = lane & (lg - 1)
    bidl = lane >> int(math.log2(lg))
    hl = []
    for d in range(2):
        lam = lam_of(d)
        r0 = 2 * p * d
        sr = [sl[r0:r0 + p, v * LANES:(v + 1) * LANES] for v in range(nv)]
        si = [sl[r0 + p:r0 + 2 * p, v * LANES:(v + 1) * LANES] for v in range(nv)]
        order = list(range(nv)) if d == 0 else list(range(nv - 1, -1, -1))
        ir, ii = {}, {}
        prev = None
        for v in order:
            if prev is None:
                ir[v], ii[v] = sr[v], si[v]
            else:
                mr, mi = _cmul(lam(1), ir[prev], ii[prev])
                ir[v], ii[v] = mr + sr[v], mi + si[v]
            prev = v
        entry = 0 if d == 0 else lg - 1
        src = (CTX_OFF + ctx_chunks) if d == 0 else (CTX_OFF - 1)
        h0r = jnp.zeros((p, LANES), _F32)
        h0i = jnp.zeros((p, LANES), _F32)
        for b in range(nb):
            m = (bidl == b) & (posl == entry)
            col = b * CTX_SEG + src
            h0r = jnp.where(m, hc[2 * d][:, col:col + 1], h0r)
            h0i = jnp.where(m, hc[2 * d + 1][:, col:col + 1], h0i)
        jr, ji = _cmul(lam(nv), h0r, h0i)
        er, ei = _lane_scan(ir[prev] + jr, ii[prev] + ji, lam, nv, posl, lg, d == 1)
        er, ei = er + h0r, ei + h0i
        hr, hi = {order[0]: er}, {order[0]: ei}
        for n, v in enumerate(order[1:], start=1):
            mr, mi = _cmul(lam(n), er, ei)
            hr[v], hi[v] = ir[order[n - 1]] + mr, ii[order[n - 1]] + mi
        hl += [jnp.concatenate([hr[v] for v in range(nv)], axis=1),
               jnp.concatenate([hi[v] for v in range(nv)], axis=1)]
    h = jnp.concatenate(hl, axis=0).astype(_BF16)
    y1 = jnp.dot(cst, h, preferred_element_type=_F32)
    for v in range(nv):
        ytl_ref[q, v] += y1[:, v * LANES:(v + 1) * LANES]
    if ytc_ref is not None:
        hcb = jnp.concatenate(hc, axis=0).astype(_BF16)
        yc = jnp.dot(mt, uc, preferred_element_type=_F32)
        ytc_ref[q] = yc + jnp.dot(cst, hcb, preferred_element_type=_F32)


def _s5_pows(nv, lg):
    pows = set(range(1, nv + 1))
    pows |= {nv << k for k in range(int(math.log2(lg)))}
    pows |= {1 << k for k in range(int(math.log2(CTX_SEG)))}
    return tuple(sorted(pows))


def _s5core(utl, utc, mt, bst, cst, lampow, nb, ctx_chunks, pows, want_yc):
    g, nv, rows, _ = utl.shape
    gq = S5_CORE_GROUPS
    out_shape = [jax.ShapeDtypeStruct((g, nv, rows, LANES), _F32)]
    out_specs = [pl.BlockSpec((gq, nv, rows, LANES), lambda i: (i, 0, 0, 0))]
    if want_yc:
        out_shape.append(jax.ShapeDtypeStruct((g, rows, LANES), _F32))
        out_specs.append(pl.BlockSpec((gq, rows, LANES), lambda i: (i, 0, 0)))
    res = pl.pallas_call(
        functools.partial(_s5core_kernel, nb=nb, ctx_chunks=ctx_chunks, pows=pows),
        out_shape=out_shape,
        grid=(g // gq,),
        in_specs=[pl.BlockSpec((gq, nv, rows, LANES), lambda i: (i, 0, 0, 0)),
                  pl.BlockSpec((gq, rows, LANES), lambda i: (i, 0, 0)),
                  pl.BlockSpec((gq, rows, rows), lambda i: (i, 0, 0)),
                  pl.BlockSpec((gq, S5_NSTATE, rows), lambda i: (i, 0, 0)),
                  pl.BlockSpec((gq, rows, S5_NSTATE), lambda i: (i, 0, 0)),
                  pl.BlockSpec((gq, 4, S5_STATE, LANES), lambda i: (i, 0, 0, 0))],
        out_specs=out_specs,
        compiler_params=_cparams(("arbitrary",)),
        name="s5core_yc" if want_yc else "s5core",
    )(utl, utc, mt, bst, cst, lampow)
    return (res[0], res[1]) if want_yc else (res[0], None)


def _s5prep_kernel(*refs):
    for q in range(refs[0].shape[1]):
        _s5prep_group(q, *refs)


def _s5prep_group(q, acol_ref, arow_ref, ldt_ref, bre_ref, bim_ref, cre_ref, cim_ref, c1_ref, c2_ref,
                  esel_ref, etile_ref, e2_ref, e3_ref, pcol_ref, mt_ref, bst_ref, cst_ref, lp_ref):
    hp = lax.Precision.HIGHEST
    dot = functools.partial(jnp.dot, precision=hp, preferred_element_type=_F32)
    t_len, p = S5_CHUNK, S5_STATE
    lane = lax.broadcasted_iota(jnp.int32, (1, LANES), 1)
    n_lane = lane.astype(_F32)
    npow = pcol_ref.shape[0]
    strips = []
    for d in range(2):
        dt = jnp.exp(ldt_ref[d, q])
        a = acol_ref[d, q]
        ar, ai = a[:, 0:1], a[:, 1:2]
        dar, dai = ar * dt, ai * dt
        pm, pa = jnp.exp(n_lane * dar), n_lane * dai
        lr, li = pm * jnp.cos(pa), pm * jnp.sin(pa)
        lbr, lbi = lr[:, 1:2], li[:, 1:2]
        den = ar * ar + ai * ai
        nr, ni = lbr - 1.0, lbi
        kr = (nr * ar + ni * ai) / den
        ki = (ni * ar - nr * ai) / den
        bbr = kr * bre_ref[d, q] - ki * bim_ref[d, q]
        bbi = kr * bim_ref[d, q] + ki * bre_ref[d, q]
        wr = _select_cols(lr, esel_ref[d])
        wi = _select_cols(li, esel_ref[d])
        btr = _select_cols(bbr, etile_ref[...])
        bti = _select_cols(bbi, etile_ref[...])
        bsr, bsi = wr * btr - wi * bti, wr * bti + wi * btr
        bst_ref[q, (2 * d) * p:(2 * d + 1) * p, :] = bsr.astype(_BF16)
        bst_ref[q, (2 * d + 1) * p:(2 * d + 2) * p, :] = bsi.astype(_BF16)
        strips.append(dot(cre_ref[d, q], bsr) - dot(cim_ref[d, q], bsi))
        w1 = _select_rows(e2_ref[d], jnp.concatenate([lr, li], axis=0).T)
        w2 = _select_rows(e2_ref[d], jnp.concatenate([li, lr], axis=0).T)
        c1 = _select_rows(e3_ref[...], c1_ref[d, q])
        c2 = _select_rows(e3_ref[...], c2_ref[d, q])
        cst_ref[q, :, d * LANES:(d + 1) * LANES] = (c1 * w1 + c2 * w2).astype(_BF16)
        arow = arow_ref[d, q]
        m = pcol_ref[...] * float(t_len)
        qm, qa = jnp.exp(m * (arow[0:1] * dt)), m * (arow[1:2] * dt)
        packed = qm * jnp.where(lane < p, jnp.cos(qa), jnp.sin(qa))
        lpt = jnp.concatenate([packed, jnp.zeros((LANES - npow, LANES), _F32)], axis=0).T
        lp_ref[q, 2 * d] = lpt[:p]
        lp_ref[q, 2 * d + 1] = lpt[p:]
    rf, rb = strips
    lane_w = lax.broadcasted_iota(jnp.int32, (1, S5_ROWS), 1)
    for t in range(t_len):
        sf = (t_len - 1 - t) * S5_GROUP
        f = rf if sf == 0 else pltpu.roll(rf, S5_ROWS - sf, axis=1)
        f = jnp.where(lane_w < (t + 1) * S5_GROUP, f, 0.0)
        sb = t * S5_GROUP
        bk = rb if sb == 0 else pltpu.roll(rb, sb, axis=1)
        bk = jnp.where(lane_w >= sb, bk, 0.0)
        mt_ref[q, t * S5_GROUP:(t + 1) * S5_GROUP, :] = (f + bk).astype(_BF16)


def _s5prep_constants(pows):
    t_len, h = S5_CHUNK, S5_GROUP
    s_of = np.arange(S5_ROWS) // h
    h_of = np.arange(S5_ROWS) % h
    n128 = np.arange(LANES)
    esel = np.stack([n128[:, None] == (t_len - 1 - s_of)[None, :], n128[:, None] == s_of[None, :]])
    etile = np.arange(h)[:, None] == h_of[None, :]
    e2 = np.stack([(s_of + 1)[:, None] == n128[None, :], (t_len - s_of)[:, None] == n128[None, :]])
    e3 = h_of[:, None] == np.arange(h)[None, :]
    pcol = np.zeros((16, 1), np.float32)
    pcol[:len(pows), 0] = pows
    f = lambda m: jnp.asarray(m, dtype=_BF16)
    return f(esel), f(etile), f(e2), f(e3), jnp.asarray(pcol)


def _s5prep(a_re, a_im, log_dt, b_re, b_im, c_re, c_im, pows):
    g, p, h = S5_GROUPS, S5_STATE, S5_GROUP
    acol = jnp.stack([a_re, a_im], axis=-1)
    arow = jnp.stack([jnp.concatenate([a_re, a_re], -1),
                      jnp.concatenate([a_im, a_im], -1)], axis=2)
    ldt = log_dt.reshape(2, g, 1, 1)
    c1 = jnp.concatenate([c_re, -c_re], -1)
    c2 = jnp.concatenate([-c_im, -c_im], -1)
    esel, etile, e2, e3, pcol = _s5prep_constants(pows)

    gq = S5_PREP_GROUPS

    def per_g(*tail):
        return pl.BlockSpec((2, gq) + tail, lambda i: (0, i) + (0,) * len(tail))

    return pl.pallas_call(
        _s5prep_kernel,
        out_shape=[jax.ShapeDtypeStruct((g, S5_ROWS, S5_ROWS), _BF16),
                   jax.ShapeDtypeStruct((g, S5_NSTATE, S5_ROWS), _BF16),
                   jax.ShapeDtypeStruct((g, S5_ROWS, S5_NSTATE), _BF16),
                   jax.ShapeDtypeStruct((g, 4, p, LANES), _F32)],
        grid=(g // gq,),
        in_specs=[per_g(p, 2), per_g(2, LANES), per_g(1, 1), per_g(p, h), per_g(p, h),
                  per_g(h, p), per_g(h, p), per_g(h, LANES), per_g(h, LANES),
                  _resident(esel.shape), _resident(etile.shape), _resident(e2.shape), _resident(e3.shape),
                  _resident(pcol.shape)],
        out_specs=[pl.BlockSpec((gq, S5_ROWS, S5_ROWS), lambda i: (i, 0, 0)),
                   pl.BlockSpec((gq, S5_NSTATE, S5_ROWS), lambda i: (i, 0, 0)),
                   pl.BlockSpec((gq, S5_ROWS, S5_NSTATE), lambda i: (i, 0, 0)),
                   pl.BlockSpec((gq, 4, p, LANES), lambda i: (i, 0, 0, 0))],
        compiler_params=_cparams(("arbitrary",)),
        name="s5prep",
    )(acol, arow, ldt, b_re, b_im, c_re, c_im, c1, c2, esel, etile, e2, e3, pcol)


def _ctx_to_ut(xn):
    b, n, _ = xn.shape
    c = n // S5_CHUNK
    v = xn.reshape(b, c, S5_CHUNK, S5_GROUPS, S5_GROUP).transpose(3, 2, 4, 0, 1)
    v = jnp.pad(v, ((0, 0),) * 4 + ((CTX_OFF, CTX_SEG - c - CTX_OFF),))
    v = v.reshape(S5_GROUPS, S5_ROWS, b * CTX_SEG)
    return jnp.pad(v, ((0, 0), (0, 0), (0, LANES - b * CTX_SEG)))


def _ctx_from_yt(yt, b, n):
    c = n // S5_CHUNK
    v = yt[:, :, :b * CTX_SEG].reshape(S5_GROUPS, S5_CHUNK, S5_GROUP, b, CTX_SEG)
    v = v[..., CTX_OFF:CTX_OFF + c]
    return v.transpose(3, 4, 1, 0, 2).reshape(b, n, D_MODEL)


def _s5_glu(xf, xn, y, dsk, gw, gb, g1):
    z = _gelu_tanh(y + dsk * xn)
    gate = _sigmoid(jnp.dot(z.astype(_BF16), gw, preferred_element_type=_F32) + gb)
    return xf + g1 * (z * gate)


def _s5post_ctx_kernel(x_ref, y_ref, mod_ref, g_ref, dsk_ref, gw_ref, gb_ref, o_ref):
    sh1, sc1, g1 = _mod_slices(mod_ref[0])[:3]
    xf = x_ref[0]
    xn = _rms_mod(xf, g_ref[...], sh1, sc1)
    o_ref[0] = _s5_glu(xf, xn, y_ref[0], dsk_ref[...], gw_ref[0], gb_ref[...], g1)


def _s5post_ctx(x, y, mod, mod_row, g, dsk, gw, layer, gb):
    b, n, d = x.shape
    tok = pl.BlockSpec((1, n, d), lambda bi: (bi, 0, 0))
    vec = pl.BlockSpec((1, d), lambda bi: (0, 0))
    return pl.pallas_call(
        _s5post_ctx_kernel,
        out_shape=jax.ShapeDtypeStruct((b, n, d), _F32),
        grid=(b,),
        in_specs=[tok, tok, pl.BlockSpec((1, 1, 6 * d), lambda bi: (mod_row, 0, 0)),
                  vec, vec, _resident_layer((d, d), layer), vec],
        out_specs=tok,
        compiler_params=_cparams(("arbitrary",)),
        name="s5post_ctx",
    )(x, y, mod, g, dsk, gw, gb)


def _s5post_kernel(x_ref, yt_ref, mod_ref, g_ref, dsk_ref, gw_ref, gb_ref, o_ref, *, nb, k):
    d = D_MODEL
    lg = LANES // nb
    gn = g_ref[...]
    dsk = dsk_ref[...]
    c = math.sqrt(2.0 / math.pi)
    mods = [_mod_slices(mod_ref[b])[:3] for b in range(nb)]
    ys = [yt_ref[:, 0, j * S5_GROUP:(j + 1) * S5_GROUP, :].reshape(d, LANES).T for j in range(k)]
    rows = lg * k
    xs, zs = [], []
    for b in range(nb):
        sh1, sc1, _ = mods[b]
        yb = jnp.stack([ys[j][b * lg:(b + 1) * lg] for j in range(k)], axis=0)
        yb = jnp.swapaxes(yb, 0, 1).reshape(rows, d)
        xf = x_ref[b].reshape(rows, d)
        ms = jnp.mean(xf * xf, axis=-1, keepdims=True)
        pre = (xf * lax.rsqrt(ms + EPS)) * ((gn * (1.0 + sc1)) * dsk) + (yb + sh1 * dsk)
        xs.append(xf)
        zs.append(pre + pre * jnp.tanh(pre * (c + (c * 0.044715) * (pre * pre))))
    z2 = jnp.concatenate(zs, axis=0)
    th = jnp.tanh(jnp.dot(z2.astype(_BF16), gw_ref[0], preferred_element_type=_F32) + gb_ref[...])
    for b in range(nb):
        q = (0.25 * mods[b][2]) * z2[b * rows:(b + 1) * rows]
        o_ref[b] = (xs[b] + (q * th[b * rows:(b + 1) * rows] + q)).reshape(lg, k, d)


def _s5post(x, yt, mod, g, dsk, gw, layer, gb):
    b, n, d = x.shape
    lg = LANES // b
    per = n // lg
    k = S5_POST_TOK
    blocks_per_v = S5_CHUNK // k
    tok = pl.BlockSpec((b, lg, k, d), lambda r: (0, 0, r, 0))
    vec = pl.BlockSpec((1, d), lambda r: (0, 0))
    out = pl.pallas_call(
        functools.partial(_s5post_kernel, nb=b, k=k),
        out_shape=jax.ShapeDtypeStruct((b, lg, per, d), _F32),
        grid=(per // k,),
        in_specs=[tok,
                  pl.BlockSpec((S5_GROUPS, 1, k * S5_GROUP, LANES),
                               lambda r: (0, r // blocks_per_v, r % blocks_per_v, 0)),
                  pl.BlockSpec((8, 1, 6 * d), lambda r: (0, 0, 0)),
                  vec, vec, _resident_layer((d, d), layer), vec],
        out_specs=tok,
        compiler_params=_cparams(("arbitrary",)),
        name="s5post",
    )(x.reshape(b, lg, per, d), yt, mod, g, dsk, gw, gb)
    return out.reshape(b, n, d)


def _mlp_kernel(x_ref, mod_ref, g_ref, w1_ref, b1_ref, w2_ref, b2_ref, fg_ref, o_ref, *, final):
    sh2, sc2, g2 = _mod_slices(mod_ref[0])[3:]
    xf = x_ref[0]
    xn = _rms_mod(xf, g_ref[...], sh2, sc2).astype(_BF16)
    acc = jnp.zeros(xf.shape, _F32)
    fc = D_MODEL
    for j in range(D_FF // fc):
        a = jnp.dot(xn, w1_ref[0, :, j * fc:(j + 1) * fc], preferred_element_type=_F32)
        a = jnp.maximum(a + b1_ref[:, j * fc:(j + 1) * fc], 0.0)
        acc = acc + jnp.dot((a * a).astype(_BF16), w2_ref[0, j * fc:(j + 1) * fc, :],
                            preferred_element_type=_F32)
    out = xf + g2 * (acc + b2_ref[...])
    if final:
        ms = jnp.mean(out * out, axis=-1, keepdims=True)
        out = out * lax.rsqrt(ms + EPS) * fg_ref[...]
    o_ref[0] = out


def _mlp(x, mod, mod_row, g, w1, b1, w2, b2, layer, fg, tt, final):
    b, n, d = x.shape
    row = (lambda bi: bi) if mod_row is None else (lambda bi: mod_row)
    tok = pl.BlockSpec((1, tt, d), lambda bi, t: (bi, t, 0))
    vec = pl.BlockSpec((1, d), lambda bi, t: (0, 0))
    return pl.pallas_call(
        functools.partial(_mlp_kernel, final=final),
        out_shape=jax.ShapeDtypeStruct((b, n, d), _F32),
        grid=(b, n // tt),
        in_specs=[tok, pl.BlockSpec((1, 1, 6 * d), lambda bi, t: (row(bi), 0, 0)), vec,
                  _resident_layer((d, D_FF), layer), pl.BlockSpec((1, D_FF), lambda bi, t: (0, 0)),
                  _resident_layer((D_FF, d), layer), vec, vec],
        out_specs=tok,
        compiler_params=_cparams(("arbitrary", "arbitrary")),
        name="mlp_final" if final else "mlp",
    )(x, mod, g, w1, b1, w2, b2, fg)


def _window_in_rows(w):
    need = POOL_TILE_ROWS + w - 1
    return -(-need // POOL_TILE_ROWS) * POOL_TILE_ROWS


def _grid_window_matrices():
    to = np.arange(POOL_TILE_ROWS * GRID_W)
    ro, co = to // GRID_W, to % GRID_W
    mats = []
    for w in POOL_WINDOWS:
        ti = np.arange(_window_in_rows(w) * GRID_W)
        ri, ci = ti // GRID_W - w // 2, ti % GRID_W
        m = ((ri[None, :] >= (ro - w // 2)[:, None]) & (ri[None, :] < (ro + w - w // 2)[:, None])
             & (ci[None, :] >= (co - w // 2)[:, None]) & (ci[None, :] < (co + w - w // 2)[:, None]))
        mats.append(jnp.asarray(m, dtype=_BF16))
    return mats


def _grid_inverse_counts(rows):
    t = np.arange(rows * GRID_W)
    r, c = t // GRID_W, t % GRID_W
    tabs = []
    for w in POOL_WINDOWS:
        rc = np.minimum(r + w - w // 2, rows) - np.maximum(r - w // 2, 0)
        cc = np.minimum(c + w - w // 2, GRID_W) - np.maximum(c - w // 2, 0)
        tabs.append(1.0 / (rc * cc))
    return jnp.asarray(np.stack(tabs), dtype=_F32)


def _seq_window_matrices(n):
    t = np.arange(n)
    mats = []
    for w in POOL_WINDOWS:
        lo, hi = np.maximum(t - w // 2, 0), np.minimum(t + w - w // 2, n)
        mats.append((t[None, :] >= lo[:, None]) & (t[None, :] < hi[:, None]))
    return jnp.asarray(np.stack(mats), dtype=_BF16)


POOL_BLOCK_ROWS = 16
POOL_HALO_ROWS = 8
POOL_TILE_ROWS = 4


def _pool_kernel(xm_ref, xt_ref, xb_ref, mod_ref, g_ref, a2_ref, a4_ref, a8_ref, a16_ref, pw_ref, ps_ref,
                 inv_ref, o_ref):
    i = pl.program_id(1)
    nblk = pl.num_programs(1)
    sh1, sc1, g1 = _mod_slices(mod_ref[0])[:3]
    gn = g_ref[...]
    halo = POOL_HALO_ROWS * GRID_W
    main = POOL_BLOCK_ROWS * GRID_W
    tile = POOL_TILE_ROWS * GRID_W
    xm = xm_ref[0]
    xn_m = _rms_mod(xm, gn, sh1, sc1)
    xn_t = jnp.where(i > 0, _rms_mod(xt_ref[0], gn, sh1, sc1), 0.0)
    xn_b = jnp.where(i < nblk - 1, _rms_mod(xb_ref[0], gn, sh1, sc1), 0.0)
    xn_ext = jnp.concatenate([xn_t, xn_m, xn_b], axis=0).astype(_BF16)
    for gi, (w, a_ref) in enumerate(zip(POOL_WINDOWS, (a2_ref, a4_ref, a8_ref, a16_ref))):
        ch = slice(gi * POOL_CH, (gi + 1) * POOL_CH)
        a = a_ref[...]
        span = a.shape[1]
        tots = []
        for k in range(main // tile):
            start = halo + k * tile - (w // 2) * GRID_W
            tots.append(jnp.dot(a, xn_ext[start:start + span, ch], preferred_element_type=_F32))
        tot = jnp.concatenate(tots, axis=0)
        inv = inv_ref[gi]
        p = tot * jnp.concatenate([inv] * (POOL_CH // LANES), axis=1) - xn_m[:, ch]
        y = jnp.dot(p.astype(_BF16), pw_ref[0, gi], preferred_element_type=_F32) * ps_ref[:, ch]
        o_ref[0, :, ch] = xm[:, ch] + g1[:, ch] * y


def _pool_grid(x, mod, g, pw, layer, ps):
    b, n, d = x.shape
    rows = n // GRID_W
    main = POOL_BLOCK_ROWS * GRID_W
    halo = POOL_HALO_ROWS * GRID_W
    nblk = n // main
    ratio = main // halo
    nh = n // halo
    amats = _grid_window_matrices()
    assert all(w // 2 <= POOL_HALO_ROWS and _window_in_rows(w) - w // 2 <= POOL_TILE_ROWS + POOL_HALO_ROWS
               for w in POOL_WINDOWS)
    inv = jnp.broadcast_to(_grid_inverse_counts(rows)[:, :, None], (len(POOL_WINDOWS), n, LANES))
    vec = pl.BlockSpec((1, d), lambda bi, i: (0, 0))
    return pl.pallas_call(
        _pool_kernel,
        out_shape=jax.ShapeDtypeStruct((b, n, d), _F32),
        grid=(b, nblk),
        in_specs=[pl.BlockSpec((1, main, d), lambda bi, i: (bi, i, 0)),
                  pl.BlockSpec((1, halo, d), lambda bi, i: (bi, jnp.maximum(i * ratio - 1, 0), 0)),
                  pl.BlockSpec((1, halo, d), lambda bi, i: (bi, jnp.minimum((i + 1) * ratio, nh - 1), 0)),
                  pl.BlockSpec((1, 1, 6 * d), lambda bi, i: (bi, 0, 0)),
                  vec,
                  *[_resident(a.shape) for a in amats],
                  _resident_layer((len(POOL_WINDOWS), POOL_CH, POOL_CH), layer),
                  vec,
                  pl.BlockSpec((len(POOL_WINDOWS), main, LANES), lambda bi, i: (0, i, 0))],
        out_specs=pl.BlockSpec((1, main, d), lambda bi, i: (bi, i, 0)),
        compiler_params=_cparams(("arbitrary", "arbitrary")),
        name="pool_grid",
    )(x, x, x, mod, g, *amats, pw, ps, inv)


def _poolseq_kernel(x_ref, mod_ref, g_ref, aseq_ref, pw_ref, ps_ref, o_ref):
    sh1, sc1, g1 = _mod_slices(mod_ref[0])[:3]
    xf = x_ref[0]
    n = xf.shape[0]
    xn = _rms_mod(xf, g_ref[...], sh1, sc1)
    xb = xn.astype(_BF16)
    t = lax.broadcasted_iota(jnp.int32, (n, POOL_CH), 0)
    for gi, w in enumerate(POOL_WINDOWS):
        ch = slice(gi * POOL_CH, (gi + 1) * POOL_CH)
        tot = jnp.dot(aseq_ref[gi], xb[:, ch], preferred_element_type=_F32)
        cnt = jnp.minimum(t + (w - w // 2), n) - jnp.maximum(t - w // 2, 0)
        p = tot / cnt.astype(_F32) - xn[:, ch]
        y = jnp.dot(p.astype(_BF16), pw_ref[0, gi], preferred_element_type=_F32) * ps_ref[:, ch]
        o_ref[0, :, ch] = xf[:, ch] + g1[:, ch] * y


def _pool_seq(x, mod, mod_row, g, pw, layer, ps):
    b, n, d = x.shape
    aseq = _seq_window_matrices(n)
    vec = pl.BlockSpec((1, d), lambda bi: (0, 0))
    return pl.pallas_call(
        _poolseq_kernel,
        out_shape=jax.ShapeDtypeStruct((b, n, d), _F32),
        grid=(b,),
        in_specs=[pl.BlockSpec((1, n, d), lambda bi: (bi, 0, 0)),
                  pl.BlockSpec((1, 1, 6 * d), lambda bi: (mod_row, 0, 0)),
                  vec,
                  _resident((len(POOL_WINDOWS), n, n)),
                  _resident_layer((len(POOL_WINDOWS), POOL_CH, POOL_CH), layer),
                  vec],
        out_specs=pl.BlockSpec((1, n, d), lambda bi: (bi, 0, 0)),
        compiler_params=_cparams(("arbitrary",)),
        name="pool_seq",
    )(x, mod, g, aseq, pw, ps)


def kernel(x, c, ctx, c_ctx, ada_w, ada_b, norm1_g, norm2_g, s5_a_re, s5_a_im, s5_log_dt, s5_b_re, s5_b_im, s5_c_re, s5_c_im, s5_d, s5_glu_w, s5_glu_b, pool_w, pool_scale, mlp_w1, mlp_b1, mlp_w2, mlp_b2, final_g):
    bsz, n_tok, d = x.shape
    n_ctx = ctx.shape[1]
    depth = ada_w.shape[0]
    assert d == D_MODEL and n_tok % (POOL_BLOCK_ROWS * GRID_W) == 0
    assert LANES % bsz == 0 and bsz * CTX_SEG <= LANES and bsz < 8
    lat_chunks = n_tok // S5_CHUNK
    ctx_chunks = n_ctx // S5_CHUNK
    assert n_tok % S5_CHUNK == 0 and n_ctx % S5_CHUNK == 0 and (bsz * lat_chunks) % LANES == 0
    assert ctx_chunks + 2 * CTX_OFF <= CTX_SEG
    nv = bsz * lat_chunks // LANES
    pows = _s5_pows(nv, LANES // bsz)
    last_ctx_reader = ((depth - 1) // N_MIXERS) * N_MIXERS
    ctx_row = bsz
    tt = min(512, n_tok)
    tt_ctx = min(512, n_ctx)

    cc = jnp.zeros((8, d), _F32).at[:bsz].set(c).at[ctx_row].set(c_ctx)
    mods = _modulation(cc, ada_w, ada_b)
    w1 = mlp_w1.astype(_BF16)
    w2 = mlp_w2.astype(_BF16)
    glu_w = s5_glu_w.astype(_BF16)
    glu_w4 = (0.25 * s5_glu_w).astype(_BF16)
    pw = pool_w.astype(_BF16)

    h_ctx = ctx
    for i in range(depth):
        ctx_in = i <= last_ctx_reader
        ctx_out = i < last_ctx_reader
        j = i // N_MIXERS
        mod = mods[i].reshape(8, 1, 6 * d)
        g1n = norm1_g[i].reshape(1, d)
        g2n = norm2_g[i].reshape(1, d)
        if i % N_MIXERS == 0:
            mt, bst, cst, lampow = _s5prep(s5_a_re[j], s5_a_im[j], s5_log_dt[j], s5_b_re[j], s5_b_im[j],
                                           s5_c_re[j], s5_c_im[j], pows)
            dsk = s5_d[j].reshape(1, d)
            gb = s5_glu_b[j].reshape(1, d)
            if ctx_in:
                utc = _ctx_to_ut(_prenorm(h_ctx, mod, ctx_row, g1n, tt_ctx))
            else:
                utc = jnp.zeros((S5_GROUPS, S5_ROWS, LANES), _BF16)
            utl = _s5pre(x, mod, g1n)
            ytl, ytc = _s5core(utl, utc, mt, bst, cst, lampow, bsz, ctx_chunks, pows, ctx_out)
            x = _s5post(x, ytl, mod, g1n, dsk, glu_w4, j, 0.5 * gb)
            if ctx_out:
                h_ctx = _s5post_ctx(h_ctx, _ctx_from_yt(ytc, bsz, n_ctx), mod, ctx_row, g1n, dsk,
                                    glu_w, j, gb)
        else:
            ps = pool_scale[j].reshape(1, d)
            x = _pool_grid(x, mod, g1n, pw, j, ps)
            if ctx_out:
                h_ctx = _pool_seq(h_ctx, mod, ctx_row, g1n, pw, j, ps)
        final = i == depth - 1
        x = _mlp(x, mod, None, g2n, w1, mlp_b1[i].reshape(1, D_FF), w2, mlp_b2[i].reshape(1, d), i,
                 final_g.reshape(1, d), tt, final)
        if ctx_out:
            h_ctx = _mlp(h_ctx, mod, ctx_row, g2n, w1, mlp_b1[i].reshape(1, D_FF), w2,
                         mlp_b2[i].reshape(1, d), i, final_g.reshape(1, d), tt_ctx, False)
    return x
```

```python
import functools
import math

import numpy as np
import jax
import jax.numpy as jnp
from jax import lax
from jax.experimental import pallas as pl
from jax.experimental.pallas import tpu as pltpu

D_MODEL = 1024
GRID_W = 64
S5_GROUP = 16
S5_GROUPS = D_MODEL // S5_GROUP
S5_STATE = 64
POOL_WINDOWS = (2, 4, 8, 16)
POOL_CH = D_MODEL // len(POOL_WINDOWS)
D_FF = 4 * D_MODEL
N_MIXERS = 2
EPS = 1e-6
LANES = 128

S5_CHUNK = 32
S5_ROWS = S5_CHUNK * S5_GROUP
S5_NSTATE = 4 * S5_STATE
CTX_SEG = 16
CTX_OFF = 2
S5_PRE_TOK = 8
S5_POST_TOK = 8
S5_CORE_GROUPS = 2
S5_PREP_GROUPS = 4
V7X_VMEM_LIMIT_BYTES = 56 * 1024 * 1024

_F32 = jnp.float32
_BF16 = jnp.bfloat16


def _cparams(sem):
    return pltpu.CompilerParams(dimension_semantics=sem, vmem_limit_bytes=V7X_VMEM_LIMIT_BYTES)


def _resident(shape):
    nd = len(shape)
    return pl.BlockSpec(shape, lambda *_: (0,) * nd, pipeline_mode=pl.Buffered(1))


def _resident_layer(shape, layer):
    nd = len(shape)
    return pl.BlockSpec((1,) + tuple(shape), lambda *_: (layer,) + (0,) * nd, pipeline_mode=pl.Buffered(1))


def _mod_kernel(cc_ref, w_ref, b_ref, o_ref):
    c = cc_ref[...]
    a = c * jax.nn.sigmoid(c)
    o_ref[0] = jnp.dot(a, w_ref[0], preferred_element_type=_F32,
                       precision=lax.Precision.HIGHEST) + b_ref[0]


def _modulation(cc, ada_w, ada_b):
    depth, d, n = ada_w.shape
    tn = 1536
    return pl.pallas_call(
        _mod_kernel,
        out_shape=jax.ShapeDtypeStruct((depth, 8, n), _F32),
        grid=(depth, n // tn),
        in_specs=[pl.BlockSpec((8, d), lambda i, j: (0, 0)),
                  pl.BlockSpec((1, d, tn), lambda i, j: (i, 0, j)),
                  pl.BlockSpec((1, 1, tn), lambda i, j: (i, 0, j))],
        out_specs=pl.BlockSpec((1, 8, tn), lambda i, j: (i, 0, j)),
        compiler_params=_cparams(("arbitrary", "arbitrary")),
        name="modulation",
    )(cc, ada_w, ada_b.reshape(depth, 1, n))


def _rms_mod(xf, g, shift, scale):
    ms = jnp.mean(xf * xf, axis=-1, keepdims=True)
    return (xf * lax.rsqrt(ms + EPS)) * (g * (1.0 + scale)) + shift


def _mod_slices(m):
    d = D_MODEL
    return tuple(m[:, k * d:(k + 1) * d] for k in range(6))


def _gelu_tanh(x):
    c = math.sqrt(2.0 / math.pi)
    hx = 0.5 * x
    return hx + hx * jnp.tanh(x * (c + (c * 0.044715) * (x * x)))


def _sigmoid(x):
    return 0.5 * jnp.tanh(0.5 * x) + 0.5


def _cmul(a, yr, yi):
    ar, ai = a
    return ar * yr - ai * yi, ar * yi + ai * yr


def _bf16_parts(x):
    hi = x.astype(_BF16)
    r1 = x - hi.astype(_F32)
    mid = r1.astype(_BF16)
    lo = (r1 - mid.astype(_F32)).astype(_BF16)
    return hi, mid, lo


def _select_cols(x, sel):
    return sum(jnp.dot(part, sel, preferred_element_type=_F32) for part in _bf16_parts(x))


def _select_rows(sel, x):
    return sum(jnp.dot(sel, part, preferred_element_type=_F32) for part in _bf16_parts(x))


def _prenorm_kernel(x_ref, mod_ref, g_ref, o_ref):
    sh1, sc1 = _mod_slices(mod_ref[0])[:2]
    o_ref[0] = _rms_mod(x_ref[0], g_ref[...], sh1, sc1).astype(o_ref.dtype)


def _prenorm(x, mod, mod_row, g, tt):
    b, n, d = x.shape
    return pl.pallas_call(
        _prenorm_kernel,
        out_shape=jax.ShapeDtypeStruct((b, n, d), _BF16),
        grid=(b, n // tt),
        in_specs=[pl.BlockSpec((1, tt, d), lambda bi, t: (bi, t, 0)),
                  pl.BlockSpec((1, 1, 6 * d), lambda bi, t: (mod_row, 0, 0)),
                  pl.BlockSpec((1, d), lambda bi, t: (0, 0))],
        out_specs=pl.BlockSpec((1, tt, d), lambda bi, t: (bi, t, 0)),
        compiler_params=_cparams(("arbitrary", "arbitrary")),
        name="prenorm",
    )(x, mod, g)


def _s5pre_kernel(x_ref, mod_ref, g_ref, ut_ref, *, nb, k):
    gn = g_ref[...]
    mods = [_mod_slices(mod_ref[b])[:2] for b in range(nb)]
    xs = [jnp.swapaxes(x_ref[b], 0, 1) for b in range(nb)]
    for j in range(k):
        xn = jnp.concatenate([_rms_mod(xs[b][j], gn, *mods[b]) for b in range(nb)], axis=0)
        ut_ref[:, 0, j * S5_GROUP:(j + 1) * S5_GROUP, :] = (
            xn.T.astype(_BF16).reshape(S5_GROUPS, S5_GROUP, LANES))


def _s5pre(x, mod, g):
    b, n, d = x.shape
    lg = LANES // b
    per = n // lg
    k = S5_PRE_TOK
    blocks_per_v = S5_CHUNK // k
    nv = b * (n // S5_CHUNK) // LANES
    return pl.pallas_call(
        functools.partial(_s5pre_kernel, nb=b, k=k),
        out_shape=jax.ShapeDtypeStruct((S5_GROUPS, nv, S5_ROWS, LANES), _BF16),
        grid=(per // k,),
        in_specs=[pl.BlockSpec((b, lg, k, d), lambda r: (0, 0, r, 0)),
                  pl.BlockSpec((8, 1, 6 * d), lambda r: (0, 0, 0)),
                  pl.BlockSpec((1, d), lambda r: (0, 0))],
        out_specs=pl.BlockSpec((S5_GROUPS, 1, k * S5_GROUP, LANES),
                               lambda r: (0, r // blocks_per_v, r % blocks_per_v, 0)),
        compiler_params=_cparams(("arbitrary",)),
        name="s5pre",
    )(x.reshape(b, lg, per, d), mod, g)


def _lane_shift(v, dist, pos, seg, reverse):
    n = v.shape[1]
    if reverse:
        return jnp.where(pos < seg - dist, pltpu.roll(v, n - dist, axis=1), 0.0)
    return jnp.where(pos >= dist, pltpu.roll(v, dist, axis=1), 0.0)


def _lane_scan(xr, xi, lam, unit, pos, seg, reverse):
    k = 0
    while (1 << k) < seg:
        dist = 1 << k
        mr, mi = _cmul(lam(unit * dist), _lane_shift(xr, dist, pos, seg, reverse),
                       _lane_shift(xi, dist, pos, seg, reverse))
        xr, xi = xr + mr, xi + mi
        k += 1
    return _lane_shift(xr, 1, pos, seg, reverse), _lane_shift(xi, 1, pos, seg, reverse)


def _s5core_kernel(utl_ref, utc_ref, mt_ref, bst_ref, cst_ref, lp_ref, ytl_ref, *ytc_out,
                   nb, ctx_chunks, pows):
    for q in range(utl_ref.shape[0]):
        _s5core_group(q, utl_ref, utc_ref, mt_ref, bst_ref, cst_ref, lp_ref, ytl_ref,
                      ytc_out[0] if ytc_out else None, nb, ctx_chunks, pows)


def _s5core_group(q, utl_ref, utc_ref, mt_ref, bst_ref, cst_ref, lp_ref, ytl_ref, ytc_ref,
                  nb, ctx_chunks, pows):
    p = S5_STATE
    nv = utl_ref.shape[1]
    ul = jnp.concatenate([utl_ref[q, v] for v in range(nv)], axis=1)
    uc = utc_ref[q]
    lg = LANES // nb
    bst, cst, mt = bst_ref[q], cst_ref[q], mt_ref[q]
    lp = lp_ref[q]

    def lam_of(d):
        def lam(n):
            k = pows.index(n)
            return lp[2 * d][:, k:k + 1], lp[2 * d + 1][:, k:k + 1]
        return lam

    lane = lax.broadcasted_iota(jnp.int32, (1, LANES), 1)
    sc = jnp.dot(bst, uc, preferred_element_type=_F32)
    posc = lane & (CTX_SEG - 1)
    hc = []
    for d in range(2):
        r0 = 2 * p * d
        hr, hi = _lane_scan(sc[r0:r0 + p], sc[r0 + p:r0 + 2 * p], lam_of(d), 1, posc, CTX_SEG, d == 1)
        hc += [hr, hi]
    sl = jnp.dot(bst, ul, preferred_element_type=_F32)
    y0 = jnp.dot(mt, ul, preferred_element_type=_F32)
    for v in range(nv):
        ytl_ref[q, v] = y0[:, v * LANES:(v + 1) * LANES]
    posl = lane & (lg - 1)
    bidl = lane >> int(math.log2(lg))
    hl = []
    for d in range(2):
        lam = lam_of(d)
        r0 = 2 * p * d
        sr = [sl[r0:r0 + p, v * LANES:(v + 1) * LANES] for v in range(nv)]
        si = [sl[r0 + p:r0 + 2 * p, v * LANES:(v + 1) * LANES] for v in range(nv)]
        order = list(range(nv)) if d == 0 else list(range(nv - 1, -1, -1))
        ir, ii = {}, {}
        prev = None
        for v in order:
            if prev is None:
                ir[v], ii[v] = sr[v], si[v]
            else:
                mr, mi = _cmul(lam(1), ir[prev], ii[prev])
                ir[v], ii[v] = mr + sr[v], mi + si[v]
            prev = v
        entry = 0 if d == 0 else lg - 1
        src = (CTX_OFF + ctx_chunks) if d == 0 else (CTX_OFF - 1)
        h0r = jnp.zeros((p, LANES), _F32)
        h0i = jnp.zeros((p, LANES), _F32)
        for b in range(nb):
            m = (bidl == b) & (posl == entry)
            col = b * CTX_SEG + src
            h0r = jnp.where(m, hc[2 * d][:, col:col + 1], h0r)
            h0i = jnp.where(m, hc[2 * d + 1][:, col:col + 1], h0i)
        jr, ji = _cmul(lam(nv), h0r, h0i)
        er, ei = _lane_scan(ir[prev] + jr, ii[prev] + ji, lam, nv, posl, lg, d == 1)
        er, ei = er + h0r, ei + h0i
        hr, hi = {order[0]: er}, {order[0]: ei}
        for n, v in enumerate(order[1:], start=1):
            mr, mi = _cmul(lam(n), er, ei)
            hr[v], hi[v] = ir[order[n - 1]] + mr, ii[order[n - 1]] + mi
        hl += [jnp.concatenate([hr[v] for v in range(nv)], axis=1),
               jnp.concatenate([hi[v] for v in range(nv)], axis=1)]
    h = jnp.concatenate(hl, axis=0).astype(_BF16)
    y1 = jnp.dot(cst, h, preferred_element_type=_F32)
    for v in range(nv):
        ytl_ref[q, v] += y1[:, v * LANES:(v + 1) * LANES]
    if ytc_ref is not None:
        hcb = jnp.concatenate(hc, axis=0).astype(_BF16)
        yc = jnp.dot(mt, uc, preferred_element_type=_F32)
        ytc_ref[q] = yc + jnp.dot(cst, hcb, preferred_element_type=_F32)


def _s5_pows(nv, lg):
    pows = set(range(1, nv + 1))
    pows |= {nv << k for k in range(int(math.log2(lg)))}
    pows |= {1 << k for k in range(int(math.log2(CTX_SEG)))}
    return tuple(sorted(pows))


def _s5core(utl, utc, mt, bst, cst, lampow, nb, ctx_chunks, pows, want_yc):
    g, nv, rows, _ = utl.shape
    gq = S5_CORE_GROUPS
    out_shape = [jax.ShapeDtypeStruct((g, nv, rows, LANES), _F32)]
    out_specs = [pl.BlockSpec((gq, nv, rows, LANES), lambda i: (i, 0, 0, 0))]
    if want_yc:
        out_shape.append(jax.ShapeDtypeStruct((g, rows, LANES), _F32))
        out_specs.append(pl.BlockSpec((gq, rows, LANES), lambda i: (i, 0, 0)))
    res = pl.pallas_call(
        functools.partial(_s5core_kernel, nb=nb, ctx_chunks=ctx_chunks, pows=pows),
        out_shape=out_shape,
        grid=(g // gq,),
        in_specs=[pl.BlockSpec((gq, nv, rows, LANES), lambda i: (i, 0, 0, 0)),
                  pl.BlockSpec((gq, rows, LANES), lambda i: (i, 0, 0)),
                  pl.BlockSpec((gq, rows, rows), lambda i: (i, 0, 0)),
                  pl.BlockSpec((gq, S5_NSTATE, rows), lambda i: (i, 0, 0)),
                  pl.BlockSpec((gq, rows, S5_NSTATE), lambda i: (i, 0, 0)),
                  pl.BlockSpec((gq, 4, S5_STATE, LANES), lambda i: (i, 0, 0, 0))],
        out_specs=out_specs,
        compiler_params=_cparams(("arbitrary",)),
        name="s5core_yc" if want_yc else "s5core",
    )(utl, utc, mt, bst, cst, lampow)
    return (res[0], res[1]) if want_yc else (res[0], None)


def _s5prep_kernel(*refs):
    for q in range(refs[0].shape[1]):
        _s5prep_group(q, *refs)


def _s5prep_group(q, acol_ref, arow_ref, ldt_ref, bre_ref, bim_ref, cre_ref, cim_ref, c1_ref, c2_ref,
                  esel_ref, etile_ref, e2_ref, e3_ref, pcol_ref, mt_ref, bst_ref, cst_ref, lp_ref):
    hp = lax.Precision.HIGHEST
    dot = functools.partial(jnp.dot, precision=hp, preferred_element_type=_F32)
    t_len, p = S5_CHUNK, S5_STATE
    lane = lax.broadcasted_iota(jnp.int32, (1, LANES), 1)
    n_lane = lane.astype(_F32)
    npow = pcol_ref.shape[0]
    strips = []
    for d in range(2):
        dt = jnp.exp(ldt_ref[d, q])
        a = acol_ref[d, q]
        ar, ai = a[:, 0:1], a[:, 1:2]
        dar, dai = ar * dt, ai * dt
        pm, pa = jnp.exp(n_lane * dar), n_lane * dai
        lr, li = pm * jnp.cos(pa), pm * jnp.sin(pa)
        lbr, lbi = lr[:, 1:2], li[:, 1:2]
        den = ar * ar + ai * ai
        nr, ni = lbr - 1.0, lbi
        kr = (nr * ar + ni * ai) / den
        ki = (ni * ar - nr * ai) / den
        bbr = kr * bre_ref[d, q] - ki * bim_ref[d, q]
        bbi = kr * bim_ref[d, q] + ki * bre_ref[d, q]
        wr = _select_cols(lr, esel_ref[d])
        wi = _select_cols(li, esel_ref[d])
        btr = _select_cols(bbr, etile_ref[...])
        bti = _select_cols(bbi, etile_ref[...])
        bsr, bsi = wr * btr - wi * bti, wr * bti + wi * btr
        bst_ref[q, (2 * d) * p:(2 * d + 1) * p, :] = bsr.astype(_BF16)
        bst_ref[q, (2 * d + 1) * p:(2 * d + 2) * p, :] = bsi.astype(_BF16)
        strips.append(dot(cre_ref[d, q], bsr) - dot(cim_ref[d, q], bsi))
        w1 = _select_rows(e2_ref[d], jnp.concatenate([lr, li], axis=0).T)
        w2 = _select_rows(e2_ref[d], jnp.concatenate([li, lr], axis=0).T)
        c1 = _select_rows(e3_ref[...], c1_ref[d, q])
        c2 = _select_rows(e3_ref[...], c2_ref[d, q])
        cst_ref[q, :, d * LANES:(d + 1) * LANES] = (c1 * w1 + c2 * w2).astype(_BF16)
        arow = arow_ref[d, q]
        m = pcol_ref[...] * float(t_len)
        qm, qa = jnp.exp(m * (arow[0:1] * dt)), m * (arow[1:2] * dt)
        packed = qm * jnp.where(lane < p, jnp.cos(qa), jnp.sin(qa))
        lpt = jnp.concatenate([packed, jnp.zeros((LANES - npow, LANES), _F32)], axis=0).T
        lp_ref[q, 2 * d] = lpt[:p]
        lp_ref[q, 2 * d + 1] = lpt[p:]
    rf, rb = strips
    lane_w = lax.broadcasted_iota(jnp.int32, (1, S5_ROWS), 1)
    for t in range(t_len):
        sf = (t_len - 1 - t) * S5_GROUP
        f = rf if sf == 0 else pltpu.roll(rf, S5_ROWS - sf, axis=1)
        f = jnp.where(lane_w < (t + 1) * S5_GROUP, f, 0.0)
        sb = t * S5_GROUP
        bk = rb if sb == 0 else pltpu.roll(rb, sb, axis=1)
        bk = jnp.where(lane_w >= sb, bk, 0.0)
        mt_ref[q, t * S5_GROUP:(t + 1) * S5_GROUP, :] = (f + bk).astype(_BF16)


def _s5prep_constants(pows):
    t_len, h = S5_CHUNK, S5_GROUP
    s_of = np.arange(S5_ROWS) // h
    h_of = np.arange(S5_ROWS) % h
    n128 = np.arange(LANES)
    esel = np.stack([n128[:, None] == (t_len - 1 - s_of)[None, :], n128[:, None] == s_of[None, :]])
    etile = np.arange(h)[:, None] == h_of[None, :]
    e2 = np.stack([(s_of + 1)[:, None] == n128[None, :], (t_len - s_of)[:, None] == n128[None, :]])
    e3 = h_of[:, None] == np.arange(h)[None, :]
    pcol = np.zeros((16, 1), np.float32)
    pcol[:len(pows), 0] = pows
    f = lambda m: jnp.asarray(m, dtype=_BF16)
    return f(esel), f(etile), f(e2), f(e3), jnp.asarray(pcol)


def _s5prep(a_re, a_im, log_dt, b_re, b_im, c_re, c_im, pows):
    g, p, h = S5_GROUPS, S5_STATE, S5_GROUP
    acol = jnp.stack([a_re, a_im], axis=-1)
    arow = jnp.stack([jnp.concatenate([a_re, a_re], -1),
                      jnp.concatenate([a_im, a_im], -1)], axis=2)
    ldt = log_dt.reshape(2, g, 1, 1)
    c1 = jnp.concatenate([c_re, -c_re], -1)
    c2 = jnp.concatenate([-c_im, -c_im], -1)
    esel, etile, e2, e3, pcol = _s5prep_constants(pows)

    gq = S5_PREP_GROUPS

    def per_g(*tail):
        return pl.BlockSpec((2, gq) + tail, lambda i: (0, i) + (0,) * len(tail))

    return pl.pallas_call(
        _s5prep_kernel,
        out_shape=[jax.ShapeDtypeStruct((g, S5_ROWS, S5_ROWS), _BF16),
                   jax.ShapeDtypeStruct((g, S5_NSTATE, S5_ROWS), _BF16),
                   jax.ShapeDtypeStruct((g, S5_ROWS, S5_NSTATE), _BF16),
                   jax.ShapeDtypeStruct((g, 4, p, LANES), _F32)],
        grid=(g // gq,),
        in_specs=[per_g(p, 2), per_g(2, LANES), per_g(1, 1), per_g(p, h), per_g(p, h),
                  per_g(h, p), per_g(h, p), per_g(h, LANES), per_g(h, LANES),
                  _resident(esel.shape), _resident(etile.shape), _resident(e2.shape), _resident(e3.shape),
                  _resident(pcol.shape)],
        out_specs=[pl.BlockSpec((gq, S5_ROWS, S5_ROWS), lambda i: (i, 0, 0)),
                   pl.BlockSpec((gq, S5_NSTATE, S5_ROWS), lambda i: (i, 0, 0)),
                   pl.BlockSpec((gq, S5_ROWS, S5_NSTATE), lambda i: (i, 0, 0)),
                   pl.BlockSpec((gq, 4, p, LANES), lambda i: (i, 0, 0, 0))],
        compiler_params=_cparams(("arbitrary",)),
        name="s5prep",
    )(acol, arow, ldt, b_re, b_im, c_re, c_im, c1, c2, esel, etile, e2, e3, pcol)


def _ctx_to_ut(xn):
    b, n, _ = xn.shape
    c = n // S5_CHUNK
    v = xn.reshape(b, c, S5_CHUNK, S5_GROUPS, S5_GROUP).transpose(3, 2, 4, 0, 1)
    v = jnp.pad(v, ((0, 0),) * 4 + ((CTX_OFF, CTX_SEG - c - CTX_OFF),))
    v = v.reshape(S5_GROUPS, S5_ROWS, b * CTX_SEG)
    return jnp.pad(v, ((0, 0), (0, 0), (0, LANES - b * CTX_SEG)))


def _ctx_from_yt(yt, b, n):
    c = n // S5_CHUNK
    v = yt[:, :, :b * CTX_SEG].reshape(S5_GROUPS, S5_CHUNK, S5_GROUP, b, CTX_SEG)
    v = v[..., CTX_OFF:CTX_OFF + c]
    return v.transpose(3, 4, 1, 0, 2).reshape(b, n, D_MODEL)


def _s5_glu(xf, xn, y, dsk, gw, gb, g1):
    z = _gelu_tanh(y + dsk * xn)
    gate = _sigmoid(jnp.dot(z.astype(_BF16), gw, preferred_element_type=_F32) + gb)
    return xf + g1 * (z * gate)


def _s5post_ctx_kernel(x_ref, y_ref, mod_ref, g_ref, dsk_ref, gw_ref, gb_ref, o_ref):
    sh1, sc1, g1 = _mod_slices(mod_ref[0])[:3]
    xf = x_ref[0]
    xn = _rms_mod(xf, g_ref[...], sh1, sc1)
    o_ref[0] = _s5_glu(xf, xn, y_ref[0], dsk_ref[...], gw_ref[0], gb_ref[...], g1)


def _s5post_ctx(x, y, mod, mod_row, g, dsk, gw, layer, gb):
    b, n, d = x.shape
    tok = pl.BlockSpec((1, n, d), lambda bi: (bi, 0, 0))
    vec = pl.BlockSpec((1, d), lambda bi: (0, 0))
    return pl.pallas_call(
        _s5post_ctx_kernel,
        out_shape=jax.ShapeDtypeStruct((b, n, d), _F32),
        grid=(b,),
        in_specs=[tok, tok, pl.BlockSpec((1, 1, 6 * d), lambda bi: (mod_row, 0, 0)),
                  vec, vec, _resident_layer((d, d), layer), vec],
        out_specs=tok,
        compiler_params=_cparams(("arbitrary",)),
        name="s5post_ctx",
    )(x, y, mod, g, dsk, gw, gb)


def _s5post_kernel(x_ref, yt_ref, mod_ref, g_ref, dsk_ref, gw_ref, gb_ref, o_ref, *, nb, k):
    d = D_MODEL
    lg = LANES // nb
    gn = g_ref[...]
    dsk = dsk_ref[...]
    c = math.sqrt(2.0 / math.pi)
    mods = [_mod_slices(mod_ref[b])[:3] for b in range(nb)]
    ys = [yt_ref[:, 0, j * S5_GROUP:(j + 1) * S5_GROUP, :].reshape(d, LANES).T for j in range(k)]
    rows = lg * k
    xs, zs = [], []
    for b in range(nb):
        sh1, sc1, _ = mods[b]
        yb = jnp.stack([ys[j][b * lg:(b + 1) * lg] for j in range(k)], axis=0)
        yb = jnp.swapaxes(yb, 0, 1).reshape(rows, d)
        xf = x_ref[b].reshape(rows, d)
        ms = jnp.mean(xf * xf, axis=-1, keepdims=True)
        pre = (xf * lax.rsqrt(ms + EPS)) * ((gn * (1.0 + sc1)) * dsk) + (yb + sh1 * dsk)
        xs.append(xf)
        zs.append(pre + pre * jnp.tanh(pre * (c + (c * 0.044715) * (pre * pre))))
    z2 = jnp.concatenate(zs, axis=0)
    th = jnp.tanh(jnp.dot(z2.astype(_BF16), gw_ref[0], preferred_element_type=_F32) + gb_ref[...])
    for b in range(nb):
        q = (0.25 * mods[b][2]) * z2[b * rows:(b + 1) * rows]
        o_ref[b] = (xs[b] + (q * th[b * rows:(b + 1) * rows] + q)).reshape(lg, k, d)


def _s5post(x, yt, mod, g, dsk, gw, layer, gb):
    b, n, d = x.shape
    lg = LANES // b
    per = n // lg
    k = S5_POST_TOK
    blocks_per_v = S5_CHUNK // k
    tok = pl.BlockSpec((b, lg, k, d), lambda r: (0, 0, r, 0))
    vec = pl.BlockSpec((1, d), lambda r: (0, 0))
    out = pl.pallas_call(
        functools.partial(_s5post_kernel, nb=b, k=k),
        out_shape=jax.ShapeDtypeStruct((b, lg, per, d), _F32),
        grid=(per // k,),
        in_specs=[tok,
                  pl.BlockSpec((S5_GROUPS, 1, k * S5_GROUP, LANES),
                               lambda r: (0, r // blocks_per_v, r % blocks_per_v, 0)),
                  pl.BlockSpec((8, 1, 6 * d), lambda r: (0, 0, 0)),
                  vec, vec, _resident_layer((d, d), layer), vec],
        out_specs=tok,
        compiler_params=_cparams(("arbitrary",)),
        name="s5post",
    )(x.reshape(b, lg, per, d), yt, mod, g, dsk, gw, gb)
    return out.reshape(b, n, d)


def _mlp_kernel(x_ref, mod_ref, g_ref, w1_ref, b1_ref, w2_ref, b2_ref, fg_ref, o_ref, *, final):
    sh2, sc2, g2 = _mod_slices(mod_ref[0])[3:]
    xf = x_ref[0]
    xn = _rms_mod(xf, g_ref[...], sh2, sc2).astype(_BF16)
    acc = jnp.zeros(xf.shape, _F32)
    fc = D_MODEL
    for j in range(D_FF // fc):
        a = jnp.dot(xn, w1_ref[0, :, j * fc:(j + 1) * fc], preferred_element_type=_F32)
        a = jnp.maximum(a + b1_ref[:, j * fc:(j + 1) * fc], 0.0)
        acc = acc + jnp.dot((a * a).astype(_BF16), w2_ref[0, j * fc:(j + 1) * fc, :],
                            preferred_element_type=_F32)
    out = xf + g2 * (acc + b2_ref[...])
    if final:
        ms = jnp.mean(out * out, axis=-1, keepdims=True)
        out = out * lax.rsqrt(ms + EPS) * fg_ref[...]
    o_ref[0] = out


def _mlp(x, mod, mod_row, g, w1, b1, w2, b2, layer, fg, tt, final):
    b, n, d = x.shape
    row = (lambda bi: bi) if mod_row is None else (lambda bi: mod_row)
    tok = pl.BlockSpec((1, tt, d), lambda bi, t: (bi, t, 0))
    vec = pl.BlockSpec((1, d), lambda bi, t: (0, 0))
    return pl.pallas_call(
        functools.partial(_mlp_kernel, final=final),
        out_shape=jax.ShapeDtypeStruct((b, n, d), _F32),
        grid=(b, n // tt),
        in_specs=[tok, pl.BlockSpec((1, 1, 6 * d), lambda bi, t: (row(bi), 0, 0)), vec,
                  _resident_layer((d, D_FF), layer), pl.BlockSpec((1, D_FF), lambda bi, t: (0, 0)),
                  _resident_layer((D_FF, d), layer), vec, vec],
        out_specs=tok,
        compiler_params=_cparams(("arbitrary", "arbitrary")),
        name="mlp_final" if final else "mlp",
    )(x, mod, g, w1, b1, w2, b2, fg)


def _window_in_rows(w):
    need = POOL_TILE_ROWS + w - 1
    return -(-need // POOL_TILE_ROWS) * POOL_TILE_ROWS


def _grid_window_matrices():
    to = np.arange(POOL_TILE_ROWS * GRID_W)
    ro, co = to // GRID_W, to % GRID_W
    mats = []
    for w in POOL_WINDOWS:
        ti = np.arange(_window_in_rows(w) * GRID_W)
        ri, ci = ti // GRID_W - w // 2, ti % GRID_W
        m = ((ri[None, :] >= (ro - w // 2)[:, None]) & (ri[None, :] < (ro + w - w // 2)[:, None])
             & (ci[None, :] >= (co - w // 2)[:, None]) & (ci[None, :] < (co + w - w // 2)[:, None]))
        mats.append(jnp.asarray(m, dtype=_BF16))
    return mats


def _grid_inverse_counts(rows):
    t = np.arange(rows * GRID_W)
    r, c = t // GRID_W, t % GRID_W
    tabs = []
    for w in POOL_WINDOWS:
        rc = np.minimum(r + w - w // 2, rows) - np.maximum(r - w // 2, 0)
        cc = np.minimum(c + w - w // 2, GRID_W) - np.maximum(c - w // 2, 0)
        tabs.append(1.0 / (rc * cc))
    return jnp.asarray(np.stack(tabs), dtype=_F32)


def _seq_window_matrices(n):
    t = np.arange(n)
    mats = []
    for w in POOL_WINDOWS:
        lo, hi = np.maximum(t - w // 2, 0), np.minimum(t + w - w // 2, n)
        mats.append((t[None, :] >= lo[:, None]) & (t[None, :] < hi[:, None]))
    return jnp.asarray(np.stack(mats), dtype=_BF16)


POOL_BLOCK_ROWS = 16
POOL_HALO_ROWS = 8
POOL_TILE_ROWS = 4


def _pool_kernel(xm_ref, xb_ref, mod_ref, g_ref, a2_ref, a4_ref, a8_ref, a16_ref, pw_ref, ps_ref, inv_ref,
                 o_ref, tail):
    i = pl.program_id(0)
    b = pl.program_id(1)
    nblk = pl.num_programs(0)
    sh1, sc1, g1 = _mod_slices(mod_ref[0])[:3]
    gn = g_ref[...]
    halo = POOL_HALO_ROWS * GRID_W
    main = POOL_BLOCK_ROWS * GRID_W
    tile = POOL_TILE_ROWS * GRID_W
    xm = xm_ref[0]
    xn_m = _rms_mod(xm, gn, sh1, sc1)
    xn_mb = xn_m.astype(_BF16)
    @pl.when(i == 0)
    def _():
        tail[b] = jnp.zeros((halo, D_MODEL), _BF16)

    xn_t = tail[b]
    xn_b = jnp.where(i < nblk - 1, _rms_mod(xb_ref[0], gn, sh1, sc1), 0.0).astype(_BF16)
    tail[b] = xn_mb[main - halo:]
    xn_ext = jnp.concatenate([xn_t, xn_mb, xn_b], axis=0)
    for gi, (w, a_ref) in enumerate(zip(POOL_WINDOWS, (a2_ref, a4_ref, a8_ref, a16_ref))):
        ch = slice(gi * POOL_CH, (gi + 1) * POOL_CH)
        a = a_ref[...]
        span = a.shape[1]
        tots = []
        for k in range(main // tile):
            start = halo + k * tile - (w // 2) * GRID_W
            tots.append(jnp.dot(a, xn_ext[start:start + span, ch], preferred_element_type=_F32))
        tot = jnp.concatenate(tots, axis=0)
        inv = inv_ref[gi]
        p = tot * jnp.concatenate([inv] * (POOL_CH // LANES), axis=1) - xn_m[:, ch]
        y = jnp.dot(p.astype(_BF16), pw_ref[0, gi], preferred_element_type=_F32) * ps_ref[:, ch]
        o_ref[0, :, ch] = xm[:, ch] + g1[:, ch] * y


def _pool_grid(x, mod, g, pw, layer, ps):
    b, n, d = x.shape
    rows = n // GRID_W
    main = POOL_BLOCK_ROWS * GRID_W
    halo = POOL_HALO_ROWS * GRID_W
    nblk = n // main
    ratio = main // halo
    nh = n // halo
    amats = _grid_window_matrices()
    assert all(w // 2 <= POOL_HALO_ROWS and _window_in_rows(w) - w // 2 <= POOL_TILE_ROWS + POOL_HALO_ROWS
               for w in POOL_WINDOWS)
    inv = jnp.broadcast_to(_grid_inverse_counts(rows)[:, :, None], (len(POOL_WINDOWS), n, LANES))
    vec = pl.BlockSpec((1, d), lambda i, bi: (0, 0))
    return pl.pallas_call(
        _pool_kernel,
        out_shape=jax.ShapeDtypeStruct((b, n, d), _F32),
        grid=(nblk, b),
        in_specs=[pl.BlockSpec((1, main, d), lambda i, bi: (bi, i, 0)),
                  pl.BlockSpec((1, halo, d), lambda i, bi: (bi, jnp.minimum((i + 1) * ratio, nh - 1), 0)),
                  pl.BlockSpec((1, 1, 6 * d), lambda i, bi: (bi, 0, 0)),
                  vec,
                  *[_resident(a.shape) for a in amats],
                  _resident_layer((len(POOL_WINDOWS), POOL_CH, POOL_CH), layer),
                  vec,
                  pl.BlockSpec((len(POOL_WINDOWS), main, LANES), lambda i, bi: (0, i, 0))],
        out_specs=pl.BlockSpec((1, main, d), lambda i, bi: (bi, i, 0)),
        scratch_shapes=[pltpu.VMEM((b, halo, d), _BF16)],
        compiler_params=_cparams(("arbitrary", "arbitrary")),
        name="pool_grid",
    )(x, x, mod, g, *amats, pw, ps, inv)


def _poolseq_kernel(x_ref, mod_ref, g_ref, aseq_ref, pw_ref, ps_ref, o_ref):
    sh1, sc1, g1 = _mod_slices(mod_ref[0])[:3]
    xf = x_ref[0]
    n = xf.shape[0]
    xn = _rms_mod(xf, g_ref[...], sh1, sc1)
    xb = xn.astype(_BF16)
    t = lax.broadcasted_iota(jnp.int32, (n, POOL_CH), 0)
    for gi, w in enumerate(POOL_WINDOWS):
        ch = slice(gi * POOL_CH, (gi + 1) * POOL_CH)
        tot = jnp.dot(aseq_ref[gi], xb[:, ch], preferred_element_type=_F32)
        cnt = jnp.minimum(t + (w - w // 2), n) - jnp.maximum(t - w // 2, 0)
        p = tot / cnt.astype(_F32) - xn[:, ch]
        y = jnp.dot(p.astype(_BF16), pw_ref[0, gi], preferred_element_type=_F32) * ps_ref[:, ch]
        o_ref[0, :, ch] = xf[:, ch] + g1[:, ch] * y


def _pool_seq(x, mod, mod_row, g, pw, layer, ps):
    b, n, d = x.shape
    aseq = _seq_window_matrices(n)
    vec = pl.BlockSpec((1, d), lambda bi: (0, 0))
    return pl.pallas_call(
        _poolseq_kernel,
        out_shape=jax.ShapeDtypeStruct((b, n, d), _F32),
        grid=(b,),
        in_specs=[pl.BlockSpec((1, n, d), lambda bi: (bi, 0, 0)),
                  pl.BlockSpec((1, 1, 6 * d), lambda bi: (mod_row, 0, 0)),
                  vec,
                  _resident((len(POOL_WINDOWS), n, n)),
                  _resident_layer((len(POOL_WINDOWS), POOL_CH, POOL_CH), layer),
                  vec],
        out_specs=pl.BlockSpec((1, n, d), lambda bi: (bi, 0, 0)),
        compiler_params=_cparams(("arbitrary",)),
        name="pool_seq",
    )(x, mod, g, aseq, pw, ps)


def kernel(x, c, ctx, c_ctx, ada_w, ada_b, norm1_g, norm2_g, s5_a_re, s5_a_im, s5_log_dt, s5_b_re, s5_b_im, s5_c_re, s5_c_im, s5_d, s5_glu_w, s5_glu_b, pool_w, pool_scale, mlp_w1, mlp_b1, mlp_w2, mlp_b2, final_g):
    bsz, n_tok, d = x.shape
    n_ctx = ctx.shape[1]
    depth = ada_w.shape[0]
    assert d == D_MODEL and n_tok % (POOL_BLOCK_ROWS * GRID_W) == 0
    assert LANES % bsz == 0 and bsz * CTX_SEG <= LANES and bsz < 8
    lat_chunks = n_tok // S5_CHUNK
    ctx_chunks = n_ctx // S5_CHUNK
    assert n_tok % S5_CHUNK == 0 and n_ctx % S5_CHUNK == 0 and (bsz * lat_chunks) % LANES == 0
    assert ctx_chunks + 2 * CTX_OFF <= CTX_SEG
    nv = bsz * lat_chunks // LANES
    pows = _s5_pows(nv, LANES // bsz)
    last_ctx_reader = ((depth - 1) // N_MIXERS) * N_MIXERS
    ctx_row = bsz
    tt = min(512, n_tok)
    tt_ctx = min(512, n_ctx)

    cc = jnp.zeros((8, d), _F32).at[:bsz].set(c).at[ctx_row].set(c_ctx)
    mods = _modulation(cc, ada_w, ada_b)
    w1 = mlp_w1.astype(_BF16)
    w2 = mlp_w2.astype(_BF16)
    glu_w = s5_glu_w.astype(_BF16)
    glu_w4 = (0.25 * s5_glu_w).astype(_BF16)
    pw = pool_w.astype(_BF16)

    h_ctx = ctx
    for i in range(depth):
        ctx_in = i <= last_ctx_reader
        ctx_out = i < last_ctx_reader
        j = i // N_MIXERS
        mod = mods[i].reshape(8, 1, 6 * d)
        g1n = norm1_g[i].reshape(1, d)
        g2n = norm2_g[i].reshape(1, d)
        if i % N_MIXERS == 0:
            mt, bst, cst, lampow = _s5prep(s5_a_re[j], s5_a_im[j], s5_log_dt[j], s5_b_re[j], s5_b_im[j],
                                           s5_c_re[j], s5_c_im[j], pows)
            dsk = s5_d[j].reshape(1, d)
            gb = s5_glu_b[j].reshape(1, d)
            if ctx_in:
                utc = _ctx_to_ut(_prenorm(h_ctx, mod, ctx_row, g1n, tt_ctx))
            else:
                utc = jnp.zeros((S5_GROUPS, S5_ROWS, LANES), _BF16)
            utl = _s5pre(x, mod, g1n)
            ytl, ytc = _s5core(utl, utc, mt, bst, cst, lampow, bsz, ctx_chunks, pows, ctx_out)
            x = _s5post(x, ytl, mod, g1n, dsk, glu_w4, j, 0.5 * gb)
            if ctx_out:
                h_ctx = _s5post_ctx(h_ctx, _ctx_from_yt(ytc, bsz, n_ctx), mod, ctx_row, g1n, dsk,
                                    glu_w, j, gb)
        else:
            ps = pool_scale[j].reshape(1, d)
            x = _pool_grid(x, mod, g1n, pw, j, ps)
            if ctx_out:
                h_ctx = _pool_seq(h_ctx, mod, ctx_row, g1n, pw, j, ps)
        final = i == depth - 1
        x = _mlp(x, mod, None, g2n, w1, mlp_b1[i].reshape(1, D_FF), w2, mlp_b2[i].reshape(1, d), i,
                 final_g.reshape(1, d), tt, final)
        if ctx_out:
            h_ctx = _mlp(h_ctx, mod, ctx_row, g2n, w1, mlp_b1[i].reshape(1, D_FF), w2,
                         mlp_b2[i].reshape(1, d), i, final_g.reshape(1, d), tt_ctx, False)
    return x
```

```python
import functools
import math

import numpy as np
import jax
import jax.numpy as jnp
from jax import lax
from jax.experimental import pallas as pl
from jax.experimental.pallas import tpu as pltpu

D_MODEL = 1024
GRID_W = 64
S5_GROUP = 16
S5_GROUPS = D_MODEL // S5_GROUP
S5_STATE = 64
POOL_WINDOWS = (2, 4, 8, 16)
POOL_CH = D_MODEL // len(POOL_WINDOWS)
D_FF = 4 * D_MODEL
N_MIXERS = 2
EPS = 1e-6
LANES = 128

S5_CHUNK = 32
S5_ROWS = S5_CHUNK * S5_GROUP
S5_NSTATE = 4 * S5_STATE
CTX_SEG = 32
CTX_OFF = 8
S5_PRE_TOK = 8
S5_POST_TOK = 8
S5_CORE_GROUPS = 2
S5_PREP_GROUPS = 4
V7X_VMEM_LIMIT_BYTES = 56 * 1024 * 1024

_F32 = jnp.float32
_BF16 = jnp.bfloat16


def _cparams(sem):
    return pltpu.CompilerParams(dimension_semantics=sem, vmem_limit_bytes=V7X_VMEM_LIMIT_BYTES)


def _resident(shape):
    nd = len(shape)
    return pl.BlockSpec(shape, lambda *_: (0,) * nd, pipeline_mode=pl.Buffered(1))


def _resident_layer(shape, layer):
    nd = len(shape)
    return pl.BlockSpec((1,) + tuple(shape), lambda *_: (layer,) + (0,) * nd, pipeline_mode=pl.Buffered(1))


def _mod_kernel(cc_ref, w_ref, b_ref, o_ref):
    c = cc_ref[...]
    a = c * jax.nn.sigmoid(c)
    o_ref[0] = jnp.dot(a, w_ref[0], preferred_element_type=_F32,
                       precision=lax.Precision.HIGHEST) + b_ref[0]


def _modulation(cc, ada_w, ada_b):
    depth, d, n = ada_w.shape
    tn = 1536
    return pl.pallas_call(
        _mod_kernel,
        out_shape=jax.ShapeDtypeStruct((depth, 8, n), _F32),
        grid=(depth, n // tn),
        in_specs=[pl.BlockSpec((8, d), lambda i, j: (0, 0)),
                  pl.BlockSpec((1, d, tn), lambda i, j: (i, 0, j)),
                  pl.BlockSpec((1, 1, tn), lambda i, j: (i, 0, j))],
        out_specs=pl.BlockSpec((1, 8, tn), lambda i, j: (i, 0, j)),
        compiler_params=_cparams(("arbitrary", "arbitrary")),
        name="modulation",
    )(cc, ada_w, ada_b.reshape(depth, 1, n))


def _rms_mod(xf, g, shift, scale):
    ms = jnp.mean(xf * xf, axis=-1, keepdims=True)
    return (xf * lax.rsqrt(ms + EPS)) * (g * (1.0 + scale)) + shift


def _mod_slices(m):
    d = D_MODEL
    return tuple(m[:, k * d:(k + 1) * d] for k in range(6))


def _cmul(a, yr, yi):
    ar, ai = a
    return ar * yr - ai * yi, ar * yi + ai * yr


def _bf16_pair(x):
    hi = x.astype(_BF16)
    lo = (x - hi.astype(_F32)).astype(_BF16)
    return hi, lo


def _select_cols(x, sel):
    return sum(jnp.dot(part, sel, preferred_element_type=_F32) for part in _bf16_pair(x))


def _select_rows(sel, x):
    return sum(jnp.dot(sel, part, preferred_element_type=_F32) for part in _bf16_pair(x))


def _dot_split(a, b):
    (ah, al), (bh, bl) = _bf16_pair(a), _bf16_pair(b)
    return (jnp.dot(ah, bh, preferred_element_type=_F32) + jnp.dot(ah, bl, preferred_element_type=_F32)
            + jnp.dot(al, bh, preferred_element_type=_F32))


def _s5pre_kernel(x_ref, mod_ref, g_ref, ut_ref, *, k, lanes):
    seg, off, mod_rows = lanes
    nb, cnt = x_ref.shape[0], x_ref.shape[1]
    gn = g_ref[...]
    mods = [_mod_slices(mod_ref[mod_rows[b]])[:2] for b in range(nb)]
    xs = [jnp.swapaxes(x_ref[b], 0, 1) for b in range(nb)]
    pads = (off, seg - off - cnt)
    for j in range(k):
        parts = []
        for b in range(nb):
            piece = [_rms_mod(xs[b][j], gn, *mods[b])]
            if pads[0]:
                piece.insert(0, jnp.zeros((pads[0], D_MODEL), _F32))
            if pads[1]:
                piece.append(jnp.zeros((pads[1], D_MODEL), _F32))
            parts += piece
        if LANES - nb * seg:
            parts.append(jnp.zeros((LANES - nb * seg, D_MODEL), _F32))
        xn = jnp.concatenate(parts, axis=0)
        ut_ref[:, 0, j * S5_GROUP:(j + 1) * S5_GROUP, :] = (
            xn.T.astype(_BF16).reshape(S5_GROUPS, S5_GROUP, LANES))


def _s5pre(x4, mod, g, lanes):
    b, cnt, per, d = x4.shape
    k = S5_PRE_TOK
    blocks_per_v = S5_CHUNK // k
    return pl.pallas_call(
        functools.partial(_s5pre_kernel, k=k, lanes=lanes),
        out_shape=jax.ShapeDtypeStruct((S5_GROUPS, per // S5_CHUNK, S5_ROWS, LANES), _BF16),
        grid=(per // k,),
        in_specs=[pl.BlockSpec((b, cnt, k, d), lambda r: (0, 0, r, 0)),
                  pl.BlockSpec((8, 1, 6 * d), lambda r: (0, 0, 0)),
                  pl.BlockSpec((1, d), lambda r: (0, 0))],
        out_specs=pl.BlockSpec((S5_GROUPS, 1, k * S5_GROUP, LANES),
                               lambda r: (0, r // blocks_per_v, r % blocks_per_v, 0)),
        compiler_params=_cparams(("arbitrary",)),
        name="s5pre",
    )(x4, mod, g)


def _lane_shift(v, dist, pos, seg, reverse):
    n = v.shape[1]
    if reverse:
        return jnp.where(pos < seg - dist, pltpu.roll(v, n - dist, axis=1), 0.0)
    return jnp.where(pos >= dist, pltpu.roll(v, dist, axis=1), 0.0)


def _lane_scan(xr, xi, lam, unit, pos, seg, reverse):
    k = 0
    while (1 << k) < seg:
        dist = 1 << k
        mr, mi = _cmul(lam(unit * dist), _lane_shift(xr, dist, pos, seg, reverse),
                       _lane_shift(xi, dist, pos, seg, reverse))
        xr, xi = xr + mr, xi + mi
        k += 1
    return _lane_shift(xr, 1, pos, seg, reverse), _lane_shift(xi, 1, pos, seg, reverse)


def _s5core_kernel(utl_ref, utc_ref, mt_ref, bst_ref, cst_ref, lp_ref, ytl_ref, *ytc_out,
                   nb, ctx_chunks, pows):
    for q in range(utl_ref.shape[0]):
        _s5core_group(q, utl_ref, utc_ref, mt_ref, bst_ref, cst_ref, lp_ref, ytl_ref,
                      ytc_out[0] if ytc_out else None, nb, ctx_chunks, pows)


def _s5core_group(q, utl_ref, utc_ref, mt_ref, bst_ref, cst_ref, lp_ref, ytl_ref, ytc_ref,
                  nb, ctx_chunks, pows):
    p = S5_STATE
    nv = utl_ref.shape[1]
    ul = jnp.concatenate([utl_ref[q, v] for v in range(nv)], axis=1)
    uc = utc_ref[q]
    lg = LANES // nb
    bst, cst, mt = bst_ref[q], cst_ref[q], mt_ref[q]
    lp = lp_ref[q]

    def lam_of(d):
        def lam(n):
            k = pows.index(n)
            return lp[2 * d][:, k:k + 1], lp[2 * d + 1][:, k:k + 1]
        return lam

    lane = lax.broadcasted_iota(jnp.int32, (1, LANES), 1)
    sc = jnp.dot(bst, uc, preferred_element_type=_F32)
    posc = lane & (CTX_SEG - 1)
    hc = []
    for d in range(2):
        r0 = 2 * p * d
        hr, hi = _lane_scan(sc[r0:r0 + p], sc[r0 + p:r0 + 2 * p], lam_of(d), 1, posc, CTX_SEG, d == 1)
        hc += [hr, hi]
    sl = jnp.dot(bst, ul, preferred_element_type=_F32)
    y0 = jnp.dot(mt, ul, preferred_element_type=_F32)
    for v in range(nv):
        ytl_ref[q, v] = y0[:, v * LANES:(v + 1) * LANES]
    posl = lane & (lg - 1)
    bidl = lane >> int(math.log2(lg))
    hl = []
    for d in range(2):
        lam = lam_of(d)
        r0 = 2 * p * d
        sr = [sl[r0:r0 + p, v * LANES:(v + 1) * LANES] for v in range(nv)]
        si = [sl[r0 + p:r0 + 2 * p, v * LANES:(v + 1) * LANES] for v in range(nv)]
        order = list(range(nv)) if d == 0 else list(range(nv - 1, -1, -1))
        ir, ii = {}, {}
        prev = None
        for v in order:
            if prev is None:
                ir[v], ii[v] = sr[v], si[v]
            else:
                mr, mi = _cmul(lam(1), ir[prev], ii[prev])
                ir[v], ii[v] = mr + sr[v], mi + si[v]
            prev = v
        entry = 0 if d == 0 else lg - 1
        src = (CTX_OFF + ctx_chunks) if d == 0 else (CTX_OFF - 1)
        h0r = jnp.zeros((p, LANES), _F32)
        h0i = jnp.zeros((p, LANES), _F32)
        for b in range(nb):
            m = (bidl == b) & (posl == entry)
            col = b * CTX_SEG + src
            h0r = jnp.where(m, hc[2 * d][:, col:col + 1], h0r)
            h0i = jnp.where(m, hc[2 * d + 1][:, col:col + 1], h0i)
        jr, ji = _cmul(lam(nv), h0r, h0i)
        er, ei = _lane_scan(ir[prev] + jr, ii[prev] + ji, lam, nv, posl, lg, d == 1)
        er, ei = er + h0r, ei + h0i
        hr, hi = {order[0]: er}, {order[0]: ei}
        for n, v in enumerate(order[1:], start=1):
            mr, mi = _cmul(lam(n), er, ei)
            hr[v], hi[v] = ir[order[n - 1]] + mr, ii[order[n - 1]] + mi
        hl += [jnp.concatenate([hr[v] for v in range(nv)], axis=1),
               jnp.concatenate([hi[v] for v in range(nv)], axis=1)]
    h = jnp.concatenate(hl, axis=0).astype(_BF16)
    y1 = jnp.dot(cst, h, preferred_element_type=_F32)
    for v in range(nv):
        ytl_ref[q, v] += y1[:, v * LANES:(v + 1) * LANES]
    if ytc_ref is not None:
        hcb = jnp.concatenate(hc, axis=0).astype(_BF16)
        yc = jnp.dot(mt, uc, preferred_element_type=_F32)
        ytc_ref[q] = yc + jnp.dot(cst, hcb, preferred_element_type=_F32)


def _s5_pows(nv, lg):
    pows = set(range(1, nv + 1))
    pows |= {nv << k for k in range(int(math.log2(lg)))}
    pows |= {1 << k for k in range(int(math.log2(CTX_SEG)))}
    return tuple(sorted(pows))


def _s5core(utl, utc, mt, bst, cst, lampow, nb, ctx_chunks, pows, want_yc):
    g, nv, rows, _ = utl.shape
    gq = S5_CORE_GROUPS
    out_shape = [jax.ShapeDtypeStruct((g, nv, rows, LANES), _F32)]
    out_specs = [pl.BlockSpec((gq, nv, rows, LANES), lambda i: (i, 0, 0, 0))]
    if want_yc:
        out_shape.append(jax.ShapeDtypeStruct((g, rows, LANES), _F32))
        out_specs.append(pl.BlockSpec((gq, rows, LANES), lambda i: (i, 0, 0)))
    res = pl.pallas_call(
        functools.partial(_s5core_kernel, nb=nb, ctx_chunks=ctx_chunks, pows=pows),
        out_shape=out_shape,
        grid=(g // gq,),
        in_specs=[pl.BlockSpec((gq, nv, rows, LANES), lambda i: (i, 0, 0, 0)),
                  pl.BlockSpec((gq, rows, LANES), lambda i: (i, 0, 0)),
                  pl.BlockSpec((gq, rows, rows), lambda i: (i, 0, 0)),
                  pl.BlockSpec((gq, S5_NSTATE, rows), lambda i: (i, 0, 0)),
                  pl.BlockSpec((gq, rows, S5_NSTATE), lambda i: (i, 0, 0)),
                  pl.BlockSpec((gq, 4, S5_STATE, LANES), lambda i: (i, 0, 0, 0))],
        out_specs=out_specs,
        compiler_params=_cparams(("arbitrary",)),
        name="s5core_yc" if want_yc else "s5core",
    )(utl, utc, mt, bst, cst, lampow)
    return (res[0], res[1]) if want_yc else (res[0], None)


def _s5prep_kernel(*refs):
    for q in range(refs[0].shape[1]):
        _s5prep_group(q, *refs)


def _s5prep_group(q, acol_ref, arow_ref, ldt_ref, bre_ref, bim_ref, cre_ref, cim_ref, c1_ref, c2_ref,
                  esel_ref, etile_ref, e2_ref, e3_ref, pcol_ref, mt_ref, bst_ref, cst_ref, lp_ref):
    t_len, p = S5_CHUNK, S5_STATE
    lane = lax.broadcasted_iota(jnp.int32, (1, LANES), 1)
    n_lane = lane.astype(_F32)
    npow = pcol_ref.shape[0]
    strips = []
    for d in range(2):
        dt = jnp.exp(ldt_ref[d, q])
        a = acol_ref[d, q]
        ar, ai = a[:, 0:1], a[:, 1:2]
        dar, dai = ar * dt, ai * dt
        pm, pa = jnp.exp(n_lane * dar), n_lane * dai
        lr, li = pm * jnp.cos(pa), pm * jnp.sin(pa)
        lbr, lbi = lr[:, 1:2], li[:, 1:2]
        den = ar * ar + ai * ai
        nr, ni = lbr - 1.0, lbi
        kr = (nr * ar + ni * ai) / den
        ki = (ni * ar - nr * ai) / den
        bbr = kr * bre_ref[d, q] - ki * bim_ref[d, q]
        bbi = kr * bim_ref[d, q] + ki * bre_ref[d, q]
        wr = _select_cols(lr, esel_ref[d])
        wi = _select_cols(li, esel_ref[d])
        btr = _select_cols(bbr, etile_ref[...])
        bti = _select_cols(bbi, etile_ref[...])
        bsr, bsi = wr * btr - wi * bti, wr * bti + wi * btr
        bst_ref[q, (2 * d) * p:(2 * d + 1) * p, :] = bsr.astype(_BF16)
        bst_ref[q, (2 * d + 1) * p:(2 * d + 2) * p, :] = bsi.astype(_BF16)
        strips.append(_dot_split(cre_ref[d, q], bsr) - _dot_split(cim_ref[d, q], bsi))
        w1 = _select_rows(e2_ref[d], jnp.concatenate([lr, li], axis=0).T)
        w2 = _select_rows(e2_ref[d], jnp.concatenate([li, lr], axis=0).T)
        c1 = _select_rows(e3_ref[...], c1_ref[d, q])
        c2 = _select_rows(e3_ref[...], c2_ref[d, q])
        cst_ref[q, :, d * LANES:(d + 1) * LANES] = (c1 * w1 + c2 * w2).astype(_BF16)
        arow = arow_ref[d, q]
        m = pcol_ref[...] * float(t_len)
        qm, qa = jnp.exp(m * (arow[0:1] * dt)), m * (arow[1:2] * dt)
        packed = qm * jnp.where(lane < p, jnp.cos(qa), jnp.sin(qa))
        lpt = jnp.concatenate([packed, jnp.zeros((LANES - npow, LANES), _F32)], axis=0).T
        lp_ref[q, 2 * d] = lpt[:p]
        lp_ref[q, 2 * d + 1] = lpt[p:]
    rf, rb = strips
    lane_w = lax.broadcasted_iota(jnp.int32, (1, S5_ROWS), 1)
    for t in range(t_len):
        sf = (t_len - 1 - t) * S5_GROUP
        f = rf if sf == 0 else pltpu.roll(rf, S5_ROWS - sf, axis=1)
        f = jnp.where(lane_w < (t + 1) * S5_GROUP, f, 0.0)
        sb = t * S5_GROUP
        bk = rb if sb == 0 else pltpu.roll(rb, sb, axis=1)
        bk = jnp.where(lane_w >= sb, bk, 0.0)
        mt_ref[q, t * S5_GROUP:(t + 1) * S5_GROUP, :] = (f + bk).astype(_BF16)


def _s5prep_constants(pows):
    t_len, h = S5_CHUNK, S5_GROUP
    s_of = np.arange(S5_ROWS) // h
    h_of = np.arange(S5_ROWS) % h
    n128 = np.arange(LANES)
    esel = np.stack([n128[:, None] == (t_len - 1 - s_of)[None, :], n128[:, None] == s_of[None, :]])
    etile = np.arange(h)[:, None] == h_of[None, :]
    e2 = np.stack([(s_of + 1)[:, None] == n128[None, :], (t_len - s_of)[:, None] == n128[None, :]])
    e3 = h_of[:, None] == np.arange(h)[None, :]
    pcol = np.zeros((16, 1), np.float32)
    pcol[:len(pows), 0] = pows
    f = lambda m: jnp.asarray(m, dtype=_BF16)
    return f(esel), f(etile), f(e2), f(e3), jnp.asarray(pcol)


def _s5prep(a_re, a_im, log_dt, b_re, b_im, c_re, c_im, pows):
    g, p, h = S5_GROUPS, S5_STATE, S5_GROUP
    acol = jnp.stack([a_re, a_im], axis=-1)
    arow = jnp.stack([jnp.concatenate([a_re, a_re], -1),
                      jnp.concatenate([a_im, a_im], -1)], axis=2)
    ldt = log_dt.reshape(2, g, 1, 1)
    c1 = jnp.concatenate([c_re, -c_re], -1)
    c2 = jnp.concatenate([-c_im, -c_im], -1)
    esel, etile, e2, e3, pcol = _s5prep_constants(pows)

    gq = S5_PREP_GROUPS

    def per_g(*tail):
        return pl.BlockSpec((2, gq) + tail, lambda i: (0, i) + (0,) * len(tail))

    return pl.pallas_call(
        _s5prep_kernel,
        out_shape=[jax.ShapeDtypeStruct((g, S5_ROWS, S5_ROWS), _BF16),
                   jax.ShapeDtypeStruct((g, S5_NSTATE, S5_ROWS), _BF16),
                   jax.ShapeDtypeStruct((g, S5_ROWS, S5_NSTATE), _BF16),
                   jax.ShapeDtypeStruct((g, 4, p, LANES), _F32)],
        grid=(g // gq,),
        in_specs=[per_g(p, 2), per_g(2, LANES), per_g(1, 1), per_g(p, h), per_g(p, h),
                  per_g(h, p), per_g(h, p), per_g(h, LANES), per_g(h, LANES),
                  _resident(esel.shape), _resident(etile.shape), _resident(e2.shape), _resident(e3.shape),
                  _resident(pcol.shape)],
        out_specs=[pl.BlockSpec((gq, S5_ROWS, S5_ROWS), lambda i: (i, 0, 0)),
                   pl.BlockSpec((gq, S5_NSTATE, S5_ROWS), lambda i: (i, 0, 0)),
                   pl.BlockSpec((gq, S5_ROWS, S5_NSTATE), lambda i: (i, 0, 0)),
                   pl.BlockSpec((gq, 4, p, LANES), lambda i: (i, 0, 0, 0))],
        compiler_params=_cparams(("arbitrary",)),
        name="s5prep",
    )(acol, arow, ldt, b_re, b_im, c_re, c_im, c1, c2, esel, etile, e2, e3, pcol)


def _s5post_kernel(x_ref, yt_ref, mod_ref, g_ref, dsk_ref, gw_ref, gb_ref, o_ref, *, k, lanes):
    seg, off, mod_rows = lanes
    nb, lg = x_ref.shape[0], x_ref.shape[1]
    d = D_MODEL
    gn = g_ref[...]
    dsk = dsk_ref[...]
    c = math.sqrt(2.0 / math.pi)
    mods = [_mod_slices(mod_ref[mod_rows[b]])[:3] for b in range(nb)]
    ys = [yt_ref[:, 0, j * S5_GROUP:(j + 1) * S5_GROUP, :].reshape(d, LANES).T for j in range(k)]
    rows = lg * k
    xs, zs = [], []
    for b in range(nb):
        sh1, sc1, _ = mods[b]
        r0 = b * seg + off
        yb = jnp.stack([ys[j][r0:r0 + lg] for j in range(k)], axis=0)
        yb = jnp.swapaxes(yb, 0, 1).reshape(rows, d)
        xf = x_ref[b].reshape(rows, d)
        ms = jnp.mean(xf * xf, axis=-1, keepdims=True)
        pre = (xf * lax.rsqrt(ms + EPS)) * ((gn * (1.0 + sc1)) * dsk) + (yb + sh1 * dsk)
        xs.append(xf)
        zs.append(pre + pre * jnp.tanh(pre * (c + (c * 0.044715) * (pre * pre))))
    z2 = jnp.concatenate(zs, axis=0)
    th = jnp.tanh(jnp.dot(z2.astype(_BF16), gw_ref[0], preferred_element_type=_F32) + gb_ref[...])
    for b in range(nb):
        q = (0.25 * mods[b][2]) * z2[b * rows:(b + 1) * rows]
        o_ref[b] = (xs[b] + (q * th[b * rows:(b + 1) * rows] + q)).reshape(lg, k, d)


def _s5post(x4, yt, mod, g, dsk, gw, layer, gb, lanes):
    b, lg, per, d = x4.shape
    k = S5_POST_TOK
    blocks_per_v = S5_CHUNK // k
    tok = pl.BlockSpec((b, lg, k, d), lambda r: (0, 0, r, 0))
    vec = pl.BlockSpec((1, d), lambda r: (0, 0))
    return pl.pallas_call(
        functools.partial(_s5post_kernel, k=k, lanes=lanes),
        out_shape=jax.ShapeDtypeStruct((b, lg, per, d), _F32),
        grid=(per // k,),
        in_specs=[tok,
                  pl.BlockSpec((S5_GROUPS, 1, k * S5_GROUP, LANES),
                               lambda r: (0, r // blocks_per_v, r % blocks_per_v, 0)),
                  pl.BlockSpec((8, 1, 6 * d), lambda r: (0, 0, 0)),
                  vec, vec, _resident_layer((d, d), layer), vec],
        out_specs=tok,
        compiler_params=_cparams(("arbitrary",)),
        name="s5post",
    )(x4, yt, mod, g, dsk, gw, gb)


def _mlp_kernel(x_ref, mod_ref, g_ref, w1_ref, b1_ref, w2_ref, b2_ref, fg_ref, o_ref, *, final):
    sh2, sc2, g2 = _mod_slices(mod_ref[0])[3:]
    xf = x_ref[0]
    xn = _rms_mod(xf, g_ref[...], sh2, sc2).astype(_BF16)
    acc = jnp.zeros(xf.shape, _F32)
    fc = D_MODEL
    for j in range(D_FF // fc):
        a = jnp.dot(xn, w1_ref[0, :, j * fc:(j + 1) * fc], preferred_element_type=_F32)
        a = jnp.maximum(a + b1_ref[:, j * fc:(j + 1) * fc], 0.0)
        acc = acc + jnp.dot((a * a).astype(_BF16), w2_ref[0, j * fc:(j + 1) * fc, :],
                            preferred_element_type=_F32)
    out = xf + g2 * (acc + b2_ref[...])
    if final:
        ms = jnp.mean(out * out, axis=-1, keepdims=True)
        out = out * lax.rsqrt(ms + EPS) * fg_ref[...]
    o_ref[0] = out


def _mlp(x, mod, mod_row, g, w1, b1, w2, b2, layer, fg, tt, final):
    b, n, d = x.shape
    row = (lambda bi: bi) if mod_row is None else (lambda bi: mod_row)
    tok = pl.BlockSpec((1, tt, d), lambda bi, t: (bi, t, 0))
    vec = pl.BlockSpec((1, d), lambda bi, t: (0, 0))
    return pl.pallas_call(
        functools.partial(_mlp_kernel, final=final),
        out_shape=jax.ShapeDtypeStruct((b, n, d), _F32),
        grid=(b, n // tt),
        in_specs=[tok, pl.BlockSpec((1, 1, 6 * d), lambda bi, t: (row(bi), 0, 0)), vec,
                  _resident_layer((d, D_FF), layer), pl.BlockSpec((1, D_FF), lambda bi, t: (0, 0)),
                  _resident_layer((D_FF, d), layer), vec, vec],
        out_specs=tok,
        compiler_params=_cparams(("arbitrary", "arbitrary")),
        name="mlp_final" if final else "mlp",
    )(x, mod, g, w1, b1, w2, b2, fg)


def _window_in_rows(w):
    need = POOL_TILE_ROWS + w - 1
    return -(-need // POOL_TILE_ROWS) * POOL_TILE_ROWS


def _grid_window_matrices():
    to = np.arange(POOL_TILE_ROWS * GRID_W)
    ro, co = to // GRID_W, to % GRID_W
    mats = []
    for w in POOL_WINDOWS:
        ti = np.arange(_window_in_rows(w) * GRID_W)
        ri, ci = ti // GRID_W - w // 2, ti % GRID_W
        m = ((ri[None, :] >= (ro - w // 2)[:, None]) & (ri[None, :] < (ro + w - w // 2)[:, None])
             & (ci[None, :] >= (co - w // 2)[:, None]) & (ci[None, :] < (co + w - w // 2)[:, None]))
        mats.append(jnp.asarray(m, dtype=_BF16))
    return mats


def _grid_inverse_counts(rows):
    t = np.arange(rows * GRID_W)
    r, c = t // GRID_W, t % GRID_W
    tabs = []
    for w in POOL_WINDOWS:
        rc = np.minimum(r + w - w // 2, rows) - np.maximum(r - w // 2, 0)
        cc = np.minimum(c + w - w // 2, GRID_W) - np.maximum(c - w // 2, 0)
        tabs.append(1.0 / (rc * cc))
    return jnp.asarray(np.stack(tabs), dtype=_F32)


def _seq_window_matrices(n):
    t = np.arange(n)
    mats = []
    for w in POOL_WINDOWS:
        lo, hi = np.maximum(t - w // 2, 0), np.minimum(t + w - w // 2, n)
        mats.append((t[None, :] >= lo[:, None]) & (t[None, :] < hi[:, None]))
    return jnp.asarray(np.stack(mats), dtype=_BF16)


POOL_BLOCK_ROWS = 16
POOL_HALO_ROWS = 8
POOL_TILE_ROWS = 4


def _pool_kernel(xm_ref, xb_ref, mod_ref, g_ref, a2_ref, a4_ref, a8_ref, a16_ref, pw_ref, ps_ref, inv_ref,
                 o_ref, tail):
    i = pl.program_id(0)
    b = pl.program_id(1)
    nblk = pl.num_programs(0)
    sh1, sc1, g1 = _mod_slices(mod_ref[0])[:3]
    gn = g_ref[...]
    halo = POOL_HALO_ROWS * GRID_W
    main = POOL_BLOCK_ROWS * GRID_W
    tile = POOL_TILE_ROWS * GRID_W
    xm = xm_ref[0]
    xn_m = _rms_mod(xm, gn, sh1, sc1)
    xn_mb = xn_m.astype(_BF16)
    @pl.when(i == 0)
    def _():
        tail[b] = jnp.zeros((halo, D_MODEL), _BF16)

    xn_t = tail[b]
    xn_b = jnp.where(i < nblk - 1, _rms_mod(xb_ref[0], gn, sh1, sc1), 0.0).astype(_BF16)
    tail[b] = xn_mb[main - halo:]
    xn_ext = jnp.concatenate([xn_t, xn_mb, xn_b], axis=0)
    for gi, (w, a_ref) in enumerate(zip(POOL_WINDOWS, (a2_ref, a4_ref, a8_ref, a16_ref))):
        ch = slice(gi * POOL_CH, (gi + 1) * POOL_CH)
        a = a_ref[...]
        span = a.shape[1]
        tots = []
        for k in range(main // tile):
            start = halo + k * tile - (w // 2) * GRID_W
            tots.append(jnp.dot(a, xn_ext[start:start + span, ch], preferred_element_type=_F32))
        tot = jnp.concatenate(tots, axis=0)
        inv = inv_ref[gi]
        p = tot * jnp.concatenate([inv] * (POOL_CH // LANES), axis=1) - xn_m[:, ch]
        y = jnp.dot(p.astype(_BF16), pw_ref[0, gi], preferred_element_type=_F32) * ps_ref[:, ch]
        o_ref[0, :, ch] = xm[:, ch] + g1[:, ch] * y


def _pool_grid(x, mod, g, pw, layer, ps):
    b, n, d = x.shape
    rows = n // GRID_W
    main = POOL_BLOCK_ROWS * GRID_W
    halo = POOL_HALO_ROWS * GRID_W
    nblk = n // main
    ratio = main // halo
    nh = n // halo
    amats = _grid_window_matrices()
    assert all(w // 2 <= POOL_HALO_ROWS and _window_in_rows(w) - w // 2 <= POOL_TILE_ROWS + POOL_HALO_ROWS
               for w in POOL_WINDOWS)
    inv = jnp.broadcast_to(_grid_inverse_counts(rows)[:, :, None], (len(POOL_WINDOWS), n, LANES))
    vec = pl.BlockSpec((1, d), lambda i, bi: (0, 0))
    return pl.pallas_call(
        _pool_kernel,
        out_shape=jax.ShapeDtypeStruct((b, n, d), _F32),
        grid=(nblk, b),
        in_specs=[pl.BlockSpec((1, main, d), lambda i, bi: (bi, i, 0)),
                  pl.BlockSpec((1, halo, d), lambda i, bi: (bi, jnp.minimum((i + 1) * ratio, nh - 1), 0)),
                  pl.BlockSpec((1, 1, 6 * d), lambda i, bi: (bi, 0, 0)),
                  vec,
                  *[_resident(a.shape) for a in amats],
                  _resident_layer((len(POOL_WINDOWS), POOL_CH, POOL_CH), layer),
                  vec,
                  pl.BlockSpec((len(POOL_WINDOWS), main, LANES), lambda i, bi: (0, i, 0))],
        out_specs=pl.BlockSpec((1, main, d), lambda i, bi: (bi, i, 0)),
        scratch_shapes=[pltpu.VMEM((b, halo, d), _BF16)],
        compiler_params=_cparams(("arbitrary", "arbitrary")),
        name="pool_grid",
    )(x, x, mod, g, *amats, pw, ps, inv)


def _poolseq_kernel(x_ref, mod_ref, g_ref, aseq_ref, pw_ref, ps_ref, o_ref):
    sh1, sc1, g1 = _mod_slices(mod_ref[0])[:3]
    xf = x_ref[0]
    n = xf.shape[0]
    xn = _rms_mod(xf, g_ref[...], sh1, sc1)
    xb = xn.astype(_BF16)
    t = lax.broadcasted_iota(jnp.int32, (n, POOL_CH), 0)
    for gi, w in enumerate(POOL_WINDOWS):
        ch = slice(gi * POOL_CH, (gi + 1) * POOL_CH)
        tot = jnp.dot(aseq_ref[gi], xb[:, ch], preferred_element_type=_F32)
        cnt = jnp.minimum(t + (w - w // 2), n) - jnp.maximum(t - w // 2, 0)
        p = tot / cnt.astype(_F32) - xn[:, ch]
        y = jnp.dot(p.astype(_BF16), pw_ref[0, gi], preferred_element_type=_F32) * ps_ref[:, ch]
        o_ref[0, :, ch] = xf[:, ch] + g1[:, ch] * y


def _pool_seq(x, mod, mod_row, g, pw, layer, ps):
    b, n, d = x.shape
    aseq = _seq_window_matrices(n)
    vec = pl.BlockSpec((1, d), lambda bi: (0, 0))
    return pl.pallas_call(
        _poolseq_kernel,
        out_shape=jax.ShapeDtypeStruct((b, n, d), _F32),
        grid=(b,),
        in_specs=[pl.BlockSpec((1, n, d), lambda bi: (bi, 0, 0)),
                  pl.BlockSpec((1, 1, 6 * d), lambda bi: (mod_row, 0, 0)),
                  vec,
                  _resident((len(POOL_WINDOWS), n, n)),
                  _resident_layer((len(POOL_WINDOWS), POOL_CH, POOL_CH), layer),
                  vec],
        out_specs=pl.BlockSpec((1, n, d), lambda bi: (bi, 0, 0)),
        compiler_params=_cparams(("arbitrary",)),
        name="pool_seq",
    )(x, mod, g, aseq, pw, ps)


def kernel(x, c, ctx, c_ctx, ada_w, ada_b, norm1_g, norm2_g, s5_a_re, s5_a_im, s5_log_dt, s5_b_re, s5_b_im, s5_c_re, s5_c_im, s5_d, s5_glu_w, s5_glu_b, pool_w, pool_scale, mlp_w1, mlp_b1, mlp_w2, mlp_b2, final_g):
    bsz, n_tok, d = x.shape
    n_ctx = ctx.shape[1]
    depth = ada_w.shape[0]
    assert d == D_MODEL and n_tok % (POOL_BLOCK_ROWS * GRID_W) == 0
    assert LANES % bsz == 0 and bsz * CTX_SEG <= LANES and bsz < 8
    lat_chunks = n_tok // S5_CHUNK
    ctx_chunks = n_ctx // S5_CHUNK
    assert n_tok % S5_CHUNK == 0 and n_ctx % S5_CHUNK == 0 and (bsz * lat_chunks) % LANES == 0
    assert CTX_OFF >= 2 and CTX_OFF + ctx_chunks + 2 <= CTX_SEG
    nv = bsz * lat_chunks // LANES
    lg = LANES // bsz
    pows = _s5_pows(nv, lg)
    lat_lanes = (lg, 0, tuple(range(bsz)))
    ctx_lanes = (CTX_SEG, CTX_OFF, (bsz,) * bsz)
    last_ctx_reader = ((depth - 1) // N_MIXERS) * N_MIXERS
    ctx_row = bsz
    tt = min(512, n_tok)
    tt_ctx = min(512, n_ctx)

    cc = jnp.zeros((8, d), _F32).at[:bsz].set(c).at[ctx_row].set(c_ctx)
    mods = _modulation(cc, ada_w, ada_b)
    w1 = mlp_w1.astype(_BF16)
    w2 = mlp_w2.astype(_BF16)
    glu_w4 = (0.25 * s5_glu_w).astype(_BF16)
    pw = pool_w.astype(_BF16)

    h_ctx = ctx
    for i in range(depth):
        ctx_in = i <= last_ctx_reader
        ctx_out = i < last_ctx_reader
        j = i // N_MIXERS
        mod = mods[i].reshape(8, 1, 6 * d)
        g1n = norm1_g[i].reshape(1, d)
        g2n = norm2_g[i].reshape(1, d)
        if i % N_MIXERS == 0:
            mt, bst, cst, lampow = _s5prep(s5_a_re[j], s5_a_im[j], s5_log_dt[j], s5_b_re[j], s5_b_im[j],
                                           s5_c_re[j], s5_c_im[j], pows)
            dsk = s5_d[j].reshape(1, d)
            gb2 = 0.5 * s5_glu_b[j].reshape(1, d)
            c4 = h_ctx.reshape(bsz, ctx_chunks, S5_CHUNK, d)
            if ctx_in:
                utc = _s5pre(c4, mod, g1n, ctx_lanes).reshape(S5_GROUPS, S5_ROWS, LANES)
            else:
                utc = jnp.zeros((S5_GROUPS, S5_ROWS, LANES), _BF16)
            x4 = x.reshape(bsz, lg, n_tok // lg, d)
            utl = _s5pre(x4, mod, g1n, lat_lanes)
            ytl, ytc = _s5core(utl, utc, mt, bst, cst, lampow, bsz, ctx_chunks, pows, ctx_out)
            x = _s5post(x4, ytl, mod, g1n, dsk, glu_w4, j, gb2, lat_lanes).reshape(bsz, n_tok, d)
            if ctx_out:
                ytc4 = ytc.reshape(S5_GROUPS, 1, S5_ROWS, LANES)
                h_ctx = _s5post(c4, ytc4, mod, g1n, dsk, glu_w4, j, gb2, ctx_lanes).reshape(bsz, n_ctx, d)
        else:
            ps = pool_scale[j].reshape(1, d)
            x = _pool_grid(x, mod, g1n, pw, j, ps)
            if ctx_out:
                h_ctx = _pool_seq(h_ctx, mod, ctx_row, g1n, pw, j, ps)
        final = i == depth - 1
        x = _mlp(x, mod, None, g2n, w1, mlp_b1[i].reshape(1, D_FF), w2, mlp_b2[i].reshape(1, d), i,
                 final_g.reshape(1, d), tt, final)
        if ctx_out:
            h_ctx = _mlp(h_ctx, mod, ctx_row, g2n, w1, mlp_b1[i].reshape(1, D_FF), w2,
                         mlp_b2[i].reshape(1, d), i, final_g.reshape(1, d), tt_ctx, False)
    return x
```

```python
import functools
import math

import numpy as np
import jax
import jax.numpy as jnp
from jax import lax
from jax.experimental import pallas as pl
from jax.experimental.pallas import tpu as pltpu

D_MODEL = 1024
GRID_W = 64
S5_GROUP = 16
S5_GROUPS = D_MODEL // S5_GROUP
S5_STATE = 64
POOL_WINDOWS = (2, 4, 8, 16)
POOL_CH = D_MODEL // len(POOL_WINDOWS)
D_FF = 4 * D_MODEL
N_MIXERS = 2
EPS = 1e-6
LANES = 128

S5_CHUNK = 32
S5_ROWS = S5_CHUNK * S5_GROUP
S5_NSTATE = 4 * S5_STATE
CTX_SEG = 32
CTX_OFF = 8
S5_PRE_TOK = 16
S5_POST_TOK = 8
S5_CORE_GROUPS = 4
S5_PREP_GROUPS = 8
V7X_VMEM_LIMIT_BYTES = 56 * 1024 * 1024

_F32 = jnp.float32
_BF16 = jnp.bfloat16


def _cparams(sem):
    return pltpu.CompilerParams(dimension_semantics=sem, vmem_limit_bytes=V7X_VMEM_LIMIT_BYTES)


def _resident(shape):
    nd = len(shape)
    return pl.BlockSpec(shape, lambda *_: (0,) * nd, pipeline_mode=pl.Buffered(1))


def _resident_layer(shape, layer):
    nd = len(shape)
    return pl.BlockSpec((1,) + tuple(shape), lambda *_: (layer,) + (0,) * nd, pipeline_mode=pl.Buffered(1))


def _mod_kernel(cc_ref, w_ref, b_ref, o_ref):
    c = cc_ref[...]
    o_ref[0] = _dot_split(c * jax.nn.sigmoid(c), w_ref[0]) + b_ref[0]


def _modulation(cc, ada_w, ada_b):
    depth, d, n = ada_w.shape
    tn = 1536
    return pl.pallas_call(
        _mod_kernel,
        out_shape=jax.ShapeDtypeStruct((depth, 8, n), _F32),
        grid=(depth, n // tn),
        in_specs=[pl.BlockSpec((8, d), lambda i, j: (0, 0)),
                  pl.BlockSpec((1, d, tn), lambda i, j: (i, 0, j)),
                  pl.BlockSpec((1, 1, tn), lambda i, j: (i, 0, j))],
        out_specs=pl.BlockSpec((1, 8, tn), lambda i, j: (i, 0, j)),
        compiler_params=_cparams(("arbitrary", "arbitrary")),
        name="modulation",
    )(cc, ada_w, ada_b.reshape(depth, 1, n))


def _rms_mod(xf, g, shift, scale):
    ms = jnp.mean(xf * xf, axis=-1, keepdims=True)
    return (xf * lax.rsqrt(ms + EPS)) * (g * (1.0 + scale)) + shift


def _mod_slices(m):
    d = D_MODEL
    return tuple(m[:, k * d:(k + 1) * d] for k in range(6))


def _cmul(a, yr, yi):
    ar, ai = a
    return ar * yr - ai * yi, ar * yi + ai * yr


def _bf16_pair(x):
    hi = x.astype(_BF16)
    lo = (x - hi.astype(_F32)).astype(_BF16)
    return hi, lo


def _select_cols(x, sel):
    return sum(jnp.dot(part, sel, preferred_element_type=_F32) for part in _bf16_pair(x))


def _select_rows(sel, x):
    return sum(jnp.dot(sel, part, preferred_element_type=_F32) for part in _bf16_pair(x))


def _dot_split(a, b):
    (ah, al), (bh, bl) = _bf16_pair(a), _bf16_pair(b)
    return (jnp.dot(ah, bh, preferred_element_type=_F32) + jnp.dot(ah, bl, preferred_element_type=_F32)
            + jnp.dot(al, bh, preferred_element_type=_F32))


def _s5pre_kernel(x_ref, mod_ref, g_ref, ut_ref, *, k, lanes):
    seg, off, mod_rows = lanes
    nb, cnt = x_ref.shape[0], x_ref.shape[1]
    gn = g_ref[...]
    mods = [_mod_slices(mod_ref[mod_rows[b]])[:2] for b in range(nb)]
    xs = [jnp.swapaxes(x_ref[b], 0, 1) for b in range(nb)]
    pads = (off, seg - off - cnt)
    for j in range(k):
        parts = []
        for b in range(nb):
            piece = [_rms_mod(xs[b][j], gn, *mods[b])]
            if pads[0]:
                piece.insert(0, jnp.zeros((pads[0], D_MODEL), _F32))
            if pads[1]:
                piece.append(jnp.zeros((pads[1], D_MODEL), _F32))
            parts += piece
        if LANES - nb * seg:
            parts.append(jnp.zeros((LANES - nb * seg, D_MODEL), _F32))
        xn = jnp.concatenate(parts, axis=0)
        ut_ref[:, 0, j * S5_GROUP:(j + 1) * S5_GROUP, :] = (
            xn.T.astype(_BF16).reshape(S5_GROUPS, S5_GROUP, LANES))


def _s5pre(x4, mod, g, lanes):
    b, cnt, per, d = x4.shape
    k = S5_PRE_TOK
    blocks_per_v = S5_CHUNK // k
    return pl.pallas_call(
        functools.partial(_s5pre_kernel, k=k, lanes=lanes),
        out_shape=jax.ShapeDtypeStruct((S5_GROUPS, per // S5_CHUNK, S5_ROWS, LANES), _BF16),
        grid=(per // k,),
        in_specs=[pl.BlockSpec((b, cnt, k, d), lambda r: (0, 0, r, 0)),
                  pl.BlockSpec((8, 1, 6 * d), lambda r: (0, 0, 0)),
                  pl.BlockSpec((1, d), lambda r: (0, 0))],
        out_specs=pl.BlockSpec((S5_GROUPS, 1, k * S5_GROUP, LANES),
                               lambda r: (0, r // blocks_per_v, r % blocks_per_v, 0)),
        compiler_params=_cparams(("arbitrary",)),
        name="s5pre",
    )(x4, mod, g)


def _lane_shift(v, dist, pos, seg, reverse):
    n = v.shape[1]
    if reverse:
        return jnp.where(pos < seg - dist, pltpu.roll(v, n - dist, axis=1), 0.0)
    return jnp.where(pos >= dist, pltpu.roll(v, dist, axis=1), 0.0)


def _lane_scan(xr, xi, lam, unit, pos, seg, reverse):
    k = 0
    while (1 << k) < seg:
        dist = 1 << k
        mr, mi = _cmul(lam(unit * dist), _lane_shift(xr, dist, pos, seg, reverse),
                       _lane_shift(xi, dist, pos, seg, reverse))
        xr, xi = xr + mr, xi + mi
        k += 1
    return _lane_shift(xr, 1, pos, seg, reverse), _lane_shift(xi, 1, pos, seg, reverse)


def _s5core_kernel(utl_ref, utc_ref, mt_ref, bst_ref, cst_ref, lp_ref, ytl_ref, *ytc_out,
                   nb, ctx_chunks, pows):
    for q in range(utl_ref.shape[0]):
        _s5core_group(q, utl_ref, utc_ref, mt_ref, bst_ref, cst_ref, lp_ref, ytl_ref,
                      ytc_out[0] if ytc_out else None, nb, ctx_chunks, pows)


def _s5core_group(q, utl_ref, utc_ref, mt_ref, bst_ref, cst_ref, lp_ref, ytl_ref, ytc_ref,
                  nb, ctx_chunks, pows):
    p = S5_STATE
    nv = utl_ref.shape[1]
    ul = jnp.concatenate([utl_ref[q, v] for v in range(nv)], axis=1)
    uc = utc_ref[q]
    lg = LANES // nb
    bst, cst, mt = bst_ref[q], cst_ref[q], mt_ref[q]
    lp = lp_ref[q]

    def lam_of(d):
        def lam(n):
            k = pows.index(n)
            return lp[2 * d][:, k:k + 1], lp[2 * d + 1][:, k:k + 1]
        return lam

    lane = lax.broadcasted_iota(jnp.int32, (1, LANES), 1)
    sc = jnp.dot(bst, uc, preferred_element_type=_F32)
    sl = jnp.dot(bst, ul, preferred_element_type=_F32)
    y0 = jnp.dot(mt, ul, preferred_element_type=_F32)
    for v in range(nv):
        ytl_ref[q, v] = y0[:, v * LANES:(v + 1) * LANES]
    posc = lane & (CTX_SEG - 1)
    hc = []
    for d in range(2):
        r0 = 2 * p * d
        hr, hi = _lane_scan(sc[r0:r0 + p], sc[r0 + p:r0 + 2 * p], lam_of(d), 1, posc, CTX_SEG, d == 1)
        hc += [hr, hi]
    posl = lane & (lg - 1)
    bidl = lane >> int(math.log2(lg))
    hl = []
    for d in range(2):
        lam = lam_of(d)
        r0 = 2 * p * d
        sr = [sl[r0:r0 + p, v * LANES:(v + 1) * LANES] for v in range(nv)]
        si = [sl[r0 + p:r0 + 2 * p, v * LANES:(v + 1) * LANES] for v in range(nv)]
        order = list(range(nv)) if d == 0 else list(range(nv - 1, -1, -1))
        ir, ii = {}, {}
        prev = None
        for v in order:
            if prev is None:
                ir[v], ii[v] = sr[v], si[v]
            else:
                mr, mi = _cmul(lam(1), ir[prev], ii[prev])
                ir[v], ii[v] = mr + sr[v], mi + si[v]
            prev = v
        entry = 0 if d == 0 else lg - 1
        src = (CTX_OFF + ctx_chunks) if d == 0 else (CTX_OFF - 1)
        h0r = jnp.zeros((p, LANES), _F32)
        h0i = jnp.zeros((p, LANES), _F32)
        for b in range(nb):
            m = (bidl == b) & (posl == entry)
            col = b * CTX_SEG + src
            h0r = jnp.where(m, hc[2 * d][:, col:col + 1], h0r)
            h0i = jnp.where(m, hc[2 * d + 1][:, col:col + 1], h0i)
        jr, ji = _cmul(lam(nv), h0r, h0i)
        er, ei = _lane_scan(ir[prev] + jr, ii[prev] + ji, lam, nv, posl, lg, d == 1)
        er, ei = er + h0r, ei + h0i
        hr, hi = {order[0]: er}, {order[0]: ei}
        for n, v in enumerate(order[1:], start=1):
            mr, mi = _cmul(lam(n), er, ei)
            hr[v], hi[v] = ir[order[n - 1]] + mr, ii[order[n - 1]] + mi
        hl += [jnp.concatenate([hr[v] for v in range(nv)], axis=1),
               jnp.concatenate([hi[v] for v in range(nv)], axis=1)]
    h = jnp.concatenate(hl, axis=0).astype(_BF16)
    y1 = jnp.dot(cst, h, preferred_element_type=_F32)
    for v in range(nv):
        ytl_ref[q, v] += y1[:, v * LANES:(v + 1) * LANES]
    if ytc_ref is not None:
        hcb = jnp.concatenate(hc, axis=0).astype(_BF16)
        yc = jnp.dot(mt, uc, preferred_element_type=_F32)
        ytc_ref[q] = yc + jnp.dot(cst, hcb, preferred_element_type=_F32)


def _s5_pows(nv, lg):
    pows = set(range(1, nv + 1))
    pows |= {nv << k for k in range(int(math.log2(lg)))}
    pows |= {1 << k for k in range(int(math.log2(CTX_SEG)))}
    return tuple(sorted(pows))


def _s5core(utl, utc, mt, bst, cst, lampow, nb, ctx_chunks, pows, want_yc):
    g, nv, rows, _ = utl.shape
    gq = S5_CORE_GROUPS
    out_shape = [jax.ShapeDtypeStruct((g, nv, rows, LANES), _F32)]
    out_specs = [pl.BlockSpec((gq, nv, rows, LANES), lambda i: (i, 0, 0, 0))]
    if want_yc:
        out_shape.append(jax.ShapeDtypeStruct((g, rows, LANES), _F32))
        out_specs.append(pl.BlockSpec((gq, rows, LANES), lambda i: (i, 0, 0)))
    res = pl.pallas_call(
        functools.partial(_s5core_kernel, nb=nb, ctx_chunks=ctx_chunks, pows=pows),
        out_shape=out_shape,
        grid=(g // gq,),
        in_specs=[pl.BlockSpec((gq, nv, rows, LANES), lambda i: (i, 0, 0, 0)),
                  pl.BlockSpec((gq, rows, LANES), lambda i: (i, 0, 0)),
                  pl.BlockSpec((gq, rows, rows), lambda i: (i, 0, 0)),
                  pl.BlockSpec((gq, S5_NSTATE, rows), lambda i: (i, 0, 0)),
                  pl.BlockSpec((gq, rows, S5_NSTATE), lambda i: (i, 0, 0)),
                  pl.BlockSpec((gq, 4, S5_STATE, LANES), lambda i: (i, 0, 0, 0))],
        out_specs=out_specs,
        compiler_params=_cparams(("arbitrary",)),
        name="s5core_yc" if want_yc else "s5core",
    )(utl, utc, mt, bst, cst, lampow)
    return (res[0], res[1]) if want_yc else (res[0], None)


def _s5prep_kernel(*refs):
    for q in range(refs[0].shape[1]):
        _s5prep_group(q, *refs)


def _s5prep_group(q, acol_ref, arow_ref, ldt_ref, bre_ref, bim_ref, cre_ref, cim_ref, c1_ref, c2_ref,
                  esel_ref, etile_ref, e2_ref, e3_ref, pcol_ref, mt_ref, bst_ref, cst_ref, lp_ref):
    t_len, p = S5_CHUNK, S5_STATE
    lane = lax.broadcasted_iota(jnp.int32, (1, LANES), 1)
    n_lane = lane.astype(_F32)
    npow = pcol_ref.shape[0]
    strips = []
    for d in range(2):
        dt = jnp.exp(ldt_ref[d, q])
        a = acol_ref[d, q]
        ar, ai = a[:, 0:1], a[:, 1:2]
        dar, dai = ar * dt, ai * dt
        pm, pa = jnp.exp(n_lane * dar), n_lane * dai
        lr, li = pm * jnp.cos(pa), pm * jnp.sin(pa)
        lbr, lbi = lr[:, 1:2], li[:, 1:2]
        den = ar * ar + ai * ai
        nr, ni = lbr - 1.0, lbi
        kr = (nr * ar + ni * ai) / den
        ki = (ni * ar - nr * ai) / den
        bbr = kr * bre_ref[d, q] - ki * bim_ref[d, q]
        bbi = kr * bim_ref[d, q] + ki * bre_ref[d, q]
        wr = _select_cols(lr, esel_ref[d])
        wi = _select_cols(li, esel_ref[d])
        btr = _select_cols(bbr, etile_ref[...])
        bti = _select_cols(bbi, etile_ref[...])
        bsr, bsi = wr * btr - wi * bti, wr * bti + wi * btr
        bst_ref[q, (2 * d) * p:(2 * d + 1) * p, :] = bsr.astype(_BF16)
        bst_ref[q, (2 * d + 1) * p:(2 * d + 2) * p, :] = bsi.astype(_BF16)
        strips.append(_dot_split(cre_ref[d, q], bsr) - _dot_split(cim_ref[d, q], bsi))
        w1 = _select_rows(e2_ref[d], jnp.concatenate([lr, li], axis=0).T)
        w2 = _select_rows(e2_ref[d], jnp.concatenate([li, lr], axis=0).T)
        c1 = _select_rows(e3_ref[...], c1_ref[d, q])
        c2 = _select_rows(e3_ref[...], c2_ref[d, q])
        cst_ref[q, :, d * LANES:(d + 1) * LANES] = (c1 * w1 + c2 * w2).astype(_BF16)
        arow = arow_ref[d, q]
        m = pcol_ref[...] * float(t_len)
        qm, qa = jnp.exp(m * (arow[0:1] * dt)), m * (arow[1:2] * dt)
        packed = qm * jnp.where(lane < p, jnp.cos(qa), jnp.sin(qa))
        lpt = jnp.concatenate([packed, jnp.zeros((LANES - npow, LANES), _F32)], axis=0).T
        lp_ref[q, 2 * d] = lpt[:p]
        lp_ref[q, 2 * d + 1] = lpt[p:]
    rf, rb = strips
    lane_w = lax.broadcasted_iota(jnp.int32, (1, S5_ROWS), 1)
    for t in range(t_len):
        sf = (t_len - 1 - t) * S5_GROUP
        f = rf if sf == 0 else pltpu.roll(rf, S5_ROWS - sf, axis=1)
        f = jnp.where(lane_w < (t + 1) * S5_GROUP, f, 0.0)
        sb = t * S5_GROUP
        bk = rb if sb == 0 else pltpu.roll(rb, sb, axis=1)
        bk = jnp.where(lane_w >= sb, bk, 0.0)
        mt_ref[q, t * S5_GROUP:(t + 1) * S5_GROUP, :] = (f + bk).astype(_BF16)


def _s5prep_constants(pows):
    t_len, h = S5_CHUNK, S5_GROUP
    s_of = np.arange(S5_ROWS) // h
    h_of = np.arange(S5_ROWS) % h
    n128 = np.arange(LANES)
    esel = np.stack([n128[:, None] == (t_len - 1 - s_of)[None, :], n128[:, None] == s_of[None, :]])
    etile = np.arange(h)[:, None] == h_of[None, :]
    e2 = np.stack([(s_of + 1)[:, None] == n128[None, :], (t_len - s_of)[:, None] == n128[None, :]])
    e3 = h_of[:, None] == np.arange(h)[None, :]
    pcol = np.zeros((16, 1), np.float32)
    pcol[:len(pows), 0] = pows
    f = lambda m: jnp.asarray(m, dtype=_BF16)
    return f(esel), f(etile), f(e2), f(e3), jnp.asarray(pcol)


def _s5prep(a_re, a_im, log_dt, b_re, b_im, c_re, c_im, pows):
    g, p, h = S5_GROUPS, S5_STATE, S5_GROUP
    acol = jnp.stack([a_re, a_im], axis=-1)
    arow = jnp.stack([jnp.concatenate([a_re, a_re], -1),
                      jnp.concatenate([a_im, a_im], -1)], axis=2)
    ldt = log_dt.reshape(2, g, 1, 1)
    c1 = jnp.concatenate([c_re, -c_re], -1)
    c2 = jnp.concatenate([-c_im, -c_im], -1)
    esel, etile, e2, e3, pcol = _s5prep_constants(pows)

    gq = S5_PREP_GROUPS

    def per_g(*tail):
        return pl.BlockSpec((2, gq) + tail, lambda i: (0, i) + (0,) * len(tail))

    return pl.pallas_call(
        _s5prep_kernel,
        out_shape=[jax.ShapeDtypeStruct((g, S5_ROWS, S5_ROWS), _BF16),
                   jax.ShapeDtypeStruct((g, S5_NSTATE, S5_ROWS), _BF16),
                   jax.ShapeDtypeStruct((g, S5_ROWS, S5_NSTATE), _BF16),
                   jax.ShapeDtypeStruct((g, 4, p, LANES), _F32)],
        grid=(g // gq,),
        in_specs=[per_g(p, 2), per_g(2, LANES), per_g(1, 1), per_g(p, h), per_g(p, h),
                  per_g(h, p), per_g(h, p), per_g(h, LANES), per_g(h, LANES),
                  _resident(esel.shape), _resident(etile.shape), _resident(e2.shape), _resident(e3.shape),
                  _resident(pcol.shape)],
        out_specs=[pl.BlockSpec((gq, S5_ROWS, S5_ROWS), lambda i: (i, 0, 0)),
                   pl.BlockSpec((gq, S5_NSTATE, S5_ROWS), lambda i: (i, 0, 0)),
                   pl.BlockSpec((gq, S5_ROWS, S5_NSTATE), lambda i: (i, 0, 0)),
                   pl.BlockSpec((gq, 4, p, LANES), lambda i: (i, 0, 0, 0))],
        compiler_params=_cparams(("arbitrary",)),
        name="s5prep",
    )(acol, arow, ldt, b_re, b_im, c_re, c_im, c1, c2, esel, etile, e2, e3, pcol)


def _s5post_kernel(x_ref, yt_ref, mod_ref, g_ref, dsk_ref, gw_ref, gb_ref, o_ref, *, k, lanes):
    seg, off, mod_rows = lanes
    nb, lg = x_ref.shape[0], x_ref.shape[1]
    d = D_MODEL
    gn = g_ref[...]
    dsk = dsk_ref[...]
    c = math.sqrt(2.0 / math.pi)
    mods = [_mod_slices(mod_ref[mod_rows[b]])[:3] for b in range(nb)]
    ys = [yt_ref[:, 0, j * S5_GROUP:(j + 1) * S5_GROUP, :].reshape(d, LANES).T for j in range(k)]
    rows = lg * k
    xs, zs = [], []
    for b in range(nb):
        sh1, sc1, _ = mods[b]
        r0 = b * seg + off
        yb = jnp.stack([ys[j][r0:r0 + lg] for j in range(k)], axis=0)
        yb = jnp.swapaxes(yb, 0, 1).reshape(rows, d)
        xf = x_ref[b].reshape(rows, d)
        ms = jnp.mean(xf * xf, axis=-1, keepdims=True)
        pre = (xf * lax.rsqrt(ms + EPS)) * ((gn * (1.0 + sc1)) * dsk) + (yb + sh1 * dsk)
        xs.append(xf)
        zs.append(pre + pre * jnp.tanh(pre * (c + (c * 0.044715) * (pre * pre))))
    z2 = jnp.concatenate(zs, axis=0)
    th = jnp.tanh(jnp.dot(z2.astype(_BF16), gw_ref[0], preferred_element_type=_F32) + gb_ref[...])
    for b in range(nb):
        q = (0.25 * mods[b][2]) * z2[b * rows:(b + 1) * rows]
        o_ref[b] = (xs[b] + (q * th[b * rows:(b + 1) * rows] + q)).reshape(lg, k, d)


def _s5post(x4, yt, mod, g, dsk, gw, layer, gb, lanes):
    b, lg, per, d = x4.shape
    k = S5_POST_TOK
    blocks_per_v = S5_CHUNK // k
    tok = pl.BlockSpec((b, lg, k, d), lambda r: (0, 0, r, 0))
    vec = pl.BlockSpec((1, d), lambda r: (0, 0))
    return pl.pallas_call(
        functools.partial(_s5post_kernel, k=k, lanes=lanes),
        out_shape=jax.ShapeDtypeStruct((b, lg, per, d), _F32),
        grid=(per // k,),
        in_specs=[tok,
                  pl.BlockSpec((S5_GROUPS, 1, k * S5_GROUP, LANES),
                               lambda r: (0, r // blocks_per_v, r % blocks_per_v, 0)),
                  pl.BlockSpec((8, 1, 6 * d), lambda r: (0, 0, 0)),
                  vec, vec, _resident_layer((d, d), layer), vec],
        out_specs=tok,
        compiler_params=_cparams(("arbitrary",)),
        name="s5post",
    )(x4, yt, mod, g, dsk, gw, gb)


def _mlp_kernel(x_ref, mod_ref, g_ref, w1_ref, b1_ref, w2_ref, b2_ref, fg_ref, o_ref, *, final):
    sh2, sc2, g2 = _mod_slices(mod_ref[0])[3:]
    xf = x_ref[0]
    xn = _rms_mod(xf, g_ref[...], sh2, sc2).astype(_BF16)
    acc = jnp.zeros(xf.shape, _F32)
    fc = D_MODEL
    for j in range(D_FF // fc):
        a = jnp.dot(xn, w1_ref[0, :, j * fc:(j + 1) * fc], preferred_element_type=_F32)
        a = jnp.maximum(a + b1_ref[:, j * fc:(j + 1) * fc], 0.0)
        acc = acc + jnp.dot((a * a).astype(_BF16), w2_ref[0, j * fc:(j + 1) * fc, :],
                            preferred_element_type=_F32)
    out = xf + g2 * (acc + b2_ref[...])
    if final:
        ms = jnp.mean(out * out, axis=-1, keepdims=True)
        out = out * lax.rsqrt(ms + EPS) * fg_ref[...]
    o_ref[0] = out


def _mlp(x, mod, mod_row, g, w1, b1, w2, b2, layer, fg, tt, final):
    b, n, d = x.shape
    row = (lambda bi: bi) if mod_row is None else (lambda bi: mod_row)
    tok = pl.BlockSpec((1, tt, d), lambda bi, t: (bi, t, 0))
    vec = pl.BlockSpec((1, d), lambda bi, t: (0, 0))
    return pl.pallas_call(
        functools.partial(_mlp_kernel, final=final),
        out_shape=jax.ShapeDtypeStruct((b, n, d), _F32),
        grid=(b, n // tt),
        in_specs=[tok, pl.BlockSpec((1, 1, 6 * d), lambda bi, t: (row(bi), 0, 0)), vec,
                  _resident_layer((d, D_FF), layer), pl.BlockSpec((1, D_FF), lambda bi, t: (0, 0)),
                  _resident_layer((D_FF, d), layer), vec, vec],
        out_specs=tok,
        compiler_params=_cparams(("arbitrary", "arbitrary")),
        name="mlp_final" if final else "mlp",
    )(x, mod, g, w1, b1, w2, b2, fg)


def _window_in_rows(w):
    need = POOL_TILE_ROWS + w - 1
    return -(-need // POOL_TILE_ROWS) * POOL_TILE_ROWS


def _grid_window_matrices():
    to = np.arange(POOL_TILE_ROWS * GRID_W)
    ro, co = to // GRID_W, to % GRID_W
    mats = []
    for w in POOL_WINDOWS:
        ti = np.arange(_window_in_rows(w) * GRID_W)
        ri, ci = ti // GRID_W - w // 2, ti % GRID_W
        m = ((ri[None, :] >= (ro - w // 2)[:, None]) & (ri[None, :] < (ro + w - w // 2)[:, None])
             & (ci[None, :] >= (co - w // 2)[:, None]) & (ci[None, :] < (co + w - w // 2)[:, None]))
        mats.append(jnp.asarray(m, dtype=_BF16))
    return mats


def _grid_inverse_counts(rows):
    t = np.arange(rows * GRID_W)
    r, c = t // GRID_W, t % GRID_W
    tabs = []
    for w in POOL_WINDOWS:
        rc = np.minimum(r + w - w // 2, rows) - np.maximum(r - w // 2, 0)
        cc = np.minimum(c + w - w // 2, GRID_W) - np.maximum(c - w // 2, 0)
        tabs.append(1.0 / (rc * cc))
    return jnp.asarray(np.stack(tabs), dtype=_F32)


def _seq_window_matrices(n):
    t = np.arange(n)
    mats = []
    for w in POOL_WINDOWS:
        lo, hi = np.maximum(t - w // 2, 0), np.minimum(t + w - w // 2, n)
        mats.append((t[None, :] >= lo[:, None]) & (t[None, :] < hi[:, None]))
    return jnp.asarray(np.stack(mats), dtype=_BF16)


POOL_BLOCK_ROWS = 16
POOL_HALO_ROWS = 8
POOL_TILE_ROWS = 4


def _pool_kernel(xm_ref, xb_ref, mod_ref, g_ref, a2_ref, a4_ref, a8_ref, a16_ref, pw_ref, ps_ref, inv_ref,
                 o_ref, tail):
    i = pl.program_id(0)
    b = pl.program_id(1)
    nblk = pl.num_programs(0)
    sh1, sc1, g1 = _mod_slices(mod_ref[0])[:3]
    gn = g_ref[...]
    halo = POOL_HALO_ROWS * GRID_W
    main = POOL_BLOCK_ROWS * GRID_W
    tile = POOL_TILE_ROWS * GRID_W
    xm = xm_ref[0]
    xn_m = _rms_mod(xm, gn, sh1, sc1)
    xn_mb = xn_m.astype(_BF16)
    @pl.when(i == 0)
    def _():
        tail[b] = jnp.zeros((halo, D_MODEL), _BF16)

    xn_t = tail[b]
    xn_b = jnp.where(i < nblk - 1, _rms_mod(xb_ref[0], gn, sh1, sc1), 0.0).astype(_BF16)
    tail[b] = xn_mb[main - halo:]
    xn_ext = jnp.concatenate([xn_t, xn_mb, xn_b], axis=0)
    for gi, (w, a_ref) in enumerate(zip(POOL_WINDOWS, (a2_ref, a4_ref, a8_ref, a16_ref))):
        ch = slice(gi * POOL_CH, (gi + 1) * POOL_CH)
        a = a_ref[...]
        span = a.shape[1]
        tots = []
        for k in range(main // tile):
            start = halo + k * tile - (w // 2) * GRID_W
            tots.append(jnp.dot(a, xn_ext[start:start + span, ch], preferred_element_type=_F32))
        tot = jnp.concatenate(tots, axis=0)
        inv = inv_ref[gi]
        p = tot * jnp.concatenate([inv] * (POOL_CH // LANES), axis=1) - xn_m[:, ch]
        y = jnp.dot(p.astype(_BF16), pw_ref[0, gi], preferred_element_type=_F32) * ps_ref[:, ch]
        o_ref[0, :, ch] = xm[:, ch] + g1[:, ch] * y


def _pool_grid(x, mod, g, pw, layer, ps):
    b, n, d = x.shape
    rows = n // GRID_W
    main = POOL_BLOCK_ROWS * GRID_W
    halo = POOL_HALO_ROWS * GRID_W
    nblk = n // main
    ratio = main // halo
    nh = n // halo
    amats = _grid_window_matrices()
    assert all(w // 2 <= POOL_HALO_ROWS and _window_in_rows(w) - w // 2 <= POOL_TILE_ROWS + POOL_HALO_ROWS
               for w in POOL_WINDOWS)
    inv = jnp.broadcast_to(_grid_inverse_counts(rows)[:, :, None], (len(POOL_WINDOWS), n, LANES))
    vec = pl.BlockSpec((1, d), lambda i, bi: (0, 0))
    return pl.pallas_call(
        _pool_kernel,
        out_shape=jax.ShapeDtypeStruct((b, n, d), _F32),
        grid=(nblk, b),
        in_specs=[pl.BlockSpec((1, main, d), lambda i, bi: (bi, i, 0)),
                  pl.BlockSpec((1, halo, d), lambda i, bi: (bi, jnp.minimum((i + 1) * ratio, nh - 1), 0)),
                  pl.BlockSpec((1, 1, 6 * d), lambda i, bi: (bi, 0, 0)),
                  vec,
                  *[_resident(a.shape) for a in amats],
                  _resident_layer((len(POOL_WINDOWS), POOL_CH, POOL_CH), layer),
                  vec,
                  pl.BlockSpec((len(POOL_WINDOWS), main, LANES), lambda i, bi: (0, i, 0))],
        out_specs=pl.BlockSpec((1, main, d), lambda i, bi: (bi, i, 0)),
        scratch_shapes=[pltpu.VMEM((b, halo, d), _BF16)],
        compiler_params=_cparams(("arbitrary", "arbitrary")),
        name="pool_grid",
    )(x, x, mod, g, *amats, pw, ps, inv)


def _poolseq_kernel(x_ref, mod_ref, g_ref, aseq_ref, pw_ref, ps_ref, o_ref):
    sh1, sc1, g1 = _mod_slices(mod_ref[0])[:3]
    xf = x_ref[0]
    n = xf.shape[0]
    xn = _rms_mod(xf, g_ref[...], sh1, sc1)
    xb = xn.astype(_BF16)
    t = lax.broadcasted_iota(jnp.int32, (n, POOL_CH), 0)
    for gi, w in enumerate(POOL_WINDOWS):
        ch = slice(gi * POOL_CH, (gi + 1) * POOL_CH)
        tot = jnp.dot(aseq_ref[gi], xb[:, ch], preferred_element_type=_F32)
        cnt = jnp.minimum(t + (w - w // 2), n) - jnp.maximum(t - w // 2, 0)
        p = tot / cnt.astype(_F32) - xn[:, ch]
        y = jnp.dot(p.astype(_BF16), pw_ref[0, gi], preferred_element_type=_F32) * ps_ref[:, ch]
        o_ref[0, :, ch] = xf[:, ch] + g1[:, ch] * y


def _pool_seq(x, mod, mod_row, g, pw, layer, ps):
    b, n, d = x.shape
    aseq = _seq_window_matrices(n)
    vec = pl.BlockSpec((1, d), lambda bi: (0, 0))
    return pl.pallas_call(
        _poolseq_kernel,
        out_shape=jax.ShapeDtypeStruct((b, n, d), _F32),
        grid=(b,),
        in_specs=[pl.BlockSpec((1, n, d), lambda bi: (bi, 0, 0)),
                  pl.BlockSpec((1, 1, 6 * d), lambda bi: (mod_row, 0, 0)),
                  vec,
                  _resident((len(POOL_WINDOWS), n, n)),
                  _resident_layer((len(POOL_WINDOWS), POOL_CH, POOL_CH), layer),
                  vec],
        out_specs=pl.BlockSpec((1, n, d), lambda bi: (bi, 0, 0)),
        compiler_params=_cparams(("arbitrary",)),
        name="pool_seq",
    )(x, mod, g, aseq, pw, ps)


def kernel(x, c, ctx, c_ctx, ada_w, ada_b, norm1_g, norm2_g, s5_a_re, s5_a_im, s5_log_dt, s5_b_re, s5_b_im, s5_c_re, s5_c_im, s5_d, s5_glu_w, s5_glu_b, pool_w, pool_scale, mlp_w1, mlp_b1, mlp_w2, mlp_b2, final_g):
    bsz, n_tok, d = x.shape
    n_ctx = ctx.shape[1]
    depth = ada_w.shape[0]
    assert d == D_MODEL and n_tok % (POOL_BLOCK_ROWS * GRID_W) == 0
    assert LANES % bsz == 0 and bsz * CTX_SEG <= LANES and bsz < 8
    lat_chunks = n_tok // S5_CHUNK
    ctx_chunks = n_ctx // S5_CHUNK
    assert n_tok % S5_CHUNK == 0 and n_ctx % S5_CHUNK == 0 and (bsz * lat_chunks) % LANES == 0
    assert CTX_OFF >= 2 and CTX_OFF + ctx_chunks + 2 <= CTX_SEG
    nv = bsz * lat_chunks // LANES
    lg = LANES // bsz
    pows = _s5_pows(nv, lg)
    lat_lanes = (lg, 0, tuple(range(bsz)))
    ctx_lanes = (CTX_SEG, CTX_OFF, (bsz,) * bsz)
    last_ctx_reader = ((depth - 1) // N_MIXERS) * N_MIXERS
    ctx_row = bsz
    tt = min(512, n_tok)
    tt_ctx = min(512, n_ctx)

    cc = jnp.zeros((8, d), _F32).at[:bsz].set(c).at[ctx_row].set(c_ctx)
    mods = _modulation(cc, ada_w, ada_b)
    w1 = mlp_w1.astype(_BF16)
    w2 = mlp_w2.astype(_BF16)
    glu_w4 = (0.25 * s5_glu_w).astype(_BF16)
    pw = pool_w.astype(_BF16)

    h_ctx = ctx
    for i in range(depth):
        ctx_in = i <= last_ctx_reader
        ctx_out = i < last_ctx_reader
        j = i // N_MIXERS
        mod = mods[i].reshape(8, 1, 6 * d)
        g1n = norm1_g[i].reshape(1, d)
        g2n = norm2_g[i].reshape(1, d)
        if i % N_MIXERS == 0:
            mt, bst, cst, lampow = _s5prep(s5_a_re[j], s5_a_im[j], s5_log_dt[j], s5_b_re[j], s5_b_im[j],
                                           s5_c_re[j], s5_c_im[j], pows)
            dsk = s5_d[j].reshape(1, d)
            gb2 = 0.5 * s5_glu_b[j].reshape(1, d)
            c4 = h_ctx.reshape(bsz, ctx_chunks, S5_CHUNK, d)
            if ctx_in:
                utc = _s5pre(c4, mod, g1n, ctx_lanes).reshape(S5_GROUPS, S5_ROWS, LANES)
            else:
                utc = jnp.zeros((S5_GROUPS, S5_ROWS, LANES), _BF16)
            x4 = x.reshape(bsz, lg, n_tok // lg, d)
            utl = _s5pre(x4, mod, g1n, lat_lanes)
            ytl, ytc = _s5core(utl, utc, mt, bst, cst, lampow, bsz, ctx_chunks, pows, ctx_out)
            x = _s5post(x4, ytl, mod, g1n, dsk, glu_w4, j, gb2, lat_lanes).reshape(bsz, n_tok, d)
            if ctx_out:
                ytc4 = ytc.reshape(S5_GROUPS, 1, S5_ROWS, LANES)
                h_ctx = _s5post(c4, ytc4, mod, g1n, dsk, glu_w4, j, gb2, ctx_lanes).reshape(bsz, n_ctx, d)
        else:
            ps = pool_scale[j].reshape(1, d)
            x = _pool_grid(x, mod, g1n, pw, j, ps)
            if ctx_out:
                h_ctx = _pool_seq(h_ctx, mod, ctx_row, g1n, pw, j, ps)
        final = i == depth - 1
        x = _mlp(x, mod, None, g2n, w1, mlp_b1[i].reshape(1, D_FF), w2, mlp_b2[i].reshape(1, d), i,
                 final_g.reshape(1, d), tt, final)
        if ctx_out:
            h_ctx = _mlp(h_ctx, mod, ctx_row, g2n, w1, mlp_b1[i].reshape(1, D_FF), w2,
                         mlp_b2[i].reshape(1, d), i, final_g.reshape(1, d), tt_ctx, False)
    return x
```

```python
import functools
import math

import numpy as np
import jax
import jax.numpy as jnp
from jax import lax
from jax.experimental import pallas as pl
from jax.experimental.pallas import tpu as pltpu

D_MODEL = 1024
GRID_W = 64
S5_GROUP = 16
S5_GROUPS = D_MODEL // S5_GROUP
S5_STATE = 64
POOL_WINDOWS = (2, 4, 8, 16)
POOL_CH = D_MODEL // len(POOL_WINDOWS)
D_FF = 4 * D_MODEL
N_MIXERS = 2
EPS = 1e-6
LANES = 128

S5_CHUNK = 32
S5_ROWS = S5_CHUNK * S5_GROUP
S5_NSTATE = 4 * S5_STATE
CTX_SEG = 32
CTX_OFF = 8
S5_PRE_TOK = 16
S5_POST_TOK = 8
S5_CORE_GROUPS = 4
S5_PREP_GROUPS = 8
V7X_VMEM_LIMIT_BYTES = 56 * 1024 * 1024

_F32 = jnp.float32
_BF16 = jnp.bfloat16


def _cparams(sem):
    return pltpu.CompilerParams(dimension_semantics=sem, vmem_limit_bytes=V7X_VMEM_LIMIT_BYTES)


def _resident(shape):
    nd = len(shape)
    return pl.BlockSpec(shape, lambda *_: (0,) * nd, pipeline_mode=pl.Buffered(1))


def _resident_layer(shape, layer):
    nd = len(shape)
    return pl.BlockSpec((1,) + tuple(shape), lambda *_: (layer,) + (0,) * nd, pipeline_mode=pl.Buffered(1))


def _mod_kernel(cc_ref, w_ref, b_ref, o_ref):
    c = cc_ref[...]
    o_ref[0] = _dot_split(c * jax.nn.sigmoid(c), w_ref[0]) + b_ref[0]


def _modulation(cc, ada_w, ada_b):
    depth, d, n = ada_w.shape
    tn = 1536
    return pl.pallas_call(
        _mod_kernel,
        out_shape=jax.ShapeDtypeStruct((depth, 8, n), _F32),
        grid=(depth, n // tn),
        in_specs=[pl.BlockSpec((8, d), lambda i, j: (0, 0)),
                  pl.BlockSpec((1, d, tn), lambda i, j: (i, 0, j)),
                  pl.BlockSpec((1, 1, tn), lambda i, j: (i, 0, j))],
        out_specs=pl.BlockSpec((1, 8, tn), lambda i, j: (i, 0, j)),
        compiler_params=_cparams(("arbitrary", "arbitrary")),
        name="modulation",
    )(cc, ada_w, ada_b.reshape(depth, 1, n))


def _rms_mod(xf, g, shift, scale):
    ms = jnp.mean(xf * xf, axis=-1, keepdims=True)
    return (xf * lax.rsqrt(ms + EPS)) * (g * (1.0 + scale)) + shift


def _mod_slices(m):
    d = D_MODEL
    return tuple(m[:, k * d:(k + 1) * d] for k in range(6))


def _cmul(a, yr, yi):
    ar, ai = a
    return ar * yr - ai * yi, ar * yi + ai * yr


def _bf16_pair(x):
    hi = x.astype(_BF16)
    lo = (x - hi.astype(_F32)).astype(_BF16)
    return hi, lo


def _select_cols(x, sel):
    return sum(jnp.dot(part, sel, preferred_element_type=_F32) for part in _bf16_pair(x))


def _select_rows(sel, x):
    return sum(jnp.dot(sel, part, preferred_element_type=_F32) for part in _bf16_pair(x))


def _dot_split(a, b):
    (ah, al), (bh, bl) = _bf16_pair(a), _bf16_pair(b)
    return (jnp.dot(ah, bh, preferred_element_type=_F32) + jnp.dot(ah, bl, preferred_element_type=_F32)
            + jnp.dot(al, bh, preferred_element_type=_F32))


def _s5pre_kernel(x_ref, mod_ref, g_ref, ut_ref, *, k, lanes):
    seg, off, mod_rows = lanes
    nb, cnt = x_ref.shape[0], x_ref.shape[1]
    gn = g_ref[...]
    mods = [_mod_slices(mod_ref[mod_rows[b]])[:2] for b in range(nb)]
    xs = [jnp.swapaxes(x_ref[b], 0, 1) for b in range(nb)]
    pads = (off, seg - off - cnt)
    for j in range(k):
        parts = []
        for b in range(nb):
            piece = [_rms_mod(xs[b][j], gn, *mods[b])]
            if pads[0]:
                piece.insert(0, jnp.zeros((pads[0], D_MODEL), _F32))
            if pads[1]:
                piece.append(jnp.zeros((pads[1], D_MODEL), _F32))
            parts += piece
        if LANES - nb * seg:
            parts.append(jnp.zeros((LANES - nb * seg, D_MODEL), _F32))
        xn = jnp.concatenate(parts, axis=0)
        ut_ref[:, 0, j * S5_GROUP:(j + 1) * S5_GROUP, :] = (
            xn.T.astype(_BF16).reshape(S5_GROUPS, S5_GROUP, LANES))


def _s5pre(x4, mod, g, lanes):
    b, cnt, per, d = x4.shape
    k = S5_PRE_TOK
    blocks_per_v = S5_CHUNK // k
    return pl.pallas_call(
        functools.partial(_s5pre_kernel, k=k, lanes=lanes),
        out_shape=jax.ShapeDtypeStruct((S5_GROUPS, per // S5_CHUNK, S5_ROWS, LANES), _BF16),
        grid=(per // k,),
        in_specs=[pl.BlockSpec((b, cnt, k, d), lambda r: (0, 0, r, 0)),
                  pl.BlockSpec((8, 1, 6 * d), lambda r: (0, 0, 0)),
                  pl.BlockSpec((1, d), lambda r: (0, 0))],
        out_specs=pl.BlockSpec((S5_GROUPS, 1, k * S5_GROUP, LANES),
                               lambda r: (0, r // blocks_per_v, r % blocks_per_v, 0)),
        compiler_params=_cparams(("arbitrary",)),
        name="s5pre",
    )(x4, mod, g)


def _lane_shift(v, dist, pos, seg, reverse):
    n = v.shape[1]
    if reverse:
        return jnp.where(pos < seg - dist, pltpu.roll(v, n - dist, axis=1), 0.0)
    return jnp.where(pos >= dist, pltpu.roll(v, dist, axis=1), 0.0)


def _lane_scan(xr, xi, lam, unit, pos, seg, reverse):
    k = 0
    while (1 << k) < seg:
        dist = 1 << k
        mr, mi = _cmul(lam(unit * dist), _lane_shift(xr, dist, pos, seg, reverse),
                       _lane_shift(xi, dist, pos, seg, reverse))
        xr, xi = xr + mr, xi + mi
        k += 1
    return _lane_shift(xr, 1, pos, seg, reverse), _lane_shift(xi, 1, pos, seg, reverse)


def _s5core_kernel(utl_ref, utc_ref, mt_ref, bst_ref, cst_ref, lp_ref, ytl_ref, *rest,
                   nb, ctx_chunks, pows):
    for q in range(utl_ref.shape[0]):
        _s5core_group(q, utl_ref, utc_ref, mt_ref, bst_ref, cst_ref, lp_ref, ytl_ref,
                      rest[0] if len(rest) == 2 else None, rest[-1], nb, ctx_chunks, pows)


def _s5core_group(q, utl_ref, utc_ref, mt_ref, bst_ref, cst_ref, lp_ref, ytl_ref, ytc_ref, yacc,
                  nb, ctx_chunks, pows):
    p = S5_STATE
    nv = utl_ref.shape[1]
    ul = jnp.concatenate([utl_ref[q, v] for v in range(nv)], axis=1)
    uc = utc_ref[q]
    lg = LANES // nb
    bst, cst, mt = bst_ref[q], cst_ref[q], mt_ref[q]
    lp = lp_ref[q]

    def lam_of(d):
        def lam(n):
            k = pows.index(n)
            return lp[2 * d][:, k:k + 1], lp[2 * d + 1][:, k:k + 1]
        return lam

    lane = lax.broadcasted_iota(jnp.int32, (1, LANES), 1)
    sc = jnp.dot(bst, uc, preferred_element_type=_F32)
    sl = jnp.dot(bst, ul, preferred_element_type=_F32)
    yacc[q] = jnp.dot(mt, ul, preferred_element_type=_F32)
    posc = lane & (CTX_SEG - 1)
    hc = []
    for d in range(2):
        r0 = 2 * p * d
        hr, hi = _lane_scan(sc[r0:r0 + p], sc[r0 + p:r0 + 2 * p], lam_of(d), 1, posc, CTX_SEG, d == 1)
        hc += [hr, hi]
    posl = lane & (lg - 1)
    bidl = lane >> int(math.log2(lg))
    hl = []
    for d in range(2):
        lam = lam_of(d)
        r0 = 2 * p * d
        sr = [sl[r0:r0 + p, v * LANES:(v + 1) * LANES] for v in range(nv)]
        si = [sl[r0 + p:r0 + 2 * p, v * LANES:(v + 1) * LANES] for v in range(nv)]
        order = list(range(nv)) if d == 0 else list(range(nv - 1, -1, -1))
        ir, ii = {}, {}
        prev = None
        for v in order:
            if prev is None:
                ir[v], ii[v] = sr[v], si[v]
            else:
                mr, mi = _cmul(lam(1), ir[prev], ii[prev])
                ir[v], ii[v] = mr + sr[v], mi + si[v]
            prev = v
        entry = 0 if d == 0 else lg - 1
        src = (CTX_OFF + ctx_chunks) if d == 0 else (CTX_OFF - 1)
        h0r = jnp.zeros((p, LANES), _F32)
        h0i = jnp.zeros((p, LANES), _F32)
        for b in range(nb):
            m = (bidl == b) & (posl == entry)
            col = b * CTX_SEG + src
            h0r = jnp.where(m, hc[2 * d][:, col:col + 1], h0r)
            h0i = jnp.where(m, hc[2 * d + 1][:, col:col + 1], h0i)
        jr, ji = _cmul(lam(nv), h0r, h0i)
        er, ei = _lane_scan(ir[prev] + jr, ii[prev] + ji, lam, nv, posl, lg, d == 1)
        er, ei = er + h0r, ei + h0i
        hr, hi = {order[0]: er}, {order[0]: ei}
        for n, v in enumerate(order[1:], start=1):
            mr, mi = _cmul(lam(n), er, ei)
            hr[v], hi[v] = ir[order[n - 1]] + mr, ii[order[n - 1]] + mi
        hl += [jnp.concatenate([hr[v] for v in range(nv)], axis=1),
               jnp.concatenate([hi[v] for v in range(nv)], axis=1)]
    h = jnp.concatenate(hl, axis=0).astype(_BF16)
    y = yacc[q] + jnp.dot(cst, h, preferred_element_type=_F32)
    for v in range(nv):
        ytl_ref[q, v] = y[:, v * LANES:(v + 1) * LANES].astype(ytl_ref.dtype)
    if ytc_ref is not None:
        hcb = jnp.concatenate(hc, axis=0).astype(_BF16)
        yc = jnp.dot(mt, uc, preferred_element_type=_F32)
        ytc_ref[q] = (yc + jnp.dot(cst, hcb, preferred_element_type=_F32)).astype(ytc_ref.dtype)


def _s5_pows(nv, lg):
    pows = set(range(1, nv + 1))
    pows |= {nv << k for k in range(int(math.log2(lg)))}
    pows |= {1 << k for k in range(int(math.log2(CTX_SEG)))}
    return tuple(sorted(pows))


def _s5core(utl, utc, mt, bst, cst, lampow, nb, ctx_chunks, pows, want_yc):
    g, nv, rows, _ = utl.shape
    gq = S5_CORE_GROUPS
    out_shape = [jax.ShapeDtypeStruct((g, nv, rows, LANES), _BF16)]
    out_specs = [pl.BlockSpec((gq, nv, rows, LANES), lambda i: (i, 0, 0, 0))]
    if want_yc:
        out_shape.append(jax.ShapeDtypeStruct((g, rows, LANES), _BF16))
        out_specs.append(pl.BlockSpec((gq, rows, LANES), lambda i: (i, 0, 0)))
    res = pl.pallas_call(
        functools.partial(_s5core_kernel, nb=nb, ctx_chunks=ctx_chunks, pows=pows),
        out_shape=out_shape,
        grid=(g // gq,),
        in_specs=[pl.BlockSpec((gq, nv, rows, LANES), lambda i: (i, 0, 0, 0)),
                  pl.BlockSpec((gq, rows, LANES), lambda i: (i, 0, 0)),
                  pl.BlockSpec((gq, rows, rows), lambda i: (i, 0, 0)),
                  pl.BlockSpec((gq, S5_NSTATE, rows), lambda i: (i, 0, 0)),
                  pl.BlockSpec((gq, rows, S5_NSTATE), lambda i: (i, 0, 0)),
                  pl.BlockSpec((gq, 4, S5_STATE, LANES), lambda i: (i, 0, 0, 0))],
        out_specs=out_specs,
        scratch_shapes=[pltpu.VMEM((gq, rows, nv * LANES), _F32)],
        compiler_params=_cparams(("arbitrary",)),
        name="s5core_yc" if want_yc else "s5core",
    )(utl, utc, mt, bst, cst, lampow)
    return (res[0], res[1]) if want_yc else (res[0], None)


def _s5prep_kernel(*refs):
    for q in range(refs[0].shape[1]):
        _s5prep_group(q, *refs)


def _s5prep_group(q, acol_ref, arow_ref, ldt_ref, bre_ref, bim_ref, cre_ref, cim_ref, c1_ref, c2_ref,
                  esel_ref, etile_ref, e2_ref, e3_ref, pcol_ref, mt_ref, bst_ref, cst_ref, lp_ref):
    t_len, p = S5_CHUNK, S5_STATE
    lane = lax.broadcasted_iota(jnp.int32, (1, LANES), 1)
    n_lane = lane.astype(_F32)
    npow = pcol_ref.shape[0]
    strips = []
    for d in range(2):
        dt = jnp.exp(ldt_ref[d, q])
        a = acol_ref[d, q]
        ar, ai = a[:, 0:1], a[:, 1:2]
        dar, dai = ar * dt, ai * dt
        pm, pa = jnp.exp(n_lane * dar), n_lane * dai
        lr, li = pm * jnp.cos(pa), pm * jnp.sin(pa)
        lbr, lbi = lr[:, 1:2], li[:, 1:2]
        den = ar * ar + ai * ai
        nr, ni = lbr - 1.0, lbi
        kr = (nr * ar + ni * ai) / den
        ki = (ni * ar - nr * ai) / den
        bbr = kr * bre_ref[d, q] - ki * bim_ref[d, q]
        bbi = kr * bim_ref[d, q] + ki * bre_ref[d, q]
        wr = _select_cols(lr, esel_ref[d])
        wi = _select_cols(li, esel_ref[d])
        btr = _select_cols(bbr, etile_ref[...])
        bti = _select_cols(bbi, etile_ref[...])
        bsr, bsi = wr * btr - wi * bti, wr * bti + wi * btr
        bst_ref[q, (2 * d) * p:(2 * d + 1) * p, :] = bsr.astype(_BF16)
        bst_ref[q, (2 * d + 1) * p:(2 * d + 2) * p, :] = bsi.astype(_BF16)
        strips.append(_dot_split(cre_ref[d, q], bsr) - _dot_split(cim_ref[d, q], bsi))
        w1 = _select_rows(e2_ref[d], jnp.concatenate([lr, li], axis=0).T)
        w2 = _select_rows(e2_ref[d], jnp.concatenate([li, lr], axis=0).T)
        c1 = _select_rows(e3_ref[...], c1_ref[d, q])
        c2 = _select_rows(e3_ref[...], c2_ref[d, q])
        cst_ref[q, :, d * LANES:(d + 1) * LANES] = (c1 * w1 + c2 * w2).astype(_BF16)
        arow = arow_ref[d, q]
        m = pcol_ref[...] * float(t_len)
        qm, qa = jnp.exp(m * (arow[0:1] * dt)), m * (arow[1:2] * dt)
        packed = qm * jnp.where(lane < p, jnp.cos(qa), jnp.sin(qa))
        lpt = jnp.concatenate([packed, jnp.zeros((LANES - npow, LANES), _F32)], axis=0).T
        lp_ref[q, 2 * d] = lpt[:p]
        lp_ref[q, 2 * d + 1] = lpt[p:]
    rf, rb = strips
    lane_w = lax.broadcasted_iota(jnp.int32, (1, S5_ROWS), 1)
    for t in range(t_len):
        sf = (t_len - 1 - t) * S5_GROUP
        f = rf if sf == 0 else pltpu.roll(rf, S5_ROWS - sf, axis=1)
        f = jnp.where(lane_w < (t + 1) * S5_GROUP, f, 0.0)
        sb = t * S5_GROUP
        bk = rb if sb == 0 else pltpu.roll(rb, sb, axis=1)
        bk = jnp.where(lane_w >= sb, bk, 0.0)
        mt_ref[q, t * S5_GROUP:(t + 1) * S5_GROUP, :] = (f + bk).astype(_BF16)


def _s5prep_constants(pows):
    t_len, h = S5_CHUNK, S5_GROUP
    s_of = np.arange(S5_ROWS) // h
    h_of = np.arange(S5_ROWS) % h
    n128 = np.arange(LANES)
    esel = np.stack([n128[:, None] == (t_len - 1 - s_of)[None, :], n128[:, None] == s_of[None, :]])
    etile = np.arange(h)[:, None] == h_of[None, :]
    e2 = np.stack([(s_of + 1)[:, None] == n128[None, :], (t_len - s_of)[:, None] == n128[None, :]])
    e3 = h_of[:, None] == np.arange(h)[None, :]
    pcol = np.zeros((16, 1), np.float32)
    pcol[:len(pows), 0] = pows
    f = lambda m: jnp.asarray(m, dtype=_BF16)
    return f(esel), f(etile), f(e2), f(e3), jnp.asarray(pcol)


def _s5prep(a_re, a_im, log_dt, b_re, b_im, c_re, c_im, pows):
    g, p, h = S5_GROUPS, S5_STATE, S5_GROUP
    acol = jnp.stack([a_re, a_im], axis=-1)
    arow = jnp.stack([jnp.concatenate([a_re, a_re], -1),
                      jnp.concatenate([a_im, a_im], -1)], axis=2)
    ldt = log_dt.reshape(2, g, 1, 1)
    c1 = jnp.concatenate([c_re, -c_re], -1)
    c2 = jnp.concatenate([-c_im, -c_im], -1)
    esel, etile, e2, e3, pcol = _s5prep_constants(pows)

    gq = S5_PREP_GROUPS

    def per_g(*tail):
        return pl.BlockSpec((2, gq) + tail, lambda i: (0, i) + (0,) * len(tail))

    return pl.pallas_call(
        _s5prep_kernel,
        out_shape=[jax.ShapeDtypeStruct((g, S5_ROWS, S5_ROWS), _BF16),
                   jax.ShapeDtypeStruct((g, S5_NSTATE, S5_ROWS), _BF16),
                   jax.ShapeDtypeStruct((g, S5_ROWS, S5_NSTATE), _BF16),
                   jax.ShapeDtypeStruct((g, 4, p, LANES), _F32)],
        grid=(g // gq,),
        in_specs=[per_g(p, 2), per_g(2, LANES), per_g(1, 1), per_g(p, h), per_g(p, h),
                  per_g(h, p), per_g(h, p), per_g(h, LANES), per_g(h, LANES),
                  _resident(esel.shape), _resident(etile.shape), _resident(e2.shape), _resident(e3.shape),
                  _resident(pcol.shape)],
        out_specs=[pl.BlockSpec((gq, S5_ROWS, S5_ROWS), lambda i: (i, 0, 0)),
                   pl.BlockSpec((gq, S5_NSTATE, S5_ROWS), lambda i: (i, 0, 0)),
                   pl.BlockSpec((gq, S5_ROWS, S5_NSTATE), lambda i: (i, 0, 0)),
                   pl.BlockSpec((gq, 4, p, LANES), lambda i: (i, 0, 0, 0))],
        compiler_params=_cparams(("arbitrary",)),
        name="s5prep",
    )(acol, arow, ldt, b_re, b_im, c_re, c_im, c1, c2, esel, etile, e2, e3, pcol)


def _s5post_kernel(x_ref, yt_ref, mod_ref, g_ref, dsk_ref, gw_ref, gb_ref, o_ref, *, k, lanes):
    seg, off, mod_rows = lanes
    nb, lg = x_ref.shape[0], x_ref.shape[1]
    d = D_MODEL
    gn = g_ref[...]
    dsk = dsk_ref[...]
    c = math.sqrt(2.0 / math.pi)
    mods = [_mod_slices(mod_ref[mod_rows[b]])[:3] for b in range(nb)]
    ys = [yt_ref[:, 0, j * S5_GROUP:(j + 1) * S5_GROUP, :].astype(_F32).reshape(d, LANES).T
          for j in range(k)]
    rows = lg * k
    xs, zs = [], []
    for b in range(nb):
        sh1, sc1, _ = mods[b]
        r0 = b * seg + off
        yb = jnp.stack([ys[j][r0:r0 + lg] for j in range(k)], axis=0)
        yb = jnp.swapaxes(yb, 0, 1).reshape(rows, d)
        xf = x_ref[b].reshape(rows, d)
        ms = jnp.mean(xf * xf, axis=-1, keepdims=True)
        pre = (xf * lax.rsqrt(ms + EPS)) * ((gn * (1.0 + sc1)) * dsk) + (yb + sh1 * dsk)
        xs.append(xf)
        zs.append(pre + pre * jnp.tanh(pre * (c + (c * 0.044715) * (pre * pre))))
    z2 = jnp.concatenate(zs, axis=0)
    th = jnp.tanh(jnp.dot(z2.astype(_BF16), gw_ref[0], preferred_element_type=_F32) + gb_ref[...])
    for b in range(nb):
        q = (0.25 * mods[b][2]) * z2[b * rows:(b + 1) * rows]
        o_ref[b] = (xs[b] + (q * th[b * rows:(b + 1) * rows] + q)).reshape(lg, k, d)


def _s5post(x4, yt, mod, g, dsk, gw, layer, gb, lanes):
    b, lg, per, d = x4.shape
    k = S5_POST_TOK
    blocks_per_v = S5_CHUNK // k
    tok = pl.BlockSpec((b, lg, k, d), lambda r: (0, 0, r, 0))
    vec = pl.BlockSpec((1, d), lambda r: (0, 0))
    return pl.pallas_call(
        functools.partial(_s5post_kernel, k=k, lanes=lanes),
        out_shape=jax.ShapeDtypeStruct((b, lg, per, d), _F32),
        grid=(per // k,),
        in_specs=[tok,
                  pl.BlockSpec((S5_GROUPS, 1, k * S5_GROUP, LANES),
                               lambda r: (0, r // blocks_per_v, r % blocks_per_v, 0)),
                  pl.BlockSpec((8, 1, 6 * d), lambda r: (0, 0, 0)),
                  vec, vec, _resident_layer((d, d), layer), vec],
        out_specs=tok,
        compiler_params=_cparams(("arbitrary",)),
        name="s5post",
    )(x4, yt, mod, g, dsk, gw, gb)


def _mlp_kernel(x_ref, mod_ref, g_ref, w1_ref, b1_ref, w2_ref, b2_ref, fg_ref, o_ref, *, final):
    sh2, sc2, g2 = _mod_slices(mod_ref[0])[3:]
    xf = x_ref[0]
    xn = _rms_mod(xf, g_ref[...], sh2, sc2).astype(_BF16)
    acc = jnp.zeros(xf.shape, _F32)
    fc = D_MODEL
    for j in range(D_FF // fc):
        a = jnp.dot(xn, w1_ref[0, :, j * fc:(j + 1) * fc], preferred_element_type=_F32)
        a = jnp.maximum(a + b1_ref[:, j * fc:(j + 1) * fc], 0.0)
        acc = acc + jnp.dot((a * a).astype(_BF16), w2_ref[0, j * fc:(j + 1) * fc, :],
                            preferred_element_type=_F32)
    out = xf + g2 * (acc + b2_ref[...])
    if final:
        ms = jnp.mean(out * out, axis=-1, keepdims=True)
        out = out * lax.rsqrt(ms + EPS) * fg_ref[...]
    o_ref[0] = out


def _mlp(x, mod, mod_row, g, w1, b1, w2, b2, layer, fg, tt, final):
    b, n, d = x.shape
    row = (lambda bi: bi) if mod_row is None else (lambda bi: mod_row)
    tok = pl.BlockSpec((1, tt, d), lambda bi, t: (bi, t, 0))
    vec = pl.BlockSpec((1, d), lambda bi, t: (0, 0))
    return pl.pallas_call(
        functools.partial(_mlp_kernel, final=final),
        out_shape=jax.ShapeDtypeStruct((b, n, d), _F32),
        grid=(b, n // tt),
        in_specs=[tok, pl.BlockSpec((1, 1, 6 * d), lambda bi, t: (row(bi), 0, 0)), vec,
                  _resident_layer((d, D_FF), layer), pl.BlockSpec((1, D_FF), lambda bi, t: (0, 0)),
                  _resident_layer((D_FF, d), layer), vec, vec],
        out_specs=tok,
        compiler_params=_cparams(("arbitrary", "arbitrary")),
        name="mlp_final" if final else "mlp",
    )(x, mod, g, w1, b1, w2, b2, fg)


def _window_in_rows(w):
    need = POOL_TILE_ROWS + w - 1
    return -(-need // POOL_TILE_ROWS) * POOL_TILE_ROWS


def _grid_window_matrices():
    to = np.arange(POOL_TILE_ROWS * GRID_W)
    ro, co = to // GRID_W, to % GRID_W
    mats = []
    for w in POOL_WINDOWS:
        ti = np.arange(_window_in_rows(w) * GRID_W)
        ri, ci = ti // GRID_W - w // 2, ti % GRID_W
        m = ((ri[None, :] >= (ro - w // 2)[:, None]) & (ri[None, :] < (ro + w - w // 2)[:, None])
             & (ci[None, :] >= (co - w // 2)[:, None]) & (ci[None, :] < (co + w - w // 2)[:, None]))
        mats.append(jnp.asarray(m, dtype=_BF16))
    return mats


def _grid_inverse_counts(rows):
    t = np.arange(rows * GRID_W)
    r, c = t // GRID_W, t % GRID_W
    tabs = []
    for w in POOL_WINDOWS:
        rc = np.minimum(r + w - w // 2, rows) - np.maximum(r - w // 2, 0)
        cc = np.minimum(c + w - w // 2, GRID_W) - np.maximum(c - w // 2, 0)
        tabs.append(1.0 / (rc * cc))
    return jnp.asarray(np.stack(tabs), dtype=_F32)


def _seq_window_matrices(n):
    t = np.arange(n)
    mats = []
    for w in POOL_WINDOWS:
        lo, hi = np.maximum(t - w // 2, 0), np.minimum(t + w - w // 2, n)
        mats.append((t[None, :] >= lo[:, None]) & (t[None, :] < hi[:, None]))
    return jnp.asarray(np.stack(mats), dtype=_BF16)


POOL_BLOCK_ROWS = 16
POOL_HALO_ROWS = 8
POOL_TILE_ROWS = 4


def _pool_kernel(xm_ref, xb_ref, mod_ref, g_ref, a2_ref, a4_ref, a8_ref, a16_ref, pw_ref, ps_ref, inv_ref,
                 o_ref, tail):
    i = pl.program_id(0)
    b = pl.program_id(1)
    nblk = pl.num_programs(0)
    sh1, sc1, g1 = _mod_slices(mod_ref[0])[:3]
    gn = g_ref[...]
    halo = POOL_HALO_ROWS * GRID_W
    main = POOL_BLOCK_ROWS * GRID_W
    tile = POOL_TILE_ROWS * GRID_W
    xm = xm_ref[0]
    xn_m = _rms_mod(xm, gn, sh1, sc1)
    xn_mb = xn_m.astype(_BF16)
    @pl.when(i == 0)
    def _():
        tail[b] = jnp.zeros((halo, D_MODEL), _BF16)

    xn_t = tail[b]
    xn_b = jnp.where(i < nblk - 1, _rms_mod(xb_ref[0], gn, sh1, sc1), 0.0).astype(_BF16)
    tail[b] = xn_mb[main - halo:]
    xn_ext = jnp.concatenate([xn_t, xn_mb, xn_b], axis=0)
    for gi, (w, a_ref) in enumerate(zip(POOL_WINDOWS, (a2_ref, a4_ref, a8_ref, a16_ref))):
        ch = slice(gi * POOL_CH, (gi + 1) * POOL_CH)
        a = a_ref[...]
        span = a.shape[1]
        tots = []
        for k in range(main // tile):
            start = halo + k * tile - (w // 2) * GRID_W
            tots.append(jnp.dot(a, xn_ext[start:start + span, ch], preferred_element_type=_F32))
        tot = jnp.concatenate(tots, axis=0)
        inv = inv_ref[gi]
        p = tot * jnp.concatenate([inv] * (POOL_CH // LANES), axis=1) - xn_m[:, ch]
        y = jnp.dot(p.astype(_BF16), pw_ref[0, gi], preferred_element_type=_F32) * ps_ref[:, ch]
        o_ref[0, :, ch] = xm[:, ch] + g1[:, ch] * y


def _pool_grid(x, mod, g, pw, layer, ps):
    b, n, d = x.shape
    rows = n // GRID_W
    main = POOL_BLOCK_ROWS * GRID_W
    halo = POOL_HALO_ROWS * GRID_W
    nblk = n // main
    ratio = main // halo
    nh = n // halo
    amats = _grid_window_matrices()
    assert all(w // 2 <= POOL_HALO_ROWS and _window_in_rows(w) - w // 2 <= POOL_TILE_ROWS + POOL_HALO_ROWS
               for w in POOL_WINDOWS)
    inv = jnp.broadcast_to(_grid_inverse_counts(rows)[:, :, None], (len(POOL_WINDOWS), n, LANES))
    vec = pl.BlockSpec((1, d), lambda i, bi: (0, 0))
    return pl.pallas_call(
        _pool_kernel,
        out_shape=jax.ShapeDtypeStruct((b, n, d), _F32),
        grid=(nblk, b),
        in_specs=[pl.BlockSpec((1, main, d), lambda i, bi: (bi, i, 0)),
                  pl.BlockSpec((1, halo, d), lambda i, bi: (bi, jnp.minimum((i + 1) * ratio, nh - 1), 0)),
                  pl.BlockSpec((1, 1, 6 * d), lambda i, bi: (bi, 0, 0)),
                  vec,
                  *[_resident(a.shape) for a in amats],
                  _resident_layer((len(POOL_WINDOWS), POOL_CH, POOL_CH), layer),
                  vec,
                  pl.BlockSpec((len(POOL_WINDOWS), main, LANES), lambda i, bi: (0, i, 0))],
        out_specs=pl.BlockSpec((1, main, d), lambda i, bi: (bi, i, 0)),
        scratch_shapes=[pltpu.VMEM((b, halo, d), _BF16)],
        compiler_params=_cparams(("arbitrary", "arbitrary")),
        name="pool_grid",
    )(x, x, mod, g, *amats, pw, ps, inv)


def _poolseq_kernel(x_ref, mod_ref, g_ref, aseq_ref, pw_ref, ps_ref, o_ref):
    sh1, sc1, g1 = _mod_slices(mod_ref[0])[:3]
    xf = x_ref[0]
    n = xf.shape[0]
    xn = _rms_mod(xf, g_ref[...], sh1, sc1)
    xb = xn.astype(_BF16)
    t = lax.broadcasted_iota(jnp.int32, (n, POOL_CH), 0)
    for gi, w in enumerate(POOL_WINDOWS):
        ch = slice(gi * POOL_CH, (gi + 1) * POOL_CH)
        tot = jnp.dot(aseq_ref[gi], xb[:, ch], preferred_element_type=_F32)
        cnt = jnp.minimum(t + (w - w // 2), n) - jnp.maximum(t - w // 2, 0)
        p = tot / cnt.astype(_F32) - xn[:, ch]
        y = jnp.dot(p.astype(_BF16), pw_ref[0, gi], preferred_element_type=_F32) * ps_ref[:, ch]
        o_ref[0, :, ch] = xf[:, ch] + g1[:, ch] * y


def _pool_seq(x, mod, mod_row, g, pw, layer, ps):
    b, n, d = x.shape
    aseq = _seq_window_matrices(n)
    vec = pl.BlockSpec((1, d), lambda bi: (0, 0))
    return pl.pallas_call(
        _poolseq_kernel,
        out_shape=jax.ShapeDtypeStruct((b, n, d), _F32),
        grid=(b,),
        in_specs=[pl.BlockSpec((1, n, d), lambda bi: (bi, 0, 0)),
                  pl.BlockSpec((1, 1, 6 * d), lambda bi: (mod_row, 0, 0)),
                  vec,
                  _resident((len(POOL_WINDOWS), n, n)),
                  _resident_layer((len(POOL_WINDOWS), POOL_CH, POOL_CH), layer),
                  vec],
        out_specs=pl.BlockSpec((1, n, d), lambda bi: (bi, 0, 0)),
        compiler_params=_cparams(("arbitrary",)),
        name="pool_seq",
    )(x, mod, g, aseq, pw, ps)


def kernel(x, c, ctx, c_ctx, ada_w, ada_b, norm1_g, norm2_g, s5_a_re, s5_a_im, s5_log_dt, s5_b_re, s5_b_im, s5_c_re, s5_c_im, s5_d, s5_glu_w, s5_glu_b, pool_w, pool_scale, mlp_w1, mlp_b1, mlp_w2, mlp_b2, final_g):
    bsz, n_tok, d = x.shape
    n_ctx = ctx.shape[1]
    depth = ada_w.shape[0]
    assert d == D_MODEL and n_tok % (POOL_BLOCK_ROWS * GRID_W) == 0
    assert LANES % bsz == 0 and bsz * CTX_SEG <= LANES and bsz < 8
    lat_chunks = n_tok // S5_CHUNK
    ctx_chunks = n_ctx // S5_CHUNK
    assert n_tok % S5_CHUNK == 0 and n_ctx % S5_CHUNK == 0 and (bsz * lat_chunks) % LANES == 0
    assert CTX_OFF >= 2 and CTX_OFF + ctx_chunks + 2 <= CTX_SEG
    nv = bsz * lat_chunks // LANES
    lg = LANES // bsz
    pows = _s5_pows(nv, lg)
    lat_lanes = (lg, 0, tuple(range(bsz)))
    ctx_lanes = (CTX_SEG, CTX_OFF, (bsz,) * bsz)
    last_ctx_reader = ((depth - 1) // N_MIXERS) * N_MIXERS
    ctx_row = bsz
    tt = min(512, n_tok)
    tt_ctx = min(512, n_ctx)

    cc = jnp.zeros((8, d), _F32).at[:bsz].set(c).at[ctx_row].set(c_ctx)
    mods = _modulation(cc, ada_w, ada_b)
    w1 = mlp_w1.astype(_BF16)
    w2 = mlp_w2.astype(_BF16)
    glu_w4 = (0.25 * s5_glu_w).astype(_BF16)
    pw = pool_w.astype(_BF16)

    h_ctx = ctx
    for i in range(depth):
        ctx_in = i <= last_ctx_reader
        ctx_out = i < last_ctx_reader
        j = i // N_MIXERS
        mod = mods[i].reshape(8, 1, 6 * d)
        g1n = norm1_g[i].reshape(1, d)
        g2n = norm2_g[i].reshape(1, d)
        if i % N_MIXERS == 0:
            mt, bst, cst, lampow = _s5prep(s5_a_re[j], s5_a_im[j], s5_log_dt[j], s5_b_re[j], s5_b_im[j],
                                           s5_c_re[j], s5_c_im[j], pows)
            dsk = s5_d[j].reshape(1, d)
            gb2 = 0.5 * s5_glu_b[j].reshape(1, d)
            c4 = h_ctx.reshape(bsz, ctx_chunks, S5_CHUNK, d)
            if ctx_in:
                utc = _s5pre(c4, mod, g1n, ctx_lanes).reshape(S5_GROUPS, S5_ROWS, LANES)
            else:
                utc = jnp.zeros((S5_GROUPS, S5_ROWS, LANES), _BF16)
            x4 = x.reshape(bsz, lg, n_tok // lg, d)
            utl = _s5pre(x4, mod, g1n, lat_lanes)
            ytl, ytc = _s5core(utl, utc, mt, bst, cst, lampow, bsz, ctx_chunks, pows, ctx_out)
            x = _s5post(x4, ytl, mod, g1n, dsk, glu_w4, j, gb2, lat_lanes).reshape(bsz, n_tok, d)
            if ctx_out:
                ytc4 = ytc.reshape(S5_GROUPS, 1, S5_ROWS, LANES)
                h_ctx = _s5post(c4, ytc4, mod, g1n, dsk, glu_w4, j, gb2, ctx_lanes).reshape(bsz, n_ctx, d)
        else:
            ps = pool_scale[j].reshape(1, d)
            x = _pool_grid(x, mod, g1n, pw, j, ps)
            if ctx_out:
                h_ctx = _pool_seq(h_ctx, mod, ctx_row, g1n, pw, j, ps)
        final = i == depth - 1
        x = _mlp(x, mod, None, g2n, w1, mlp_b1[i].reshape(1, D_FF), w2, mlp_b2[i].reshape(1, d), i,
                 final_g.reshape(1, d), tt, final)
        if ctx_out:
            h_ctx = _mlp(h_ctx, mod, ctx_row, g2n, w1, mlp_b1[i].reshape(1, D_FF), w2,
                         mlp_b2[i].reshape(1, d), i, final_g.reshape(1, d), tt_ctx, False)
    return x
```

```python
import functools
import math

import numpy as np
import jax
import jax.numpy as jnp
from jax import lax
from jax.experimental import pallas as pl
from jax.experimental.pallas import tpu as pltpu

D_MODEL = 1024
GRID_W = 64
S5_GROUP = 16
S5_GROUPS = D_MODEL // S5_GROUP
S5_STATE = 64
POOL_WINDOWS = (2, 4, 8, 16)
POOL_CH = D_MODEL // len(POOL_WINDOWS)
D_FF = 4 * D_MODEL
N_MIXERS = 2
EPS = 1e-6
LANES = 128

S5_CHUNK = 32
S5_ROWS = S5_CHUNK * S5_GROUP
S5_NSTATE = 4 * S5_STATE
CTX_SEG = 32
CTX_OFF = 8
S5_PRE_TOK = 16
S5_POST_TOK = 8
S5_CORE_GROUPS = 4
S5_PREP_GROUPS = 8
V7X_VMEM_LIMIT_BYTES = 56 * 1024 * 1024

_F32 = jnp.float32
_BF16 = jnp.bfloat16


def _cparams(sem):
    return pltpu.CompilerParams(dimension_semantics=sem, vmem_limit_bytes=V7X_VMEM_LIMIT_BYTES)


def _resident(shape):
    nd = len(shape)
    return pl.BlockSpec(shape, lambda *_: (0,) * nd, pipeline_mode=pl.Buffered(1))


def _resident_layer(shape, layer):
    nd = len(shape)
    return pl.BlockSpec((1,) + tuple(shape), lambda *_: (layer,) + (0,) * nd, pipeline_mode=pl.Buffered(1))


def _mod_kernel(cc_ref, w_ref, b_ref, o_ref):
    c = cc_ref[...]
    o_ref[0] = _dot_split(c * jax.nn.sigmoid(c), w_ref[0]) + b_ref[0]


def _modulation(cc, ada_w, ada_b):
    depth, d, n = ada_w.shape
    tn = 1536
    return pl.pallas_call(
        _mod_kernel,
        out_shape=jax.ShapeDtypeStruct((depth, 8, n), _F32),
        grid=(depth, n // tn),
        in_specs=[pl.BlockSpec((8, d), lambda i, j: (0, 0)),
                  pl.BlockSpec((1, d, tn), lambda i, j: (i, 0, j)),
                  pl.BlockSpec((1, 1, tn), lambda i, j: (i, 0, j))],
        out_specs=pl.BlockSpec((1, 8, tn), lambda i, j: (i, 0, j)),
        compiler_params=_cparams(("arbitrary", "arbitrary")),
        name="modulation",
    )(cc, ada_w, ada_b.reshape(depth, 1, n))


def _rms_mod(xf, g, shift, scale):
    ms = jnp.mean(xf * xf, axis=-1, keepdims=True)
    return (xf * lax.rsqrt(ms + EPS)) * (g * (1.0 + scale)) + shift


def _mod_slices(m):
    d = D_MODEL
    return tuple(m[:, k * d:(k + 1) * d] for k in range(6))


def _cmul(a, yr, yi):
    ar, ai = a
    return ar * yr - ai * yi, ar * yi + ai * yr


def _bf16_pair(x):
    hi = x.astype(_BF16)
    lo = (x - hi.astype(_F32)).astype(_BF16)
    return hi, lo


def _select_cols(x, sel):
    return sum(jnp.dot(part, sel, preferred_element_type=_F32) for part in _bf16_pair(x))


def _select_rows(sel, x):
    return sum(jnp.dot(sel, part, preferred_element_type=_F32) for part in _bf16_pair(x))


def _dot_split(a, b):
    (ah, al), (bh, bl) = _bf16_pair(a), _bf16_pair(b)
    return (jnp.dot(ah, bh, preferred_element_type=_F32) + jnp.dot(ah, bl, preferred_element_type=_F32)
            + jnp.dot(al, bh, preferred_element_type=_F32))


def _s5pre_kernel(x_ref, mod_ref, g_ref, ut_ref, *, k, lanes):
    seg, off, mod_rows = lanes
    nb, cnt = x_ref.shape[0], x_ref.shape[1]
    gn = g_ref[...]
    mods = [_mod_slices(mod_ref[mod_rows[b]])[:2] for b in range(nb)]
    xs = [jnp.swapaxes(x_ref[b], 0, 1) for b in range(nb)]
    pads = (off, seg - off - cnt)
    for j in range(k):
        parts = []
        for b in range(nb):
            piece = [_rms_mod(xs[b][j], gn, *mods[b])]
            if pads[0]:
                piece.insert(0, jnp.zeros((pads[0], D_MODEL), _F32))
            if pads[1]:
                piece.append(jnp.zeros((pads[1], D_MODEL), _F32))
            parts += piece
        if LANES - nb * seg:
            parts.append(jnp.zeros((LANES - nb * seg, D_MODEL), _F32))
        xn = jnp.concatenate(parts, axis=0)
        ut_ref[:, 0, j * S5_GROUP:(j + 1) * S5_GROUP, :] = (
            xn.T.astype(_BF16).reshape(S5_GROUPS, S5_GROUP, LANES))


def _s5pre(x4, mod, g, lanes):
    b, cnt, per, d = x4.shape
    k = S5_PRE_TOK
    blocks_per_v = S5_CHUNK // k
    return pl.pallas_call(
        functools.partial(_s5pre_kernel, k=k, lanes=lanes),
        out_shape=jax.ShapeDtypeStruct((S5_GROUPS, per // S5_CHUNK, S5_ROWS, LANES), _BF16),
        grid=(per // k,),
        in_specs=[pl.BlockSpec((b, cnt, k, d), lambda r: (0, 0, r, 0)),
                  pl.BlockSpec((8, 1, 6 * d), lambda r: (0, 0, 0)),
                  pl.BlockSpec((1, d), lambda r: (0, 0))],
        out_specs=pl.BlockSpec((S5_GROUPS, 1, k * S5_GROUP, LANES),
                               lambda r: (0, r // blocks_per_v, r % blocks_per_v, 0)),
        compiler_params=_cparams(("arbitrary",)),
        name="s5pre",
    )(x4, mod, g)


def _lane_shift(v, dist, pos, seg, reverse):
    n = v.shape[1]
    if reverse:
        return jnp.where(pos < seg - dist, pltpu.roll(v, n - dist, axis=1), 0.0)
    return jnp.where(pos >= dist, pltpu.roll(v, dist, axis=1), 0.0)


def _lane_scan(xr, xi, lam, unit, pos, seg, reverse):
    k = 0
    while (1 << k) < seg:
        dist = 1 << k
        mr, mi = _cmul(lam(unit * dist), _lane_shift(xr, dist, pos, seg, reverse),
                       _lane_shift(xi, dist, pos, seg, reverse))
        xr, xi = xr + mr, xi + mi
        k += 1
    return _lane_shift(xr, 1, pos, seg, reverse), _lane_shift(xi, 1, pos, seg, reverse)


def _s5core_kernel(utl_ref, utc_ref, mt_ref, bst_ref, cst_ref, lp_ref, ytl_ref, *rest,
                   nb, ctx_chunks, pows):
    for q in range(utl_ref.shape[0]):
        _s5core_group(q, utl_ref, utc_ref, mt_ref, bst_ref, cst_ref, lp_ref, ytl_ref,
                      rest[0] if len(rest) == 2 else None, rest[-1], nb, ctx_chunks, pows)


def _s5core_group(q, utl_ref, utc_ref, mt_ref, bst_ref, cst_ref, lp_ref, ytl_ref, ytc_ref, yacc,
                  nb, ctx_chunks, pows):
    p = S5_STATE
    nv = utl_ref.shape[1]
    ul = jnp.concatenate([utl_ref[q, v] for v in range(nv)], axis=1)
    uc = utc_ref[q]
    lg = LANES // nb
    bst, cst, mt = bst_ref[q], cst_ref[q], mt_ref[q]
    lp = lp_ref[q]

    def lam_of(d):
        def lam(n):
            k = pows.index(n)
            return lp[2 * d][:, k:k + 1], lp[2 * d + 1][:, k:k + 1]
        return lam

    lane = lax.broadcasted_iota(jnp.int32, (1, LANES), 1)
    sc = jnp.dot(bst, uc, preferred_element_type=_F32)
    sl = jnp.dot(bst, ul, preferred_element_type=_F32)
    yacc[q] = jnp.dot(mt, ul, preferred_element_type=_F32)
    posc = lane & (CTX_SEG - 1)
    hc = []
    for d in range(2):
        r0 = 2 * p * d
        hr, hi = _lane_scan(sc[r0:r0 + p], sc[r0 + p:r0 + 2 * p], lam_of(d), 1, posc, CTX_SEG, d == 1)
        hc += [hr, hi]
    posl = lane & (lg - 1)
    bidl = lane >> int(math.log2(lg))
    hl = []
    for d in range(2):
        lam = lam_of(d)
        r0 = 2 * p * d
        sr = [sl[r0:r0 + p, v * LANES:(v + 1) * LANES] for v in range(nv)]
        si = [sl[r0 + p:r0 + 2 * p, v * LANES:(v + 1) * LANES] for v in range(nv)]
        order = list(range(nv)) if d == 0 else list(range(nv - 1, -1, -1))
        ir, ii = {}, {}
        prev = None
        for v in order:
            if prev is None:
                ir[v], ii[v] = sr[v], si[v]
            else:
                mr, mi = _cmul(lam(1), ir[prev], ii[prev])
                ir[v], ii[v] = mr + sr[v], mi + si[v]
            prev = v
        entry = 0 if d == 0 else lg - 1
        src = (CTX_OFF + ctx_chunks) if d == 0 else (CTX_OFF - 1)
        h0r = jnp.zeros((p, LANES), _F32)
        h0i = jnp.zeros((p, LANES), _F32)
        for b in range(nb):
            m = (bidl == b) & (posl == entry)
            col = b * CTX_SEG + src
            h0r = jnp.where(m, hc[2 * d][:, col:col + 1], h0r)
            h0i = jnp.where(m, hc[2 * d + 1][:, col:col + 1], h0i)
        jr, ji = _cmul(lam(nv), h0r, h0i)
        er, ei = _lane_scan(ir[prev] + jr, ii[prev] + ji, lam, nv, posl, lg, d == 1)
        er, ei = er + h0r, ei + h0i
        hr, hi = {order[0]: er}, {order[0]: ei}
        for n, v in enumerate(order[1:], start=1):
            mr, mi = _cmul(lam(n), er, ei)
            hr[v], hi[v] = ir[order[n - 1]] + mr, ii[order[n - 1]] + mi
        hl += [jnp.concatenate([hr[v] for v in range(nv)], axis=1),
               jnp.concatenate([hi[v] for v in range(nv)], axis=1)]
    h = jnp.concatenate(hl, axis=0).astype(_BF16)
    y = yacc[q] + jnp.dot(cst, h, preferred_element_type=_F32)
    for v in range(nv):
        ytl_ref[q, v] = y[:, v * LANES:(v + 1) * LANES].astype(ytl_ref.dtype)
    if ytc_ref is not None:
        hcb = jnp.concatenate(hc, axis=0).astype(_BF16)
        yc = jnp.dot(mt, uc, preferred_element_type=_F32)
        ytc_ref[q] = (yc + jnp.dot(cst, hcb, preferred_element_type=_F32)).astype(ytc_ref.dtype)


def _s5_pows(nv, lg):
    pows = set(range(1, nv + 1))
    pows |= {nv << k for k in range(int(math.log2(lg)))}
    pows |= {1 << k for k in range(int(math.log2(CTX_SEG)))}
    return tuple(sorted(pows))


def _s5core(utl, utc, mt, bst, cst, lampow, nb, ctx_chunks, pows, want_yc):
    g, nv, rows, _ = utl.shape
    gq = S5_CORE_GROUPS
    out_shape = [jax.ShapeDtypeStruct((g, nv, rows, LANES), _BF16)]
    out_specs = [pl.BlockSpec((gq, nv, rows, LANES), lambda i: (i, 0, 0, 0))]
    if want_yc:
        out_shape.append(jax.ShapeDtypeStruct((g, rows, LANES), _BF16))
        out_specs.append(pl.BlockSpec((gq, rows, LANES), lambda i: (i, 0, 0)))
    res = pl.pallas_call(
        functools.partial(_s5core_kernel, nb=nb, ctx_chunks=ctx_chunks, pows=pows),
        out_shape=out_shape,
        grid=(g // gq,),
        in_specs=[pl.BlockSpec((gq, nv, rows, LANES), lambda i: (i, 0, 0, 0)),
                  pl.BlockSpec((gq, rows, LANES), lambda i: (i, 0, 0)),
                  pl.BlockSpec((gq, rows, rows), lambda i: (i, 0, 0)),
                  pl.BlockSpec((gq, S5_NSTATE, rows), lambda i: (i, 0, 0)),
                  pl.BlockSpec((gq, rows, S5_NSTATE), lambda i: (i, 0, 0)),
                  pl.BlockSpec((gq, 4, S5_STATE, LANES), lambda i: (i, 0, 0, 0))],
        out_specs=out_specs,
        scratch_shapes=[pltpu.VMEM((gq, rows, nv * LANES), _F32)],
        compiler_params=_cparams(("arbitrary",)),
        name="s5core_yc" if want_yc else "s5core",
    )(utl, utc, mt, bst, cst, lampow)
    return (res[0], res[1]) if want_yc else (res[0], None)


def _s5prep_kernel(*refs):
    for q in range(refs[0].shape[1]):
        _s5prep_group(q, *refs)


def _s5prep_group(q, acol_ref, arow_ref, ldt_ref, bre_ref, bim_ref, cre_ref, cim_ref, c1_ref, c2_ref,
                  esel_ref, etile_ref, e2_ref, e3_ref, pcol_ref, mt_ref, bst_ref, cst_ref, lp_ref):
    t_len, p = S5_CHUNK, S5_STATE
    lane = lax.broadcasted_iota(jnp.int32, (1, LANES), 1)
    n_lane = lane.astype(_F32)
    npow = pcol_ref.shape[0]
    strips = []
    for d in range(2):
        dt = jnp.exp(ldt_ref[d, q])
        a = acol_ref[d, q]
        ar, ai = a[:, 0:1], a[:, 1:2]
        dar, dai = ar * dt, ai * dt
        pm, pa = jnp.exp(n_lane * dar), n_lane * dai
        lr, li = pm * jnp.cos(pa), pm * jnp.sin(pa)
        lbr, lbi = lr[:, 1:2], li[:, 1:2]
        den = ar * ar + ai * ai
        nr, ni = lbr - 1.0, lbi
        kr = (nr * ar + ni * ai) / den
        ki = (ni * ar - nr * ai) / den
        bbr = kr * bre_ref[d, q] - ki * bim_ref[d, q]
        bbi = kr * bim_ref[d, q] + ki * bre_ref[d, q]
        wr = _select_cols(lr, esel_ref[d])
        wi = _select_cols(li, esel_ref[d])
        btr = _select_cols(bbr, etile_ref[...])
        bti = _select_cols(bbi, etile_ref[...])
        bsr, bsi = wr * btr - wi * bti, wr * bti + wi * btr
        bst_ref[q, (2 * d) * p:(2 * d + 1) * p, :] = bsr.astype(_BF16)
        bst_ref[q, (2 * d + 1) * p:(2 * d + 2) * p, :] = bsi.astype(_BF16)
        strips.append(_dot_split(cre_ref[d, q], bsr) - _dot_split(cim_ref[d, q], bsi))
        w1 = _select_rows(e2_ref[d], jnp.concatenate([lr, li], axis=0).T)
        w2 = _select_rows(e2_ref[d], jnp.concatenate([li, lr], axis=0).T)
        c1 = _select_rows(e3_ref[...], c1_ref[d, q])
        c2 = _select_rows(e3_ref[...], c2_ref[d, q])
        cst_ref[q, :, d * LANES:(d + 1) * LANES] = (c1 * w1 + c2 * w2).astype(_BF16)
        arow = arow_ref[d, q]
        m = pcol_ref[...] * float(t_len)
        qm, qa = jnp.exp(m * (arow[0:1] * dt)), m * (arow[1:2] * dt)
        packed = qm * jnp.where(lane < p, jnp.cos(qa), jnp.sin(qa))
        lpt = jnp.concatenate([packed, jnp.zeros((LANES - npow, LANES), _F32)], axis=0).T
        lp_ref[q, 2 * d] = lpt[:p]
        lp_ref[q, 2 * d + 1] = lpt[p:]
    rf, rb = strips
    lane_w = lax.broadcasted_iota(jnp.int32, (1, S5_ROWS), 1)
    for t in range(t_len):
        sf = (t_len - 1 - t) * S5_GROUP
        f = rf if sf == 0 else pltpu.roll(rf, S5_ROWS - sf, axis=1)
        f = jnp.where(lane_w < (t + 1) * S5_GROUP, f, 0.0)
        sb = t * S5_GROUP
        bk = rb if sb == 0 else pltpu.roll(rb, sb, axis=1)
        bk = jnp.where(lane_w >= sb, bk, 0.0)
        mt_ref[q, t * S5_GROUP:(t + 1) * S5_GROUP, :] = (f + bk).astype(_BF16)


def _s5prep_constants(pows):
    t_len, h = S5_CHUNK, S5_GROUP
    s_of = np.arange(S5_ROWS) // h
    h_of = np.arange(S5_ROWS) % h
    n128 = np.arange(LANES)
    esel = np.stack([n128[:, None] == (t_len - 1 - s_of)[None, :], n128[:, None] == s_of[None, :]])
    etile = np.arange(h)[:, None] == h_of[None, :]
    e2 = np.stack([(s_of + 1)[:, None] == n128[None, :], (t_len - s_of)[:, None] == n128[None, :]])
    e3 = h_of[:, None] == np.arange(h)[None, :]
    pcol = np.zeros((16, 1), np.float32)
    pcol[:len(pows), 0] = pows
    f = lambda m: jnp.asarray(m, dtype=_BF16)
    return f(esel), f(etile), f(e2), f(e3), jnp.asarray(pcol)


def _s5prep(a_re, a_im, log_dt, b_re, b_im, c_re, c_im, pows):
    g, p, h = S5_GROUPS, S5_STATE, S5_GROUP
    acol = jnp.stack([a_re, a_im], axis=-1)
    arow = jnp.stack([jnp.concatenate([a_re, a_re], -1),
                      jnp.concatenate([a_im, a_im], -1)], axis=2)
    ldt = log_dt.reshape(2, g, 1, 1)
    c1 = jnp.concatenate([c_re, -c_re], -1)
    c2 = jnp.concatenate([-c_im, -c_im], -1)
    esel, etile, e2, e3, pcol = _s5prep_constants(pows)

    gq = S5_PREP_GROUPS

    def per_g(*tail):
        return pl.BlockSpec((2, gq) + tail, lambda i: (0, i) + (0,) * len(tail))

    return pl.pallas_call(
        _s5prep_kernel,
        out_shape=[jax.ShapeDtypeStruct((g, S5_ROWS, S5_ROWS), _BF16),
                   jax.ShapeDtypeStruct((g, S5_NSTATE, S5_ROWS), _BF16),
                   jax.ShapeDtypeStruct((g, S5_ROWS, S5_NSTATE), _BF16),
                   jax.ShapeDtypeStruct((g, 4, p, LANES), _F32)],
        grid=(g // gq,),
        in_specs=[per_g(p, 2), per_g(2, LANES), per_g(1, 1), per_g(p, h), per_g(p, h),
                  per_g(h, p), per_g(h, p), per_g(h, LANES), per_g(h, LANES),
                  _resident(esel.shape), _resident(etile.shape), _resident(e2.shape), _resident(e3.shape),
                  _resident(pcol.shape)],
        out_specs=[pl.BlockSpec((gq, S5_ROWS, S5_ROWS), lambda i: (i, 0, 0)),
                   pl.BlockSpec((gq, S5_NSTATE, S5_ROWS), lambda i: (i, 0, 0)),
                   pl.BlockSpec((gq, S5_ROWS, S5_NSTATE), lambda i: (i, 0, 0)),
                   pl.BlockSpec((gq, 4, p, LANES), lambda i: (i, 0, 0, 0))],
        compiler_params=_cparams(("arbitrary",)),
        name="s5prep",
    )(acol, arow, ldt, b_re, b_im, c_re, c_im, c1, c2, esel, etile, e2, e3, pcol)


def _s5post_kernel(x_ref, yt_ref, mod_ref, g_ref, dsk_ref, gw_ref, gb_ref, o_ref, *, k, lanes):
    seg, off, mod_rows = lanes
    nb, lg = x_ref.shape[0], x_ref.shape[1]
    d = D_MODEL
    gn = g_ref[...]
    dsk = dsk_ref[...]
    c = math.sqrt(2.0 / math.pi)
    mods = [_mod_slices(mod_ref[mod_rows[b]])[:3] for b in range(nb)]
    ys = [yt_ref[:, 0, j * S5_GROUP:(j + 1) * S5_GROUP, :].astype(_F32).reshape(d, LANES).T
          for j in range(k)]
    rows = lg * k
    xs, zs = [], []
    for b in range(nb):
        sh1, sc1, _ = mods[b]
        r0 = b * seg + off
        yb = jnp.stack([ys[j][r0:r0 + lg] for j in range(k)], axis=0)
        yb = jnp.swapaxes(yb, 0, 1).reshape(rows, d)
        xf = x_ref[b].reshape(rows, d)
        ms = jnp.mean(xf * xf, axis=-1, keepdims=True)
        pre = (xf * lax.rsqrt(ms + EPS)) * ((gn * (1.0 + sc1)) * dsk) + (yb + sh1 * dsk)
        xs.append(xf)
        zs.append(pre + pre * jnp.tanh(pre * (c + (c * 0.044715) * (pre * pre))))
    z2 = jnp.concatenate(zs, axis=0)
    th = jnp.tanh(jnp.dot(z2.astype(_BF16), gw_ref[0], preferred_element_type=_F32) + gb_ref[...])
    for b in range(nb):
        q = (0.25 * mods[b][2]) * z2[b * rows:(b + 1) * rows]
        o_ref[b] = (xs[b] + (q * th[b * rows:(b + 1) * rows] + q)).reshape(lg, k, d)


def _s5post(x4, yt, mod, g, dsk, gw, layer, gb, lanes):
    b, lg, per, d = x4.shape
    k = S5_POST_TOK
    blocks_per_v = S5_CHUNK // k
    tok = pl.BlockSpec((b, lg, k, d), lambda r: (0, 0, r, 0))
    vec = pl.BlockSpec((1, d), lambda r: (0, 0))
    return pl.pallas_call(
        functools.partial(_s5post_kernel, k=k, lanes=lanes),
        out_shape=jax.ShapeDtypeStruct((b, lg, per, d), _F32),
        grid=(per // k,),
        in_specs=[tok,
                  pl.BlockSpec((S5_GROUPS, 1, k * S5_GROUP, LANES),
                               lambda r: (0, r // blocks_per_v, r % blocks_per_v, 0)),
                  pl.BlockSpec((8, 1, 6 * d), lambda r: (0, 0, 0)),
                  vec, vec, _resident_layer((d, d), layer), vec],
        out_specs=tok,
        compiler_params=_cparams(("arbitrary",)),
        name="s5post",
    )(x4, yt, mod, g, dsk, gw, gb)


def _mlp_kernel(x_ref, mod_ref, g_ref, w1_ref, b1_ref, w2_ref, b2_ref, fg_ref, o_ref, *, final):
    sh2, sc2, g2 = _mod_slices(mod_ref[0])[3:]
    xf = x_ref[0]
    xn = _rms_mod(xf, g_ref[...], sh2, sc2).astype(_BF16)
    acc = jnp.zeros(xf.shape, _F32)
    fc = D_MODEL
    for j in range(D_FF // fc):
        a = jnp.dot(xn, w1_ref[0, :, j * fc:(j + 1) * fc], preferred_element_type=_F32)
        a = jnp.maximum(a + b1_ref[:, j * fc:(j + 1) * fc], 0.0)
        acc = acc + jnp.dot((a * a).astype(_BF16), w2_ref[0, j * fc:(j + 1) * fc, :],
                            preferred_element_type=_F32)
    out = xf + g2 * (acc + b2_ref[...])
    if final:
        ms = jnp.mean(out * out, axis=-1, keepdims=True)
        out = out * lax.rsqrt(ms + EPS) * fg_ref[...]
    o_ref[0] = out


def _mlp(x, mod, mod_row, g, w1, b1, w2, b2, layer, fg, tt, final):
    b, n, d = x.shape
    row = (lambda bi: bi) if mod_row is None else (lambda bi: mod_row)
    tok = pl.BlockSpec((1, tt, d), lambda bi, t: (bi, t, 0))
    vec = pl.BlockSpec((1, d), lambda bi, t: (0, 0))
    return pl.pallas_call(
        functools.partial(_mlp_kernel, final=final),
        out_shape=jax.ShapeDtypeStruct((b, n, d), _F32),
        grid=(b, n // tt),
        in_specs=[tok, pl.BlockSpec((1, 1, 6 * d), lambda bi, t: (row(bi), 0, 0)), vec,
                  _resident_layer((d, D_FF), layer), pl.BlockSpec((1, D_FF), lambda bi, t: (0, 0)),
                  _resident_layer((D_FF, d), layer), vec, vec],
        out_specs=tok,
        compiler_params=_cparams(("arbitrary", "arbitrary")),
        name="mlp_final" if final else "mlp",
    )(x, mod, g, w1, b1, w2, b2, fg)


def _window_in_rows(w):
    need = POOL_TILE_ROWS + w - 1
    return -(-need // POOL_TILE_ROWS) * POOL_TILE_ROWS


def _grid_window_matrices():
    to = np.arange(POOL_TILE_ROWS * GRID_W)
    ro, co = to // GRID_W, to % GRID_W
    mats = []
    for w in POOL_WINDOWS:
        ti = np.arange(_window_in_rows(w) * GRID_W)
        ri, ci = ti // GRID_W - w // 2, ti % GRID_W
        m = ((ri[None, :] >= (ro - w // 2)[:, None]) & (ri[None, :] < (ro + w - w // 2)[:, None])
             & (ci[None, :] >= (co - w // 2)[:, None]) & (ci[None, :] < (co + w - w // 2)[:, None]))
        mats.append(jnp.asarray(m, dtype=_BF16))
    return mats


def _grid_inverse_counts(rows):
    t = np.arange(rows * GRID_W)
    r, c = t // GRID_W, t % GRID_W
    tabs = []
    for w in POOL_WINDOWS:
        rc = np.minimum(r + w - w // 2, rows) - np.maximum(r - w // 2, 0)
        cc = np.minimum(c + w - w // 2, GRID_W) - np.maximum(c - w // 2, 0)
        tabs.append(1.0 / (rc * cc))
    return jnp.asarray(np.stack(tabs), dtype=_F32)


def _seq_window_matrices(n):
    t = np.arange(n)
    mats = []
    for w in POOL_WINDOWS:
        lo, hi = np.maximum(t - w // 2, 0), np.minimum(t + w - w // 2, n)
        mats.append((t[None, :] >= lo[:, None]) & (t[None, :] < hi[:, None]))
    return jnp.asarray(np.stack(mats), dtype=_BF16)


POOL_BLOCK_ROWS = 16
POOL_HALO_ROWS = 8
POOL_TILE_ROWS = 4


def _pool_kernel(xm_ref, xb_ref, mod_ref, g_ref, a2_ref, a4_ref, a8_ref, a16_ref, pw_ref, ps_ref, inv_ref,
                 o_ref, tail):
    i = pl.program_id(0)
    b = pl.program_id(1)
    nblk = pl.num_programs(0)
    sh1, sc1, g1 = _mod_slices(mod_ref[0])[:3]
    gn = g_ref[...]
    halo = POOL_HALO_ROWS * GRID_W
    main = POOL_BLOCK_ROWS * GRID_W
    tile = POOL_TILE_ROWS * GRID_W
    xm = xm_ref[0]
    xn_m = _rms_mod(xm, gn, sh1, sc1)
    xn_mb = xn_m.astype(_BF16)
    @pl.when(i == 0)
    def _():
        tail[b] = jnp.zeros((halo, D_MODEL), _BF16)

    xn_t = tail[b]
    xn_b = jnp.where(i < nblk - 1, _rms_mod(xb_ref[0], gn, sh1, sc1), 0.0).astype(_BF16)
    tail[b] = xn_mb[main - halo:]
    xn_ext = jnp.concatenate([xn_t, xn_mb, xn_b], axis=0)
    for gi, (w, a_ref) in enumerate(zip(POOL_WINDOWS, (a2_ref, a4_ref, a8_ref, a16_ref))):
        ch = slice(gi * POOL_CH, (gi + 1) * POOL_CH)
        a = a_ref[...]
        span = a.shape[1]
        tots = []
        for k in range(main // tile):
            start = halo + k * tile - (w // 2) * GRID_W
            tots.append(jnp.dot(a, xn_ext[start:start + span, ch], preferred_element_type=_F32))
        tot = jnp.concatenate(tots, axis=0)
        inv = inv_ref[gi]
        p = tot * jnp.concatenate([inv] * (POOL_CH // LANES), axis=1) - xn_m[:, ch]
        y = jnp.dot(p.astype(_BF16), pw_ref[0, gi], preferred_element_type=_F32) * ps_ref[:, ch]
        o_ref[0, :, ch] = xm[:, ch] + g1[:, ch] * y


def _pool_grid(x, mod, g, pw, layer, ps):
    b, n, d = x.shape
    rows = n // GRID_W
    main = POOL_BLOCK_ROWS * GRID_W
    halo = POOL_HALO_ROWS * GRID_W
    nblk = n // main
    ratio = main // halo
    nh = n // halo
    amats = _grid_window_matrices()
    assert all(w // 2 <= POOL_HALO_ROWS and _window_in_rows(w) - w // 2 <= POOL_TILE_ROWS + POOL_HALO_ROWS
               for w in POOL_WINDOWS)
    inv = jnp.broadcast_to(_grid_inverse_counts(rows)[:, :, None], (len(POOL_WINDOWS), n, LANES))
    vec = pl.BlockSpec((1, d), lambda i, bi: (0, 0))
    return pl.pallas_call(
        _pool_kernel,
        out_shape=jax.ShapeDtypeStruct((b, n, d), _F32),
        grid=(nblk, b),
        in_specs=[pl.BlockSpec((1, main, d), lambda i, bi: (bi, i, 0)),
                  pl.BlockSpec((1, halo, d), lambda i, bi: (bi, jnp.minimum((i + 1) * ratio, nh - 1), 0)),
                  pl.BlockSpec((1, 1, 6 * d), lambda i, bi: (bi, 0, 0)),
                  vec,
                  *[_resident(a.shape) for a in amats],
                  _resident_layer((len(POOL_WINDOWS), POOL_CH, POOL_CH), layer),
                  vec,
                  pl.BlockSpec((len(POOL_WINDOWS), main, LANES), lambda i, bi: (0, i, 0))],
        out_specs=pl.BlockSpec((1, main, d), lambda i, bi: (bi, i, 0)),
        scratch_shapes=[pltpu.VMEM((b, halo, d), _BF16)],
        compiler_params=_cparams(("arbitrary", "arbitrary")),
        name="pool_grid",
    )(x, x, mod, g, *amats, pw, ps, inv)


def _poolseq_kernel(x_ref, mod_ref, g_ref, aseq_ref, pw_ref, ps_ref, o_ref):
    sh1, sc1, g1 = _mod_slices(mod_ref[0])[:3]
    xf = x_ref[0]
    n = xf.shape[0]
    xn = _rms_mod(xf, g_ref[...], sh1, sc1)
    xb = xn.astype(_BF16)
    t = lax.broadcasted_iota(jnp.int32, (n, POOL_CH), 0)
    for gi, w in enumerate(POOL_WINDOWS):
        ch = slice(gi * POOL_CH, (gi + 1) * POOL_CH)
        tot = jnp.dot(aseq_ref[gi], xb[:, ch], preferred_element_type=_F32)
        cnt = jnp.minimum(t + (w - w // 2), n) - jnp.maximum(t - w // 2, 0)
        p = tot / cnt.astype(_F32) - xn[:, ch]
        y = jnp.dot(p.astype(_BF16), pw_ref[0, gi], preferred_element_type=_F32) * ps_ref[:, ch]
        o_ref[0, :, ch] = xf[:, ch] + g1[:, ch] * y


def _pool_seq(x, mod, mod_row, g, pw, layer, ps):
    b, n, d = x.shape
    aseq = _seq_window_matrices(n)
    vec = pl.BlockSpec((1, d), lambda bi: (0, 0))
    return pl.pallas_call(
        _poolseq_kernel,
        out_shape=jax.ShapeDtypeStruct((b, n, d), _F32),
        grid=(b,),
        in_specs=[pl.BlockSpec((1, n, d), lambda bi: (bi, 0, 0)),
                  pl.BlockSpec((1, 1, 6 * d), lambda bi: (mod_row, 0, 0)),
                  vec,
                  _resident((len(POOL_WINDOWS), n, n)),
                  _resident_layer((len(POOL_WINDOWS), POOL_CH, POOL_CH), layer),
                  vec],
        out_specs=pl.BlockSpec((1, n, d), lambda bi: (bi, 0, 0)),
        compiler_params=_cparams(("arbitrary",)),
        name="pool_seq",
    )(x, mod, g, aseq, pw, ps)


def kernel(x, c, ctx, c_ctx, ada_w, ada_b, norm1_g, norm2_g, s5_a_re, s5_a_im, s5_log_dt, s5_b_re, s5_b_im, s5_c_re, s5_c_im, s5_d, s5_glu_w, s5_glu_b, pool_w, pool_scale, mlp_w1, mlp_b1, mlp_w2, mlp_b2, final_g):
    bsz, n_tok, d = x.shape
    n_ctx = ctx.shape[1]
    depth = ada_w.shape[0]
    assert d == D_MODEL and n_tok % (POOL_BLOCK_ROWS * GRID_W) == 0
    assert LANES % bsz == 0 and bsz * CTX_SEG <= LANES and bsz < 8
    lat_chunks = n_tok // S5_CHUNK
    ctx_chunks = n_ctx // S5_CHUNK
    assert n_tok % S5_CHUNK == 0 and n_ctx % S5_CHUNK == 0 and (bsz * lat_chunks) % LANES == 0
    assert CTX_OFF >= 2 and CTX_OFF + ctx_chunks + 2 <= CTX_SEG
    nv = bsz * lat_chunks // LANES
    lg = LANES // bsz
    pows = _s5_pows(nv, lg)
    lat_lanes = (lg, 0, tuple(range(bsz)))
    ctx_lanes = (CTX_SEG, CTX_OFF, (bsz,) * bsz)
    last_ctx_reader = ((depth - 1) // N_MIXERS) * N_MIXERS
    ctx_row = bsz
    tt = min(1024, n_tok)
    tt_ctx = min(512, n_ctx)

    cc = jnp.zeros((8, d), _F32).at[:bsz].set(c).at[ctx_row].set(c_ctx)
    mods = _modulation(cc, ada_w, ada_b)
    w1 = mlp_w1.astype(_BF16)
    w2 = mlp_w2.astype(_BF16)
    glu_w4 = (0.25 * s5_glu_w).astype(_BF16)
    pw = pool_w.astype(_BF16)

    h_ctx = ctx
    for i in range(depth):
        ctx_in = i <= last_ctx_reader
        ctx_out = i < last_ctx_reader
        j = i // N_MIXERS
        mod = mods[i].reshape(8, 1, 6 * d)
        g1n = norm1_g[i].reshape(1, d)
        g2n = norm2_g[i].reshape(1, d)
        if i % N_MIXERS == 0:
            mt, bst, cst, lampow = _s5prep(s5_a_re[j], s5_a_im[j], s5_log_dt[j], s5_b_re[j], s5_b_im[j],
                                           s5_c_re[j], s5_c_im[j], pows)
            dsk = s5_d[j].reshape(1, d)
            gb2 = 0.5 * s5_glu_b[j].reshape(1, d)
            c4 = h_ctx.reshape(bsz, ctx_chunks, S5_CHUNK, d)
            if ctx_in:
                utc = _s5pre(c4, mod, g1n, ctx_lanes).reshape(S5_GROUPS, S5_ROWS, LANES)
            else:
                utc = jnp.zeros((S5_GROUPS, S5_ROWS, LANES), _BF16)
            x4 = x.reshape(bsz, lg, n_tok // lg, d)
            utl = _s5pre(x4, mod, g1n, lat_lanes)
            ytl, ytc = _s5core(utl, utc, mt, bst, cst, lampow, bsz, ctx_chunks, pows, ctx_out)
            x = _s5post(x4, ytl, mod, g1n, dsk, glu_w4, j, gb2, lat_lanes).reshape(bsz, n_tok, d)
            if ctx_out:
                ytc4 = ytc.reshape(S5_GROUPS, 1, S5_ROWS, LANES)
                h_ctx = _s5post(c4, ytc4, mod, g1n, dsk, glu_w4, j, gb2, ctx_lanes).reshape(bsz, n_ctx, d)
        else:
            ps = pool_scale[j].reshape(1, d)
            x = _pool_grid(x, mod, g1n, pw, j, ps)
            if ctx_out:
                h_ctx = _pool_seq(h_ctx, mod, ctx_row, g1n, pw, j, ps)
        final = i == depth - 1
        x = _mlp(x, mod, None, g2n, w1, mlp_b1[i].reshape(1, D_FF), w2, mlp_b2[i].reshape(1, d), i,
                 final_g.reshape(1, d), tt, final)
        if ctx_out:
            h_ctx = _mlp(h_ctx, mod, ctx_row, g2n, w1, mlp_b1[i].reshape(1, D_FF), w2,
                         mlp_b2[i].reshape(1, d), i, final_g.reshape(1, d), tt_ctx, False)
    return x
```

```python
import functools
import math

import numpy as np
import jax
import jax.numpy as jnp
from jax import lax
from jax.experimental import pallas as pl
from jax.experimental.pallas import tpu as pltpu

D_MODEL = 1024
GRID_W = 64
S5_GROUP = 16
S5_GROUPS = D_MODEL // S5_GROUP
S5_STATE = 64
POOL_WINDOWS = (2, 4, 8, 16)
POOL_CH = D_MODEL // len(POOL_WINDOWS)
D_FF = 4 * D_MODEL
N_MIXERS = 2
EPS = 1e-6
LANES = 128

S5_CHUNK = 32
S5_ROWS = S5_CHUNK * S5_GROUP
S5_NSTATE = 4 * S5_STATE
CTX_SEG = 32
CTX_OFF = 8
S5_PRE_TOK = 16
S5_POST_TOK = 8
S5_CORE_GROUPS = 4
S5_PREP_GROUPS = 8
V7X_VMEM_LIMIT_BYTES = 56 * 1024 * 1024

_F32 = jnp.float32
_BF16 = jnp.bfloat16


def _cparams(sem):
    return pltpu.CompilerParams(dimension_semantics=sem, vmem_limit_bytes=V7X_VMEM_LIMIT_BYTES)


def _resident(shape):
    nd = len(shape)
    return pl.BlockSpec(shape, lambda *_: (0,) * nd, pipeline_mode=pl.Buffered(1))


def _resident_layer(shape, layer):
    nd = len(shape)
    return pl.BlockSpec((1,) + tuple(shape), lambda *_: (layer,) + (0,) * nd, pipeline_mode=pl.Buffered(1))


def _mod_kernel(cc_ref, w_ref, b_ref, o_ref):
    c = cc_ref[...]
    o_ref[0] = _dot_split(c * jax.nn.sigmoid(c), w_ref[0]) + b_ref[0]


def _modulation(cc, ada_w, ada_b):
    depth, d, n = ada_w.shape
    tn = 1536
    return pl.pallas_call(
        _mod_kernel,
        out_shape=jax.ShapeDtypeStruct((depth, 8, n), _F32),
        grid=(depth, n // tn),
        in_specs=[pl.BlockSpec((8, d), lambda i, j: (0, 0)),
                  pl.BlockSpec((1, d, tn), lambda i, j: (i, 0, j)),
                  pl.BlockSpec((1, 1, tn), lambda i, j: (i, 0, j))],
        out_specs=pl.BlockSpec((1, 8, tn), lambda i, j: (i, 0, j)),
        compiler_params=_cparams(("arbitrary", "arbitrary")),
        name="modulation",
    )(cc, ada_w, ada_b.reshape(depth, 1, n))


def _rms_mod(xf, g, shift, scale):
    ms = jnp.mean(xf * xf, axis=-1, keepdims=True)
    return (xf * lax.rsqrt(ms + EPS)) * (g * (1.0 + scale)) + shift


def _mod_slices(m):
    d = D_MODEL
    return tuple(m[:, k * d:(k + 1) * d] for k in range(6))


def _cmul(a, yr, yi):
    ar, ai = a
    return ar * yr - ai * yi, ar * yi + ai * yr


def _bf16_pair(x):
    hi = x.astype(_BF16)
    lo = (x - hi.astype(_F32)).astype(_BF16)
    return hi, lo


def _select_cols(x, sel):
    return sum(jnp.dot(part, sel, preferred_element_type=_F32) for part in _bf16_pair(x))


def _select_rows(sel, x):
    return sum(jnp.dot(sel, part, preferred_element_type=_F32) for part in _bf16_pair(x))


def _dot_split(a, b):
    (ah, al), (bh, bl) = _bf16_pair(a), _bf16_pair(b)
    return (jnp.dot(ah, bh, preferred_element_type=_F32) + jnp.dot(ah, bl, preferred_element_type=_F32)
            + jnp.dot(al, bh, preferred_element_type=_F32))


def _s5pre_kernel(x_ref, mod_ref, g_ref, ut_ref, *, k, lanes):
    seg, off, mod_rows = lanes
    nb, cnt = x_ref.shape[0], x_ref.shape[1]
    gn = g_ref[...]
    mods = [_mod_slices(mod_ref[mod_rows[b]])[:2] for b in range(nb)]
    xs = [jnp.swapaxes(x_ref[b], 0, 1) for b in range(nb)]
    pads = (off, seg - off - cnt)
    for j in range(k):
        parts = []
        for b in range(nb):
            piece = [_rms_mod(xs[b][j], gn, *mods[b])]
            if pads[0]:
                piece.insert(0, jnp.zeros((pads[0], D_MODEL), _F32))
            if pads[1]:
                piece.append(jnp.zeros((pads[1], D_MODEL), _F32))
            parts += piece
        if LANES - nb * seg:
            parts.append(jnp.zeros((LANES - nb * seg, D_MODEL), _F32))
        xn = jnp.concatenate(parts, axis=0)
        ut_ref[:, 0, j * S5_GROUP:(j + 1) * S5_GROUP, :] = (
            xn.T.astype(_BF16).reshape(S5_GROUPS, S5_GROUP, LANES))


def _s5pre(x4, mod, g, lanes):
    b, cnt, per, d = x4.shape
    k = S5_PRE_TOK
    blocks_per_v = S5_CHUNK // k
    return pl.pallas_call(
        functools.partial(_s5pre_kernel, k=k, lanes=lanes),
        out_shape=jax.ShapeDtypeStruct((S5_GROUPS, per // S5_CHUNK, S5_ROWS, LANES), _BF16),
        grid=(per // k,),
        in_specs=[pl.BlockSpec((b, cnt, k, d), lambda r: (0, 0, r, 0)),
                  pl.BlockSpec((8, 1, 6 * d), lambda r: (0, 0, 0)),
                  pl.BlockSpec((1, d), lambda r: (0, 0))],
        out_specs=pl.BlockSpec((S5_GROUPS, 1, k * S5_GROUP, LANES),
                               lambda r: (0, r // blocks_per_v, r % blocks_per_v, 0)),
        compiler_params=_cparams(("arbitrary",)),
        name="s5pre",
    )(x4, mod, g)


def _lane_shift(v, dist, pos, seg, reverse):
    n = v.shape[1]
    if reverse:
        return jnp.where(pos < seg - dist, pltpu.roll(v, n - dist, axis=1), 0.0)
    return jnp.where(pos >= dist, pltpu.roll(v, dist, axis=1), 0.0)


def _lane_scan(xr, xi, lam, unit, pos, seg, reverse):
    k = 0
    while (1 << k) < seg:
        dist = 1 << k
        mr, mi = _cmul(lam(unit * dist), _lane_shift(xr, dist, pos, seg, reverse),
                       _lane_shift(xi, dist, pos, seg, reverse))
        xr, xi = xr + mr, xi + mi
        k += 1
    return _lane_shift(xr, 1, pos, seg, reverse), _lane_shift(xi, 1, pos, seg, reverse)


def _s5core_kernel(utl_ref, utc_ref, mt_ref, bst_ref, cst_ref, lp_ref, ytl_ref, *rest,
                   nb, ctx_chunks, pows):
    for q in range(utl_ref.shape[0]):
        _s5core_group(q, utl_ref, utc_ref, mt_ref, bst_ref, cst_ref, lp_ref, ytl_ref,
                      rest[0] if len(rest) == 2 else None, rest[-1], nb, ctx_chunks, pows)


def _s5core_group(q, utl_ref, utc_ref, mt_ref, bst_ref, cst_ref, lp_ref, ytl_ref, ytc_ref, yacc,
                  nb, ctx_chunks, pows):
    p = S5_STATE
    nv = utl_ref.shape[1]
    ul = jnp.concatenate([utl_ref[q, v] for v in range(nv)], axis=1)
    uc = utc_ref[q]
    lg = LANES // nb
    bst, cst, mt = bst_ref[q], cst_ref[q], mt_ref[q]
    lp = lp_ref[q]

    def lam_of(d):
        def lam(n):
            k = pows.index(n)
            return lp[2 * d][:, k:k + 1], lp[2 * d + 1][:, k:k + 1]
        return lam

    lane = lax.broadcasted_iota(jnp.int32, (1, LANES), 1)
    sc = jnp.dot(bst, uc, preferred_element_type=_F32)
    sl = jnp.dot(bst, ul, preferred_element_type=_F32)
    yacc[q] = jnp.dot(mt, ul, preferred_element_type=_F32)
    posc = lane & (CTX_SEG - 1)
    hc = []
    for d in range(2):
        r0 = 2 * p * d
        hr, hi = _lane_scan(sc[r0:r0 + p], sc[r0 + p:r0 + 2 * p], lam_of(d), 1, posc, CTX_SEG, d == 1)
        hc += [hr, hi]
    posl = lane & (lg - 1)
    bidl = lane >> int(math.log2(lg))
    hl = []
    for d in range(2):
        lam = lam_of(d)
        r0 = 2 * p * d
        sr = [sl[r0:r0 + p, v * LANES:(v + 1) * LANES] for v in range(nv)]
        si = [sl[r0 + p:r0 + 2 * p, v * LANES:(v + 1) * LANES] for v in range(nv)]
        order = list(range(nv)) if d == 0 else list(range(nv - 1, -1, -1))
        ir, ii = {}, {}
        prev = None
        for v in order:
            if prev is None:
                ir[v], ii[v] = sr[v], si[v]
            else:
                mr, mi = _cmul(lam(1), ir[prev], ii[prev])
                ir[v], ii[v] = mr + sr[v], mi + si[v]
            prev = v
        entry = 0 if d == 0 else lg - 1
        src = (CTX_OFF + ctx_chunks) if d == 0 else (CTX_OFF - 1)
        h0r = jnp.zeros((p, LANES), _F32)
        h0i = jnp.zeros((p, LANES), _F32)
        for b in range(nb):
            m = (bidl == b) & (posl == entry)
            col = b * CTX_SEG + src
            h0r = jnp.where(m, hc[2 * d][:, col:col + 1], h0r)
            h0i = jnp.where(m, hc[2 * d + 1][:, col:col + 1], h0i)
        jr, ji = _cmul(lam(nv), h0r, h0i)
        er, ei = _lane_scan(ir[prev] + jr, ii[prev] + ji, lam, nv, posl, lg, d == 1)
        er, ei = er + h0r, ei + h0i
        hr, hi = {order[0]: er}, {order[0]: ei}
        for n, v in enumerate(order[1:], start=1):
            mr, mi = _cmul(lam(n), er, ei)
            hr[v], hi[v] = ir[order[n - 1]] + mr, ii[order[n - 1]] + mi
        hl += [jnp.concatenate([hr[v] for v in range(nv)], axis=1),
               jnp.concatenate([hi[v] for v in range(nv)], axis=1)]
    h = jnp.concatenate(hl, axis=0).astype(_BF16)
    y = yacc[q] + jnp.dot(cst, h, preferred_element_type=_F32)
    for v in range(nv):
        ytl_ref[q, v] = y[:, v * LANES:(v + 1) * LANES].astype(ytl_ref.dtype)
    if ytc_ref is not None:
        hcb = jnp.concatenate(hc, axis=0).astype(_BF16)
        yc = jnp.dot(mt, uc, preferred_element_type=_F32)
        ytc_ref[q] = (yc + jnp.dot(cst, hcb, preferred_element_type=_F32)).astype(ytc_ref.dtype)


def _s5_pows(nv, lg):
    pows = set(range(1, nv + 1))
    pows |= {nv << k for k in range(int(math.log2(lg)))}
    pows |= {1 << k for k in range(int(math.log2(CTX_SEG)))}
    return tuple(sorted(pows))


def _s5core(utl, utc, mt, bst, cst, lampow, nb, ctx_chunks, pows, want_yc):
    g, nv, rows, _ = utl.shape
    gq = S5_CORE_GROUPS
    out_shape = [jax.ShapeDtypeStruct((g, nv, rows, LANES), _BF16)]
    out_specs = [pl.BlockSpec((gq, nv, rows, LANES), lambda i: (i, 0, 0, 0))]
    if want_yc:
        out_shape.append(jax.ShapeDtypeStruct((g, rows, LANES), _BF16))
        out_specs.append(pl.BlockSpec((gq, rows, LANES), lambda i: (i, 0, 0)))
    res = pl.pallas_call(
        functools.partial(_s5core_kernel, nb=nb, ctx_chunks=ctx_chunks, pows=pows),
        out_shape=out_shape,
        grid=(g // gq,),
        in_specs=[pl.BlockSpec((gq, nv, rows, LANES), lambda i: (i, 0, 0, 0)),
                  pl.BlockSpec((gq, rows, LANES), lambda i: (i, 0, 0)),
                  pl.BlockSpec((gq, rows, rows), lambda i: (i, 0, 0)),
                  pl.BlockSpec((gq, S5_NSTATE, rows), lambda i: (i, 0, 0)),
                  pl.BlockSpec((gq, rows, S5_NSTATE), lambda i: (i, 0, 0)),
                  pl.BlockSpec((gq, 4, S5_STATE, LANES), lambda i: (i, 0, 0, 0))],
        out_specs=out_specs,
        scratch_shapes=[pltpu.VMEM((gq, rows, nv * LANES), _F32)],
        compiler_params=_cparams(("arbitrary",)),
        name="s5core_yc" if want_yc else "s5core",
    )(utl, utc, mt, bst, cst, lampow)
    return (res[0], res[1]) if want_yc else (res[0], None)


def _s5prep_kernel(*refs):
    for q in range(refs[0].shape[1]):
        _s5prep_group(q, *refs)


def _s5prep_group(q, acol_ref, arow_ref, ldt_ref, bre_ref, bim_ref, cre_ref, cim_ref, c1_ref, c2_ref,
                  esel_ref, etile_ref, e2_ref, e3_ref, pcol_ref, mt_ref, bst_ref, cst_ref, lp_ref):
    t_len, p = S5_CHUNK, S5_STATE
    lane = lax.broadcasted_iota(jnp.int32, (1, LANES), 1)
    n_lane = lane.astype(_F32)
    npow = pcol_ref.shape[0]
    strips = []
    for d in range(2):
        dt = jnp.exp(ldt_ref[d, q])
        a = acol_ref[d, q]
        ar, ai = a[:, 0:1], a[:, 1:2]
        dar, dai = ar * dt, ai * dt
        pm, pa = jnp.exp(n_lane * dar), n_lane * dai
        lr, li = pm * jnp.cos(pa), pm * jnp.sin(pa)
        lbr, lbi = lr[:, 1:2], li[:, 1:2]
        den = ar * ar + ai * ai
        nr, ni = lbr - 1.0, lbi
        kr = (nr * ar + ni * ai) / den
        ki = (ni * ar - nr * ai) / den
        bbr = kr * bre_ref[d, q] - ki * bim_ref[d, q]
        bbi = kr * bim_ref[d, q] + ki * bre_ref[d, q]
        wr = _select_cols(lr, esel_ref[d])
        wi = _select_cols(li, esel_ref[d])
        btr = _select_cols(bbr, etile_ref[...])
        bti = _select_cols(bbi, etile_ref[...])
        bsr, bsi = wr * btr - wi * bti, wr * bti + wi * btr
        bst_ref[q, (2 * d) * p:(2 * d + 1) * p, :] = bsr.astype(_BF16)
        bst_ref[q, (2 * d + 1) * p:(2 * d + 2) * p, :] = bsi.astype(_BF16)
        strips.append(_dot_split(cre_ref[d, q], bsr) - _dot_split(cim_ref[d, q], bsi))
        w1 = _select_rows(e2_ref[d], jnp.concatenate([lr, li], axis=0).T)
        w2 = _select_rows(e2_ref[d], jnp.concatenate([li, lr], axis=0).T)
        c1 = _select_rows(e3_ref[...], c1_ref[d, q])
        c2 = _select_rows(e3_ref[...], c2_ref[d, q])
        cst_ref[q, :, d * LANES:(d + 1) * LANES] = (c1 * w1 + c2 * w2).astype(_BF16)
        arow = arow_ref[d, q]
        m = pcol_ref[...] * float(t_len)
        qm, qa = jnp.exp(m * (arow[0:1] * dt)), m * (arow[1:2] * dt)
        packed = qm * jnp.where(lane < p, jnp.cos(qa), jnp.sin(qa))
        lpt = jnp.concatenate([packed, jnp.zeros((LANES - npow, LANES), _F32)], axis=0).T
        lp_ref[q, 2 * d] = lpt[:p]
        lp_ref[q, 2 * d + 1] = lpt[p:]
    rf, rb = strips
    lane_w = lax.broadcasted_iota(jnp.int32, (1, S5_ROWS), 1)
    zero_lag = (t_len - 1) * S5_GROUP
    left = rf + jnp.where(lane_w >= zero_lag, pltpu.roll(rb, zero_lag, axis=1), 0.0)
    right = jnp.where(lane_w < zero_lag, pltpu.roll(rb, S5_ROWS - S5_GROUP, axis=1), 0.0)
    strip = jnp.concatenate([left, right], axis=1)
    for t in range(t_len):
        sh = (t_len - 1 - t) * S5_GROUP
        win = strip if sh == 0 else pltpu.roll(strip, 2 * S5_ROWS - sh, axis=1)
        mt_ref[q, t * S5_GROUP:(t + 1) * S5_GROUP, :] = win[:, :S5_ROWS].astype(_BF16)


def _s5prep_constants(pows):
    t_len, h = S5_CHUNK, S5_GROUP
    s_of = np.arange(S5_ROWS) // h
    h_of = np.arange(S5_ROWS) % h
    n128 = np.arange(LANES)
    esel = np.stack([n128[:, None] == (t_len - 1 - s_of)[None, :], n128[:, None] == s_of[None, :]])
    etile = np.arange(h)[:, None] == h_of[None, :]
    e2 = np.stack([(s_of + 1)[:, None] == n128[None, :], (t_len - s_of)[:, None] == n128[None, :]])
    e3 = h_of[:, None] == np.arange(h)[None, :]
    pcol = np.zeros((16, 1), np.float32)
    pcol[:len(pows), 0] = pows
    f = lambda m: jnp.asarray(m, dtype=_BF16)
    return f(esel), f(etile), f(e2), f(e3), jnp.asarray(pcol)


def _s5prep(a_re, a_im, log_dt, b_re, b_im, c_re, c_im, pows):
    g, p, h = S5_GROUPS, S5_STATE, S5_GROUP
    acol = jnp.stack([a_re, a_im], axis=-1)
    arow = jnp.stack([jnp.concatenate([a_re, a_re], -1),
                      jnp.concatenate([a_im, a_im], -1)], axis=2)
    ldt = log_dt.reshape(2, g, 1, 1)
    c1 = jnp.concatenate([c_re, -c_re], -1)
    c2 = jnp.concatenate([-c_im, -c_im], -1)
    esel, etile, e2, e3, pcol = _s5prep_constants(pows)

    gq = S5_PREP_GROUPS

    def per_g(*tail):
        return pl.BlockSpec((2, gq) + tail, lambda i: (0, i) + (0,) * len(tail))

    return pl.pallas_call(
        _s5prep_kernel,
        out_shape=[jax.ShapeDtypeStruct((g, S5_ROWS, S5_ROWS), _BF16),
                   jax.ShapeDtypeStruct((g, S5_NSTATE, S5_ROWS), _BF16),
                   jax.ShapeDtypeStruct((g, S5_ROWS, S5_NSTATE), _BF16),
                   jax.ShapeDtypeStruct((g, 4, p, LANES), _F32)],
        grid=(g // gq,),
        in_specs=[per_g(p, 2), per_g(2, LANES), per_g(1, 1), per_g(p, h), per_g(p, h),
                  per_g(h, p), per_g(h, p), per_g(h, LANES), per_g(h, LANES),
                  _resident(esel.shape), _resident(etile.shape), _resident(e2.shape), _resident(e3.shape),
                  _resident(pcol.shape)],
        out_specs=[pl.BlockSpec((gq, S5_ROWS, S5_ROWS), lambda i: (i, 0, 0)),
                   pl.BlockSpec((gq, S5_NSTATE, S5_ROWS), lambda i: (i, 0, 0)),
                   pl.BlockSpec((gq, S5_ROWS, S5_NSTATE), lambda i: (i, 0, 0)),
                   pl.BlockSpec((gq, 4, p, LANES), lambda i: (i, 0, 0, 0))],
        compiler_params=_cparams(("arbitrary",)),
        name="s5prep",
    )(acol, arow, ldt, b_re, b_im, c_re, c_im, c1, c2, esel, etile, e2, e3, pcol)


def _s5post_kernel(x_ref, yt_ref, mod_ref, g_ref, dsk_ref, gw_ref, gb_ref, o_ref, *, k, lanes):
    seg, off, mod_rows = lanes
    nb, lg = x_ref.shape[0], x_ref.shape[1]
    d = D_MODEL
    gn = g_ref[...]
    dsk = dsk_ref[...]
    c = math.sqrt(2.0 / math.pi)
    mods = [_mod_slices(mod_ref[mod_rows[b]])[:3] for b in range(nb)]
    ys = [yt_ref[:, 0, j * S5_GROUP:(j + 1) * S5_GROUP, :].astype(_F32).reshape(d, LANES).T
          for j in range(k)]
    rows = lg * k
    xs, zs = [], []
    for b in range(nb):
        sh1, sc1, _ = mods[b]
        r0 = b * seg + off
        yb = jnp.stack([ys[j][r0:r0 + lg] for j in range(k)], axis=0)
        yb = jnp.swapaxes(yb, 0, 1).reshape(rows, d)
        xf = x_ref[b].reshape(rows, d)
        ms = jnp.mean(xf * xf, axis=-1, keepdims=True)
        pre = (xf * lax.rsqrt(ms + EPS)) * ((gn * (1.0 + sc1)) * dsk) + (yb + sh1 * dsk)
        xs.append(xf)
        zs.append(pre + pre * jnp.tanh(pre * (c + (c * 0.044715) * (pre * pre))))
    z2 = jnp.concatenate(zs, axis=0)
    th = jnp.tanh(jnp.dot(z2.astype(_BF16), gw_ref[0], preferred_element_type=_F32) + gb_ref[...])
    for b in range(nb):
        q = (0.25 * mods[b][2]) * z2[b * rows:(b + 1) * rows]
        o_ref[b] = (xs[b] + (q * th[b * rows:(b + 1) * rows] + q)).reshape(lg, k, d)


def _s5post(x4, yt, mod, g, dsk, gw, layer, gb, lanes):
    b, lg, per, d = x4.shape
    k = S5_POST_TOK
    blocks_per_v = S5_CHUNK // k
    tok = pl.BlockSpec((b, lg, k, d), lambda r: (0, 0, r, 0))
    vec = pl.BlockSpec((1, d), lambda r: (0, 0))
    return pl.pallas_call(
        functools.partial(_s5post_kernel, k=k, lanes=lanes),
        out_shape=jax.ShapeDtypeStruct((b, lg, per, d), _F32),
        grid=(per // k,),
        in_specs=[tok,
                  pl.BlockSpec((S5_GROUPS, 1, k * S5_GROUP, LANES),
                               lambda r: (0, r // blocks_per_v, r % blocks_per_v, 0)),
                  pl.BlockSpec((8, 1, 6 * d), lambda r: (0, 0, 0)),
                  vec, vec, _resident_layer((d, d), layer), vec],
        out_specs=tok,
        compiler_params=_cparams(("arbitrary",)),
        name="s5post",
    )(x4, yt, mod, g, dsk, gw, gb)


def _mlp_kernel(x_ref, mod_ref, g_ref, w1_ref, b1_ref, w2_ref, b2_ref, fg_ref, o_ref, *, final):
    sh2, sc2, g2 = _mod_slices(mod_ref[0])[3:]
    xf = x_ref[0]
    xn = _rms_mod(xf, g_ref[...], sh2, sc2).astype(_BF16)
    acc = jnp.zeros(xf.shape, _F32)
    fc = D_MODEL
    for j in range(D_FF // fc):
        a = jnp.dot(xn, w1_ref[0, :, j * fc:(j + 1) * fc], preferred_element_type=_F32)
        a = jnp.maximum(a + b1_ref[:, j * fc:(j + 1) * fc], 0.0)
        acc = acc + jnp.dot((a * a).astype(_BF16), w2_ref[0, j * fc:(j + 1) * fc, :],
                            preferred_element_type=_F32)
    out = xf + g2 * (acc + b2_ref[...])
    if final:
        ms = jnp.mean(out * out, axis=-1, keepdims=True)
        out = out * lax.rsqrt(ms + EPS) * fg_ref[...]
    o_ref[0] = out


def _mlp(x, mod, mod_row, g, w1, b1, w2, b2, layer, fg, tt, final):
    b, n, d = x.shape
    row = (lambda bi: bi) if mod_row is None else (lambda bi: mod_row)
    tok = pl.BlockSpec((1, tt, d), lambda bi, t: (bi, t, 0))
    vec = pl.BlockSpec((1, d), lambda bi, t: (0, 0))
    return pl.pallas_call(
        functools.partial(_mlp_kernel, final=final),
        out_shape=jax.ShapeDtypeStruct((b, n, d), _F32),
        grid=(b, n // tt),
        in_specs=[tok, pl.BlockSpec((1, 1, 6 * d), lambda bi, t: (row(bi), 0, 0)), vec,
                  _resident_layer((d, D_FF), layer), pl.BlockSpec((1, D_FF), lambda bi, t: (0, 0)),
                  _resident_layer((D_FF, d), layer), vec, vec],
        out_specs=tok,
        compiler_params=_cparams(("arbitrary", "arbitrary")),
        name="mlp_final" if final else "mlp",
    )(x, mod, g, w1, b1, w2, b2, fg)


def _window_in_rows(w):
    need = POOL_TILE_ROWS + w - 1
    return -(-need // POOL_TILE_ROWS) * POOL_TILE_ROWS


def _grid_window_matrices():
    to = np.arange(POOL_TILE_ROWS * GRID_W)
    ro, co = to // GRID_W, to % GRID_W
    mats = []
    for w in POOL_WINDOWS:
        ti = np.arange(_window_in_rows(w) * GRID_W)
        ri, ci = ti // GRID_W - w // 2, ti % GRID_W
        m = ((ri[None, :] >= (ro - w // 2)[:, None]) & (ri[None, :] < (ro + w - w // 2)[:, None])
             & (ci[None, :] >= (co - w // 2)[:, None]) & (ci[None, :] < (co + w - w // 2)[:, None]))
        mats.append(jnp.asarray(m, dtype=_BF16))
    return mats


def _grid_inverse_counts(rows):
    t = np.arange(rows * GRID_W)
    r, c = t // GRID_W, t % GRID_W
    tabs = []
    for w in POOL_WINDOWS:
        rc = np.minimum(r + w - w // 2, rows) - np.maximum(r - w // 2, 0)
        cc = np.minimum(c + w - w // 2, GRID_W) - np.maximum(c - w // 2, 0)
        tabs.append(1.0 / (rc * cc))
    return jnp.asarray(np.stack(tabs), dtype=_F32)


def _seq_window_matrices(n):
    t = np.arange(n)
    mats = []
    for w in POOL_WINDOWS:
        lo, hi = np.maximum(t - w // 2, 0), np.minimum(t + w - w // 2, n)
        mats.append((t[None, :] >= lo[:, None]) & (t[None, :] < hi[:, None]))
    return jnp.asarray(np.stack(mats), dtype=_BF16)


POOL_BLOCK_ROWS = 16
POOL_HALO_ROWS = 8
POOL_TILE_ROWS = 4


def _pool_kernel(xm_ref, xb_ref, mod_ref, g_ref, a2_ref, a4_ref, a8_ref, a16_ref, pw_ref, ps_ref, inv_ref,
                 o_ref, tail):
    i = pl.program_id(0)
    b = pl.program_id(1)
    nblk = pl.num_programs(0)
    sh1, sc1, g1 = _mod_slices(mod_ref[0])[:3]
    gn = g_ref[...]
    halo = POOL_HALO_ROWS * GRID_W
    main = POOL_BLOCK_ROWS * GRID_W
    tile = POOL_TILE_ROWS * GRID_W
    xm = xm_ref[0]
    xn_m = _rms_mod(xm, gn, sh1, sc1)
    xn_mb = xn_m.astype(_BF16)
    @pl.when(i == 0)
    def _():
        tail[b] = jnp.zeros((halo, D_MODEL), _BF16)

    xn_t = tail[b]
    xn_b = jnp.where(i < nblk - 1, _rms_mod(xb_ref[0], gn, sh1, sc1), 0.0).astype(_BF16)
    tail[b] = xn_mb[main - halo:]
    xn_ext = jnp.concatenate([xn_t, xn_mb, xn_b], axis=0)
    for gi, (w, a_ref) in enumerate(zip(POOL_WINDOWS, (a2_ref, a4_ref, a8_ref, a16_ref))):
        ch = slice(gi * POOL_CH, (gi + 1) * POOL_CH)
        a = a_ref[...]
        span = a.shape[1]
        tots = []
        for k in range(main // tile):
            start = halo + k * tile - (w // 2) * GRID_W
            tots.append(jnp.dot(a, xn_ext[start:start + span, ch], preferred_element_type=_F32))
        tot = jnp.concatenate(tots, axis=0)
        inv = inv_ref[gi]
        p = tot * jnp.concatenate([inv] * (POOL_CH // LANES), axis=1) - xn_m[:, ch]
        y = jnp.dot(p.astype(_BF16), pw_ref[0, gi], preferred_element_type=_F32) * ps_ref[:, ch]
        o_ref[0, :, ch] = xm[:, ch] + g1[:, ch] * y


def _pool_grid(x, mod, g, pw, layer, ps):
    b, n, d = x.shape
    rows = n // GRID_W
    main = POOL_BLOCK_ROWS * GRID_W
    halo = POOL_HALO_ROWS * GRID_W
    nblk = n // main
    ratio = main // halo
    nh = n // halo
    amats = _grid_window_matrices()
    assert all(w // 2 <= POOL_HALO_ROWS and _window_in_rows(w) - w // 2 <= POOL_TILE_ROWS + POOL_HALO_ROWS
               for w in POOL_WINDOWS)
    inv = jnp.broadcast_to(_grid_inverse_counts(rows)[:, :, None], (len(POOL_WINDOWS), n, LANES))
    vec = pl.BlockSpec((1, d), lambda i, bi: (0, 0))
    return pl.pallas_call(
        _pool_kernel,
        out_shape=jax.ShapeDtypeStruct((b, n, d), _F32),
        grid=(nblk, b),
        in_specs=[pl.BlockSpec((1, main, d), lambda i, bi: (bi, i, 0)),
                  pl.BlockSpec((1, halo, d), lambda i, bi: (bi, jnp.minimum((i + 1) * ratio, nh - 1), 0)),
                  pl.BlockSpec((1, 1, 6 * d), lambda i, bi: (bi, 0, 0)),
                  vec,
                  *[_resident(a.shape) for a in amats],
                  _resident_layer((len(POOL_WINDOWS), POOL_CH, POOL_CH), layer),
                  vec,
                  pl.BlockSpec((len(POOL_WINDOWS), main, LANES), lambda i, bi: (0, i, 0))],
        out_specs=pl.BlockSpec((1, main, d), lambda i, bi: (bi, i, 0)),
        scratch_shapes=[pltpu.VMEM((b, halo, d), _BF16)],
        compiler_params=_cparams(("arbitrary", "arbitrary")),
        name="pool_grid",
    )(x, x, mod, g, *amats, pw, ps, inv)


def _poolseq_kernel(x_ref, mod_ref, g_ref, aseq_ref, pw_ref, ps_ref, o_ref):
    sh1, sc1, g1 = _mod_slices(mod_ref[0])[:3]
    xf = x_ref[0]
    n = xf.shape[0]
    xn = _rms_mod(xf, g_ref[...], sh1, sc1)
    xb = xn.astype(_BF16)
    t = lax.broadcasted_iota(jnp.int32, (n, POOL_CH), 0)
    for gi, w in enumerate(POOL_WINDOWS):
        ch = slice(gi * POOL_CH, (gi + 1) * POOL_CH)
        tot = jnp.dot(aseq_ref[gi], xb[:, ch], preferred_element_type=_F32)
        cnt = jnp.minimum(t + (w - w // 2), n) - jnp.maximum(t - w // 2, 0)
        p = tot / cnt.astype(_F32) - xn[:, ch]
        y = jnp.dot(p.astype(_BF16), pw_ref[0, gi], preferred_element_type=_F32) * ps_ref[:, ch]
        o_ref[0, :, ch] = xf[:, ch] + g1[:, ch] * y


def _pool_seq(x, mod, mod_row, g, pw, layer, ps):
    b, n, d = x.shape
    aseq = _seq_window_matrices(n)
    vec = pl.BlockSpec((1, d), lambda bi: (0, 0))
    return pl.pallas_call(
        _poolseq_kernel,
        out_shape=jax.ShapeDtypeStruct((b, n, d), _F32),
        grid=(b,),
        in_specs=[pl.BlockSpec((1, n, d), lambda bi: (bi, 0, 0)),
                  pl.BlockSpec((1, 1, 6 * d), lambda bi: (mod_row, 0, 0)),
                  vec,
                  _resident((len(POOL_WINDOWS), n, n)),
                  _resident_layer((len(POOL_WINDOWS), POOL_CH, POOL_CH), layer),
                  vec],
        out_specs=pl.BlockSpec((1, n, d), lambda bi: (bi, 0, 0)),
        compiler_params=_cparams(("arbitrary",)),
        name="pool_seq",
    )(x, mod, g, aseq, pw, ps)


def kernel(x, c, ctx, c_ctx, ada_w, ada_b, norm1_g, norm2_g, s5_a_re, s5_a_im, s5_log_dt, s5_b_re, s5_b_im, s5_c_re, s5_c_im, s5_d, s5_glu_w, s5_glu_b, pool_w, pool_scale, mlp_w1, mlp_b1, mlp_w2, mlp_b2, final_g):
    bsz, n_tok, d = x.shape
    n_ctx = ctx.shape[1]
    depth = ada_w.shape[0]
    assert d == D_MODEL and n_tok % (POOL_BLOCK_ROWS * GRID_W) == 0
    assert LANES % bsz == 0 and bsz * CTX_SEG <= LANES and bsz < 8
    lat_chunks = n_tok // S5_CHUNK
    ctx_chunks = n_ctx // S5_CHUNK
    assert n_tok % S5_CHUNK == 0 and n_ctx % S5_CHUNK == 0 and (bsz * lat_chunks) % LANES == 0
    assert CTX_OFF >= 2 and CTX_OFF + ctx_chunks + 2 <= CTX_SEG
    nv = bsz * lat_chunks // LANES
    lg = LANES // bsz
    pows = _s5_pows(nv, lg)
    lat_lanes = (lg, 0, tuple(range(bsz)))
    ctx_lanes = (CTX_SEG, CTX_OFF, (bsz,) * bsz)
    last_ctx_reader = ((depth - 1) // N_MIXERS) * N_MIXERS
    ctx_row = bsz
    tt = min(1024, n_tok)
    tt_ctx = min(512, n_ctx)

    cc = jnp.zeros((8, d), _F32).at[:bsz].set(c).at[ctx_row].set(c_ctx)
    mods = _modulation(cc, ada_w, ada_b)
    w1 = mlp_w1.astype(_BF16)
    w2 = mlp_w2.astype(_BF16)
    glu_w4 = (0.25 * s5_glu_w).astype(_BF16)
    pw = pool_w.astype(_BF16)

    h_ctx = ctx
    for i in range(depth):
        ctx_in = i <= last_ctx_reader
        ctx_out = i < last_ctx_reader
        j = i // N_MIXERS
        mod = mods[i].reshape(8, 1, 6 * d)
        g1n = norm1_g[i].reshape(1, d)
        g2n = norm2_g[i].reshape(1, d)
        if i % N_MIXERS == 0:
            mt, bst, cst, lampow = _s5prep(s5_a_re[j], s5_a_im[j], s5_log_dt[j], s5_b_re[j], s5_b_im[j],
                                           s5_c_re[j], s5_c_im[j], pows)
            dsk = s5_d[j].reshape(1, d)
            gb2 = 0.5 * s5_glu_b[j].reshape(1, d)
            c4 = h_ctx.reshape(bsz, ctx_chunks, S5_CHUNK, d)
            if ctx_in:
                utc = _s5pre(c4, mod, g1n, ctx_lanes).reshape(S5_GROUPS, S5_ROWS, LANES)
            else:
                utc = jnp.zeros((S5_GROUPS, S5_ROWS, LANES), _BF16)
            x4 = x.reshape(bsz, lg, n_tok // lg, d)
            utl = _s5pre(x4, mod, g1n, lat_lanes)
            ytl, ytc = _s5core(utl, utc, mt, bst, cst, lampow, bsz, ctx_chunks, pows, ctx_out)
            x = _s5post(x4, ytl, mod, g1n, dsk, glu_w4, j, gb2, lat_lanes).reshape(bsz, n_tok, d)
            if ctx_out:
                ytc4 = ytc.reshape(S5_GROUPS, 1, S5_ROWS, LANES)
                h_ctx = _s5post(c4, ytc4, mod, g1n, dsk, glu_w4, j, gb2, ctx_lanes).reshape(bsz, n_ctx, d)
        else:
            ps = pool_scale[j].reshape(1, d)
            x = _pool_grid(x, mod, g1n, pw, j, ps)
            if ctx_out:
                h_ctx = _pool_seq(h_ctx, mod, ctx_row, g1n, pw, j, ps)
        final = i == depth - 1
        x = _mlp(x, mod, None, g2n, w1, mlp_b1[i].reshape(1, D_FF), w2, mlp_b2[i].reshape(1, d), i,
                 final_g.reshape(1, d), tt, final)
        if ctx_out:
            h_ctx = _mlp(h_ctx, mod, ctx_row, g2n, w1, mlp_b1[i].reshape(1, D_FF), w2,
                         mlp_b2[i].reshape(1, d), i, final_g.reshape(1, d), tt_ctx, False)
    return x
```
